```python
import math
import jax, jax.numpy as jnp
from jax import lax
import numpy as np

D_MODEL = 1024
BATCH = 2
SEQ = 8192
DEPTH = 2

M_HEADS = 4
M_QK_DIM = 128
M_V_DIM = 256
M_QK_WIDTH = M_HEADS * M_QK_DIM
M_V_WIDTH = M_HEADS * M_V_DIM
M_CHUNK = 64
CONV_K = 4
A_Q_HEADS = 16
A_KV_HEADS = 4
A_HEAD_DIM = 64
A_GROUP = A_Q_HEADS // A_KV_HEADS
A_Q_WIDTH = A_Q_HEADS * A_HEAD_DIM
A_KV_WIDTH = A_KV_HEADS * A_HEAD_DIM
WINDOW = 128
N_BUCKETS = 32
MAX_DISTANCE = 128
D_FF = 4 * D_MODEL
EPS = 1e-6

IN_SIZES = (2 * M_QK_WIDTH, M_V_WIDTH, M_V_WIDTH, M_HEADS, M_HEADS,
            A_Q_WIDTH, A_KV_WIDTH, A_KV_WIDTH, D_MODEL, D_MODEL)
N_IN = sum(IN_SIZES)

kernel_name = "hybrid_mlstm_swa_sink_t5bias_sqrelu"


def rmsnorm(x, g):
    xf = x.astype(jnp.float32)
    y = xf * lax.rsqrt(jnp.mean(xf * xf, axis=-1, keepdims=True) + EPS)
    return (y * g.astype(jnp.float32)).astype(x.dtype)


def t5_bucket(n):
    max_exact = N_BUCKETS // 2
    n = np.maximum(n, 0)
    large = max_exact + (np.log(np.maximum(n, 1) / max_exact)
                         / np.log(MAX_DISTANCE / max_exact)
                         * (N_BUCKETS - max_exact)).astype(np.int32)
    large = np.minimum(large, N_BUCKETS - 1)
    return np.where(n < max_exact, n, large).astype(np.int32)


def causal_conv(x, w, b):
    S = x.shape[1]
    xp = jnp.pad(x, ((0, 0), (CONV_K - 1, 0), (0, 0)))
    y = b
    for j in range(CONV_K):
        y = y + xp[:, j:j + S] * w[j]
    return y


def mlstm_chunkwise(q, k, v, ig, logf):
    B, S, H, dk = q.shape
    dv = v.shape[-1]
    L = M_CHUNK
    NC = S // L

    def chunks4(t):
        return t.reshape(B, NC, L, H, t.shape[-1]).transpose(1, 0, 3, 2, 4)

    def chunks3(t):
        return t.reshape(B, NC, L, H).transpose(1, 0, 3, 2)

    tril = jnp.tril(jnp.ones((L, L), dtype=bool))

    def step(carry, xs):
        C, n, m = carry
        qc, kc, vc, ic, fc = xs
        b = jnp.cumsum(fc, axis=-1)
        dmat = jnp.where(tril, b[..., :, None] - b[..., None, :] + ic[..., None, :], -jnp.inf)
        m_inter = b + m[..., None]
        m_comb = jnp.maximum(m_inter, jnp.max(dmat, axis=-1))
        w_intra = jnp.exp(dmat - m_comb[..., None])
        w_inter = jnp.exp(m_inter - m_comb)
        s = jnp.einsum('bhld,bhsd->bhls', qc, kc) * w_intra
        num = (jnp.einsum('bhls,bhsv->bhlv', s, vc)
               + w_inter[..., None] * jnp.einsum('bhld,bhdv->bhlv', qc, C))
        den = jnp.sum(s, axis=-1) + w_inter * jnp.einsum('bhld,bhd->bhl', qc, n)
        h = num / jnp.maximum(jnp.abs(den), jnp.exp(-m_comb))[..., None]
        bL = b[..., -1]
        g = bL[..., None] - b + ic
        m_new = jnp.maximum(bL + m, jnp.max(g, axis=-1))
        decay = jnp.exp(bL + m - m_new)
        wk = jnp.exp(g - m_new[..., None])
        C_new = decay[..., None, None] * C + jnp.einsum('bhl,bhld,bhlv->bhdv', wk, kc, vc)
        n_new = decay[..., None] * n + jnp.einsum('bhl,bhld->bhd', wk, kc)
        return (C_new, n_new, m_new), h

    init = (jnp.zeros((B, H, dk, dv), jnp.float32),
            jnp.zeros((B, H, dk), jnp.float32),
            jnp.zeros((B, H), jnp.float32))
    _, hs = lax.scan(step, init, (chunks4(q), chunks4(k), chunks4(v), chunks3(ig), chunks3(logf)))
    return hs.transpose(1, 0, 3, 2, 4).reshape(B, S, H, dv)


def sliding_window_attention(q, k, v, sinks, rel_bias):
    B, S = q.shape[:2]
    W = WINDOW
    NB = S // W
    qb = q.astype(jnp.float32).reshape(B, NB, W, A_KV_HEADS, A_GROUP, A_HEAD_DIM)
    kb = k.astype(jnp.float32).reshape(B, NB, W, A_KV_HEADS, A_HEAD_DIM)
    vb = v.astype(jnp.float32).reshape(B, NB, W, A_KV_HEADS, A_HEAD_DIM)

    def with_prev(t):
        prev = jnp.concatenate([jnp.zeros_like(t[:, :1]), t[:, :-1]], axis=1)
        return jnp.concatenate([prev, t], axis=2)

    k_ctx, v_ctx = with_prev(kb), with_prev(vb)
    scores = jnp.einsum('bnqhgd,bnkhd->bnhgqk', qb, k_ctx) * (A_HEAD_DIM ** -0.5)

    dist = np.arange(W)[:, None] + W - np.arange(2 * W)[None, :]
    bias = rel_bias.astype(jnp.float32)[t5_bucket(dist)]
    bias = bias.transpose(2, 0, 1).reshape(A_KV_HEADS, A_GROUP, W, 2 * W)
    valid_rel = jnp.asarray((dist >= 0) & (dist < W))
    kpos = jnp.arange(NB)[:, None, None] * W - W + jnp.arange(2 * W)[None, None, :]
    valid = valid_rel[None] & (kpos >= 0)
    logits = jnp.where(valid[None, :, None, None], scores + bias, -jnp.inf)

    sink = sinks.astype(jnp.float32).reshape(A_KV_HEADS, A_GROUP, 1)
    m = jnp.maximum(jnp.max(logits, axis=-1), sink)
    p = jnp.exp(logits - m[..., None])
    denom = jnp.sum(p, axis=-1) + jnp.exp(sink - m)
    out = jnp.einsum('bnhgqk,bnkhd->bnqhgd', p / denom[..., None], v_ctx)
    return out.reshape(B, S, A_Q_WIDTH).astype(q.dtype)


def mixer_block(h, w_in, conv_w, conv_b, b_igate, b_fgate, mlstm_norm_g, sinks,
                rel_bias, w_branch_m, w_branch_a, w_out):
    B, S, _ = h.shape
    proj = h @ w_in
    offsets = [int(o) for o in np.cumsum(IN_SIZES)[:-1]]
    mqk, mv, mo, mi, mf, aq, ak, av, ga, gb = jnp.split(proj, offsets, axis=-1)

    mqk = jax.nn.silu(causal_conv(mqk, conv_w, conv_b))
    mq, mk = jnp.split(mqk.astype(jnp.float32), 2, axis=-1)
    mq = mq.reshape(B, S, M_HEADS, M_QK_DIM) * (M_QK_DIM ** -0.5)
    mk = mk.reshape(B, S, M_HEADS, M_QK_DIM)
    mvf = mv.astype(jnp.float32).reshape(B, S, M_HEADS, M_V_DIM)
    ig = mi.astype(jnp.float32) + b_igate.astype(jnp.float32)
    logf = jax.nn.log_sigmoid(mf.astype(jnp.float32) + b_fgate.astype(jnp.float32))
    hm = mlstm_chunkwise(mq, mk, mvf, ig, logf)
    hm = hm * lax.rsqrt(jnp.mean(hm * hm, axis=-1, keepdims=True) + EPS)
    hm = hm * mlstm_norm_g.astype(jnp.float32).reshape(M_HEADS, M_V_DIM)
    hm = (hm.reshape(B, S, M_V_WIDTH) * jax.nn.sigmoid(mo.astype(jnp.float32))).astype(h.dtype)

    ha = sliding_window_attention(aq.reshape(B, S, A_Q_HEADS, A_HEAD_DIM),
                                  ak.reshape(B, S, A_KV_HEADS, A_HEAD_DIM),
                                  av.reshape(B, S, A_KV_HEADS, A_HEAD_DIM),
                                  sinks, rel_bias)

    y = jax.nn.sigmoid(ga) * (hm @ w_branch_m) + jax.nn.sigmoid(gb) * (ha @ w_branch_a)
    return y @ w_out


def sqrelu_mlp(h, w_up, w_down):
    return jnp.square(jax.nn.relu(h @ w_up)) @ w_down


def setup_inputs(seed: int = 0) -> dict:
    key = jax.random.key(seed)
    ks = jax.random.split(key, 20)
    f32 = jnp.float32
    nrm = lambda k, shape, s: jax.random.normal(k, shape, f32) * s
    return {
        "x": nrm(ks[0], (BATCH, SEQ, D_MODEL), 1.0),
        "norm_mix_g": 1.0 + nrm(ks[1], (DEPTH, D_MODEL), 0.05),
        "w_in": nrm(ks[2], (DEPTH, D_MODEL, N_IN), D_MODEL ** -0.5),
        "conv_w": nrm(ks[3], (DEPTH, CONV_K, 2 * M_QK_WIDTH), CONV_K ** -0.5),
        "conv_b": nrm(ks[4], (DEPTH, 2 * M_QK_WIDTH), 0.02),
        "b_igate": nrm(ks[5], (DEPTH, M_HEADS), 0.1),
        "b_fgate": jnp.linspace(3.0, 6.0, M_HEADS, dtype=f32)[None] + nrm(ks[6], (DEPTH, M_HEADS), 0.1),
        "mlstm_norm_g": 1.0 + nrm(ks[7], (DEPTH, M_V_WIDTH), 0.05),
        "attn_sinks": nrm(ks[8], (DEPTH, A_Q_HEADS), 0.5),
        "rel_bias": nrm(ks[9], (N_BUCKETS, A_Q_HEADS), 0.5),
        "w_branch_m": nrm(ks[10], (DEPTH, M_V_WIDTH, D_MODEL), M_V_WIDTH ** -0.5),
        "w_branch_a": nrm(ks[11], (DEPTH, A_Q_WIDTH, D_MODEL), A_Q_WIDTH ** -0.5),
        "w_out": nrm(ks[12], (DEPTH, D_MODEL, D_MODEL), D_MODEL ** -0.5),
        "norm_mlp_g": 1.0 + nrm(ks[13], (DEPTH, D_MODEL), 0.05),
        "w_up": nrm(ks[14], (DEPTH, D_MODEL, D_FF), D_MODEL ** -0.5),
        "w_down": nrm(ks[15], (DEPTH, D_FF, D_MODEL), D_FF ** -0.5),
        "final_norm_g": 1.0 + nrm(ks[16], (D_MODEL,), 0.05),
    }


def reference(x, norm_mix_g, w_in, conv_w, conv_b, b_igate, b_fgate, mlstm_norm_g,
              attn_sinks, rel_bias, w_branch_m, w_branch_a, w_out, norm_mlp_g,
              w_up, w_down, final_norm_g):
    for l in range(DEPTH):
        h = rmsnorm(x, norm_mix_g[l])
        x = x + mixer_block(h, w_in[l], conv_w[l], conv_b[l], b_igate[l], b_fgate[l],
                            mlstm_norm_g[l], attn_sinks[l], rel_bias,
                            w_branch_m[l], w_branch_a[l], w_out[l])
        h = rmsnorm(x, norm_mlp_g[l])
        x = x + sqrelu_mlp(h, w_up[l], w_down[l])
    return rmsnorm(x, final_norm_g)
```

```python
import functools

import numpy as np
import jax
import jax.numpy as jnp
from jax import lax
from jax.experimental import pallas as pl
from jax.experimental.pallas import tpu as pltpu

D_MODEL = 1024
DEPTH = 2
M_HEADS = 4
M_QK_DIM = 128
M_V_DIM = 256
M_QK_WIDTH = M_HEADS * M_QK_DIM
M_V_WIDTH = M_HEADS * M_V_DIM
CONV_K = 4
A_Q_HEADS = 16
A_KV_HEADS = 4
A_HEAD_DIM = 64
A_GROUP = A_Q_HEADS // A_KV_HEADS
A_Q_WIDTH = A_Q_HEADS * A_HEAD_DIM
A_KV_WIDTH = A_KV_HEADS * A_HEAD_DIM
WINDOW = 128
N_BUCKETS = 32
MAX_DISTANCE = 128
D_FF = 4 * D_MODEL
EPS = 1e-6
IN_SIZES = (2 * M_QK_WIDTH, M_V_WIDTH, M_V_WIDTH, M_HEADS, M_HEADS,
            A_Q_WIDTH, A_KV_WIDTH, A_KV_WIDTH, D_MODEL, D_MODEL)

PROJ_TM = 256
MLSTM_CHUNK = 128
CONV_HALO = 8
NEG_BIG = -1e30
VMEM_LIMIT_BYTES = 56 * 1024 * 1024

F32 = jnp.float32
BF16 = jnp.bfloat16


def _const_spec(shape):
    nd = len(shape)
    return pl.BlockSpec(shape, lambda *_: (0,) * nd, pipeline_mode=pl.Buffered(1))


def _rmsnorm(x, g):
    return x * lax.rsqrt(jnp.mean(x * x, axis=-1, keepdims=True) + EPS) * g


def _inproj_kernel(x_ref, g_ref, w_qk, w_v, w_o, w_g, w_aq, w_ak, w_av, w_ga, w_gb,
                   o_qk, o_v, o_o, o_g, o_aq, o_ak, o_av, o_ga, o_gb):
    h = _rmsnorm(x_ref[...], g_ref[...]).astype(BF16)
    for w, o in ((w_qk, o_qk), (w_v, o_v), (w_o, o_o), (w_aq, o_aq), (w_ak, o_ak),
                 (w_av, o_av), (w_ga, o_ga), (w_gb, o_gb)):
        o[...] = jnp.dot(h, w[...], preferred_element_type=F32).astype(o.dtype)
    o_g[...] = jnp.dot(h, w_g[...], preferred_element_type=F32)[:, :2 * M_HEADS]


def _inproj(x2, g, ws):
    T = x2.shape[0]
    tm = PROJ_TM
    row = lambda n: pl.BlockSpec((tm, n), lambda i: (i, 0))
    out_widths = (2 * M_QK_WIDTH, M_V_WIDTH, M_V_WIDTH, 2 * M_HEADS, A_Q_WIDTH,
                  A_KV_WIDTH, A_KV_WIDTH, D_MODEL, D_MODEL)
    out_dtypes = (F32, BF16, F32, F32, BF16, BF16, BF16, F32, F32)
    return pl.pallas_call(
        _inproj_kernel,
        grid=(T // tm,),
        in_specs=[row(D_MODEL), _const_spec((1, D_MODEL))] + [_const_spec(w.shape) for w in ws],
        out_specs=[row(n) for n in out_widths],
        out_shape=[jax.ShapeDtypeStruct((T, n), d) for n, d in zip(out_widths, out_dtypes)],
        compiler_params=pltpu.CompilerParams(
            dimension_semantics=("parallel",), vmem_limit_bytes=VMEM_LIMIT_BYTES),
        name="inproj",
    )(x2, g, *ws)


def _mlstm_kernel(qk_ref, v_ref, o_ref, grow_ref, cw_ref, cb_ref, gb_ref, ng_ref,
                  out_ref, xs_ref, c_ref, n_ref, m_ref):
    L = MLSTM_CHUNK
    c = pl.program_id(1)

    @pl.when(c == 0)
    def _():
        c_ref[...] = jnp.zeros_like(c_ref)
        n_ref[...] = jnp.zeros_like(n_ref)
        m_ref[...] = jnp.zeros_like(m_ref)
        xs_ref[0:CONV_HALO, :] = jnp.zeros((CONV_HALO, 2 * M_QK_WIDTH), F32)

    @pl.when(c > 0)
    def _():
        xs_ref[0:CONV_HALO, :] = xs_ref[L:L + CONV_HALO, :]

    xs_ref[CONV_HALO:CONV_HALO + L, :] = qk_ref[...].astype(F32)
    y = cb_ref[...]
    for j in range(CONV_K):
        off = CONV_HALO - (CONV_K - 1) + j
        y = y + xs_ref[off:off + L, :] * cw_ref[j:j + 1, :]
    y = y * jax.nn.sigmoid(y)

    row_i = lax.broadcasted_iota(jnp.int32, (L, L), 0)
    col_j = lax.broadcasted_iota(jnp.int32, (L, L), 1)
    tril = col_j <= row_i
    eye = col_j == row_i
    triu_f = (row_i <= col_j).astype(F32)

    g_rows = grow_ref[...] + gb_ref[...]
    logf_rows = jax.nn.log_sigmoid(g_rows)
    b_rows = jnp.dot(logf_rows, triu_f, preferred_element_type=F32,
                     precision=lax.Precision.HIGHEST)

    for h in range(M_HEADS):
        q = (y[:, h * M_QK_DIM:(h + 1) * M_QK_DIM] * (M_QK_DIM ** -0.5))
        k = y[:, M_QK_WIDTH + h * M_QK_DIM:M_QK_WIDTH + (h + 1) * M_QK_DIM]
        v = v_ref[:, h * M_V_DIM:(h + 1) * M_V_DIM]
        qb = q.astype(BF16)
        kb = k.astype(BF16)

        m_prev = m_ref[h][:, 0:1]
        n_prev = n_ref[h]
        c_prev = c_ref[h]

        logf_row = logf_rows[M_HEADS + h:M_HEADS + h + 1, :]
        b_row = b_rows[M_HEADS + h:M_HEADS + h + 1, :]
        a_row = g_rows[h:h + 1, :] - b_row
        b_col = jnp.sum(jnp.where(tril, logf_row, 0.0), axis=-1, keepdims=True)
        a_col = jnp.sum(jnp.where(eye, a_row, 0.0), axis=-1, keepdims=True)
        b_last = jnp.sum(logf_row, axis=-1, keepdims=True)

        a_mat = jnp.where(tril, a_row, -jnp.inf)
        m_row = jnp.maximum(m_prev, jnp.max(a_mat, axis=-1, keepdims=True))
        w_intra = jnp.exp(a_mat - m_row)
        w_inter = jnp.exp(m_prev - m_row)

        s = lax.dot_general(qb, kb, (((1,), (1,)), ((), ())),
                            preferred_element_type=F32) * w_intra
        num = (jnp.dot(s.astype(BF16), v, preferred_element_type=F32)
               + w_inter * jnp.dot(qb, c_prev.astype(BF16), preferred_element_type=F32))
        den = (jnp.sum(s, axis=-1, keepdims=True)
               + w_inter * jnp.sum(q * n_prev, axis=-1, keepdims=True))
        hh = num / jnp.maximum(jnp.abs(den), jnp.exp(-(b_col + m_row)))

        m_last = jnp.maximum(m_prev, jnp.max(a_row, axis=-1, keepdims=True))
        decay = jnp.exp(m_prev - m_last)
        kw = k * jnp.exp(a_col - m_last)
        c_ref[h] = decay * c_prev + lax.dot_general(
            kw.astype(BF16), v, (((0,), (0,)), ((), ())), preferred_element_type=F32)
        n_ref[h] = decay * n_prev + jnp.sum(kw, axis=0, keepdims=True)
        m_ref[h] = jnp.broadcast_to(b_last + m_last, (1, 128))

        sl = slice(h * M_V_DIM, (h + 1) * M_V_DIM)
        hn = hh * lax.rsqrt(jnp.mean(hh * hh, axis=-1, keepdims=True) + EPS) * ng_ref[:, sl]
        out_ref[:, sl] = (hn * jax.nn.sigmoid(o_ref[:, sl])).astype(out_ref.dtype)


def _mlstm(mqk, mv, mo, g_rows, conv_w, conv_b, gate_b, norm_g):
    B, S, _ = mqk.shape
    L = MLSTM_CHUNK
    tok = lambda n: pl.BlockSpec((None, L, n), lambda b, c: (b, c, 0))
    return pl.pallas_call(
        _mlstm_kernel,
        grid=(B, S // L),
        in_specs=[tok(2 * M_QK_WIDTH), tok(M_V_WIDTH), tok(M_V_WIDTH),
                  pl.BlockSpec((None, 2 * M_HEADS, L), lambda b, c: (b, 0, c)),
                  _const_spec((CONV_K, 2 * M_QK_WIDTH)), _const_spec((1, 2 * M_QK_WIDTH)),
                  _const_spec((2 * M_HEADS, 1)), _const_spec((1, M_V_WIDTH))],
        out_specs=tok(M_V_WIDTH),
        out_shape=jax.ShapeDtypeStruct((B, S, M_V_WIDTH), BF16),
        scratch_shapes=[pltpu.VMEM((CONV_HALO + L, 2 * M_QK_WIDTH), F32),
                        pltpu.VMEM((M_HEADS, M_QK_DIM, M_V_DIM), F32),
                        pltpu.VMEM((M_HEADS, 1, M_QK_DIM), F32),
                        pltpu.VMEM((M_HEADS, 1, 128), F32)],
        compiler_params=pltpu.CompilerParams(
            dimension_semantics=("arbitrary", "arbitrary"), vmem_limit_bytes=VMEM_LIMIT_BYTES),
        name="mlstm",
    )(mqk, mv, mo, g_rows, conv_w, conv_b, gate_b, norm_g)


def _swa_kernel(sink_ref, q_ref, kp_ref, kc_ref, vp_ref, vc_ref, bp_ref, bc_ref, out_ref):
    n = pl.program_id(1)
    prev_pen = jnp.where(n == 0, NEG_BIG, 0.0).astype(F32)
    nt = (((1,), (1,)), ((), ()))
    for hq in range(A_Q_HEADS):
        kv = hq // A_GROUP
        qs = slice(hq * A_HEAD_DIM, (hq + 1) * A_HEAD_DIM)
        ks = slice(kv * A_HEAD_DIM, (kv + 1) * A_HEAD_DIM)
        q = q_ref[:, qs]
        s_p = lax.dot_general(q, kp_ref[:, ks], nt, preferred_element_type=F32) + (bp_ref[hq] + prev_pen)
        s_c = lax.dot_general(q, kc_ref[:, ks], nt, preferred_element_type=F32) + bc_ref[hq]
        sink = sink_ref[hq]
        m = jnp.maximum(jnp.maximum(jnp.max(s_p, axis=-1, keepdims=True),
                                    jnp.max(s_c, axis=-1, keepdims=True)), sink)
        p_p = jnp.exp(s_p - m)
        p_c = jnp.exp(s_c - m)
        denom = (jnp.sum(p_p, axis=-1, keepdims=True) + jnp.sum(p_c, axis=-1, keepdims=True)
                 + jnp.exp(sink - m))
        o = (jnp.dot(p_p.astype(BF16), vp_ref[:, ks], preferred_element_type=F32)
             + jnp.dot(p_c.astype(BF16), vc_ref[:, ks], preferred_element_type=F32))
        out_ref[:, qs] = (o / denom).astype(out_ref.dtype)


def _swa(aq, ak, av, sinks, bias_prev, bias_cur):
    B, S, _ = aq.shape
    W = WINDOW
    cur = lambda n: pl.BlockSpec((None, W, n), lambda b, i: (b, i, 0))
    prev = lambda n: pl.BlockSpec((None, W, n), lambda b, i: (b, jnp.maximum(i - 1, 0), 0))
    return pl.pallas_call(
        _swa_kernel,
        grid=(B, S // W),
        in_specs=[pl.BlockSpec(memory_space=pltpu.SMEM),
                  cur(A_Q_WIDTH), prev(A_KV_WIDTH), cur(A_KV_WIDTH), prev(A_KV_WIDTH), cur(A_KV_WIDTH),
                  _const_spec((A_Q_HEADS, W, W)), _const_spec((A_Q_HEADS, W, W))],
        out_specs=cur(A_Q_WIDTH),
        out_shape=jax.ShapeDtypeStruct((B, S, A_Q_WIDTH), BF16),
        compiler_params=pltpu.CompilerParams(
            dimension_semantics=("parallel", "parallel"), vmem_limit_bytes=VMEM_LIMIT_BYTES),
        name="swa",
    )(sinks, aq, ak, ak, av, av, bias_prev, bias_cur)


def _merge_kernel(x_ref, hm_ref, ha_ref, ga_ref, gb_ref, wm_ref, wa_ref, wo_ref, out_ref):
    y = (jax.nn.sigmoid(ga_ref[...]) * jnp.dot(hm_ref[...], wm_ref[...], preferred_element_type=F32)
         + jax.nn.sigmoid(gb_ref[...]) * jnp.dot(ha_ref[...], wa_ref[...], preferred_element_type=F32))
    out_ref[...] = x_ref[...] + jnp.dot(y.astype(BF16), wo_ref[...], preferred_element_type=F32)


def _merge(x2, hm, ha, ga, gb, wm, wa, wo):
    T = x2.shape[0]
    tm = PROJ_TM
    row = pl.BlockSpec((tm, D_MODEL), lambda i: (i, 0))
    wspec = _const_spec((D_MODEL, D_MODEL))
    return pl.pallas_call(
        _merge_kernel,
        grid=(T // tm,),
        in_specs=[row, row, row, row, row, wspec, wspec, wspec],
        out_specs=row,
        out_shape=jax.ShapeDtypeStruct((T, D_MODEL), F32),
        compiler_params=pltpu.CompilerParams(
            dimension_semantics=("parallel",), vmem_limit_bytes=VMEM_LIMIT_BYTES),
        name="merge",
    )(x2, hm, ha, ga, gb, wm, wa, wo)


def _mlp_kernel(x_ref, g_ref, wu_ref, wd_ref, fg_ref, out_ref, *, final):
    x = x_ref[...]
    h = _rmsnorm(x, g_ref[...]).astype(BF16)
    u = jnp.maximum(jnp.dot(h, wu_ref[...], preferred_element_type=F32), 0.0)
    x = x + jnp.dot((u * u).astype(BF16), wd_ref[...], preferred_element_type=F32)
    if final:
        x = _rmsnorm(x, fg_ref[...])
    out_ref[...] = x


def _mlp(x2, g, wu, wd, fg, final):
    T = x2.shape[0]
    tm = PROJ_TM
    row = pl.BlockSpec((tm, D_MODEL), lambda i: (i, 0))
    return pl.pallas_call(
        functools.partial(_mlp_kernel, final=final),
        grid=(T // tm,),
        in_specs=[row, _const_spec((1, D_MODEL)), _const_spec((D_MODEL, D_FF)),
                  _const_spec((D_FF, D_MODEL)), _const_spec((1, D_MODEL))],
        out_specs=row,
        out_shape=jax.ShapeDtypeStruct((T, D_MODEL), F32),
        compiler_params=pltpu.CompilerParams(
            dimension_semantics=("parallel",), vmem_limit_bytes=VMEM_LIMIT_BYTES),
        name="mlp",
    )(x2, g, wu, wd, fg)


def _t5_bucket(n):
    max_exact = N_BUCKETS // 2
    n = np.maximum(n, 0)
    large = max_exact + (np.log(np.maximum(n, 1) / max_exact)
                         / np.log(MAX_DISTANCE / max_exact)
                         * (N_BUCKETS - max_exact)).astype(np.int32)
    large = np.minimum(large, N_BUCKETS - 1)
    return np.where(n < max_exact, n, large).astype(np.int32)


def _bias_tables(rel_bias):
    W = WINDOW
    dist = np.arange(W)[:, None] + W - np.arange(2 * W)[None, :]
    valid = (dist >= 0) & (dist < W)
    bias = rel_bias.astype(F32)[_t5_bucket(dist)]
    bias = jnp.where(jnp.asarray(valid)[:, :, None], bias, NEG_BIG)
    bias = bias.transpose(2, 0, 1)
    return bias[:, :, :W], bias[:, :, W:]


def _split_w_in(w):
    offs = np.cumsum((0,) + IN_SIZES)
    parts = [w[:, offs[i]:offs[i + 1]] for i in range(len(IN_SIZES))]
    wqk, wv, wo, wi, wf, waq, wak, wav, wga, wgb = parts
    wg = jnp.pad(jnp.concatenate([wi, wf], axis=1), ((0, 0), (0, 128 - 2 * M_HEADS)))
    waq = waq * (A_HEAD_DIM ** -0.5)
    return [p.astype(BF16) for p in (wqk, wv, wo, wg, waq, wak, wav, wga, wgb)]


def kernel(x, norm_mix_g, w_in, conv_w, conv_b, b_igate, b_fgate, mlstm_norm_g, attn_sinks,
           rel_bias, w_branch_m, w_branch_a, w_out, norm_mlp_g, w_up, w_down, final_norm_g):
    B, S, D = x.shape
    T = B * S
    bias_prev, bias_cur = _bias_tables(rel_bias)
    x2 = x.reshape(T, D)
    for l in range(DEPTH):
        ws = _split_w_in(w_in[l])
        mqk, mv, mo, gates, aq, ak, av, ga, gb = _inproj(x2, norm_mix_g[l].reshape(1, D), ws)
        g_rows = gates.reshape(B, S, 2 * M_HEADS).transpose(0, 2, 1)
        gate_b = jnp.concatenate([b_igate[l], b_fgate[l]]).reshape(2 * M_HEADS, 1)
        hm = _mlstm(mqk.reshape(B, S, -1), mv.reshape(B, S, -1), mo.reshape(B, S, -1), g_rows,
                    conv_w[l], conv_b[l].reshape(1, -1), gate_b, mlstm_norm_g[l].reshape(1, -1))
        ha = _swa(aq.reshape(B, S, -1), ak.reshape(B, S, -1), av.reshape(B, S, -1),
                  attn_sinks[l], bias_prev, bias_cur)
        x2 = _merge(x2, hm.reshape(T, -1), ha.reshape(T, -1), ga, gb,
                    w_branch_m[l].astype(BF16), w_branch_a[l].astype(BF16), w_out[l].astype(BF16))
        x2 = _mlp(x2, norm_mlp_g[l].reshape(1, D), w_up[l].astype(BF16), w_down[l].astype(BF16),
                  final_norm_g.reshape(1, D), final=(l == DEPTH - 1))
    return x2.reshape(B, S, D)
```

```python
import functools

import numpy as np
import jax
import jax.numpy as jnp
from jax import lax
from jax.experimental import pallas as pl
from jax.experimental.pallas import tpu as pltpu

D_MODEL = 1024
DEPTH = 2
M_HEADS = 4
M_QK_DIM = 128
M_V_DIM = 256
M_QK_WIDTH = M_HEADS * M_QK_DIM
M_V_WIDTH = M_HEADS * M_V_DIM
CONV_K = 4
A_Q_HEADS = 16
A_KV_HEADS = 4
A_HEAD_DIM = 64
A_GROUP = A_Q_HEADS // A_KV_HEADS
A_Q_WIDTH = A_Q_HEADS * A_HEAD_DIM
A_KV_WIDTH = A_KV_HEADS * A_HEAD_DIM
WINDOW = 128
N_BUCKETS = 32
MAX_DISTANCE = 128
D_FF = 4 * D_MODEL
EPS = 1e-6
IN_SIZES = (2 * M_QK_WIDTH, M_V_WIDTH, M_V_WIDTH, M_HEADS, M_HEADS,
            A_Q_WIDTH, A_KV_WIDTH, A_KV_WIDTH, D_MODEL, D_MODEL)

PROJ_TM = 256
MLSTM_CHUNK = 128
CONV_HALO = 8
SWA_BLOCKS = 2
NEG_BIG = -1e30
VMEM_LIMIT_BYTES = 56 * 1024 * 1024

F32 = jnp.float32
BF16 = jnp.bfloat16


def _const_spec(shape):
    nd = len(shape)
    return pl.BlockSpec(shape, lambda *_: (0,) * nd, pipeline_mode=pl.Buffered(1))


def _rmsnorm(x, g):
    return x * lax.rsqrt(jnp.mean(x * x, axis=-1, keepdims=True) + EPS) * g


def _inproj_kernel(x_ref, g_ref, w_qk, w_v, w_o, w_g, w_aq, w_ak, w_av, w_ga, w_gb,
                   o_qk, o_v, o_o, o_g, o_aq, o_ak, o_av, o_ga, o_gb):
    h = _rmsnorm(x_ref[...], g_ref[...]).astype(BF16)
    for w, o in ((w_qk, o_qk), (w_v, o_v), (w_o, o_o), (w_aq, o_aq), (w_ak, o_ak),
                 (w_av, o_av), (w_ga, o_ga), (w_gb, o_gb)):
        o[...] = jnp.dot(h, w[...], preferred_element_type=F32).astype(o.dtype)
    o_g[...] = jnp.dot(h, w_g[...], preferred_element_type=F32)[:, :2 * M_HEADS]


def _inproj(x2, g, ws):
    T = x2.shape[0]
    tm = PROJ_TM
    row = lambda n: pl.BlockSpec((tm, n), lambda i: (i, 0))
    out_widths = (2 * M_QK_WIDTH, M_V_WIDTH, M_V_WIDTH, 2 * M_HEADS, A_Q_WIDTH,
                  2 * A_KV_WIDTH, 2 * A_KV_WIDTH, D_MODEL, D_MODEL)
    out_dtypes = (F32, BF16, F32, F32, BF16, BF16, BF16, F32, F32)
    return pl.pallas_call(
        _inproj_kernel,
        grid=(T // tm,),
        in_specs=[row(D_MODEL), _const_spec((1, D_MODEL))] + [_const_spec(w.shape) for w in ws],
        out_specs=[row(n) for n in out_widths],
        out_shape=[jax.ShapeDtypeStruct((T, n), d) for n, d in zip(out_widths, out_dtypes)],
        compiler_params=pltpu.CompilerParams(
            dimension_semantics=("parallel",), vmem_limit_bytes=VMEM_LIMIT_BYTES),
        name="inproj",
    )(x2, g, *ws)


def _mlstm_kernel(qk_ref, v_ref, o_ref, grow_ref, cw_ref, cb_ref, gb_ref, ng_ref,
                  out_ref, xs_ref, c_ref, n_ref, m_ref):
    L = MLSTM_CHUNK
    c = pl.program_id(1)

    @pl.when(c == 0)
    def _():
        c_ref[...] = jnp.zeros_like(c_ref)
        n_ref[...] = jnp.zeros_like(n_ref)
        m_ref[...] = jnp.zeros_like(m_ref)
        xs_ref[0:CONV_HALO, :] = jnp.zeros((CONV_HALO, 2 * M_QK_WIDTH), F32)

    @pl.when(c > 0)
    def _():
        xs_ref[0:CONV_HALO, :] = xs_ref[L:L + CONV_HALO, :]

    xs_ref[CONV_HALO:CONV_HALO + L, :] = qk_ref[...].astype(F32)
    y = cb_ref[...]
    for j in range(CONV_K):
        off = CONV_HALO - (CONV_K - 1) + j
        y = y + xs_ref[off:off + L, :] * cw_ref[j:j + 1, :]
    y = y * jax.nn.sigmoid(y)

    row_i = lax.broadcasted_iota(jnp.int32, (L, L), 0)
    col_j = lax.broadcasted_iota(jnp.int32, (L, L), 1)
    tril = col_j <= row_i
    eye = col_j == row_i
    triu_f = (row_i <= col_j).astype(F32)

    g_rows = grow_ref[...] + gb_ref[...]
    logf_rows = jax.nn.log_sigmoid(g_rows)
    b_rows = jnp.dot(logf_rows, triu_f, preferred_element_type=F32,
                     precision=lax.Precision.HIGHEST)

    for h in range(M_HEADS):
        q = (y[:, h * M_QK_DIM:(h + 1) * M_QK_DIM] * (M_QK_DIM ** -0.5))
        k = y[:, M_QK_WIDTH + h * M_QK_DIM:M_QK_WIDTH + (h + 1) * M_QK_DIM]
        v = v_ref[:, h * M_V_DIM:(h + 1) * M_V_DIM]
        qb = q.astype(BF16)
        kb = k.astype(BF16)

        m_prev = m_ref[h][:, 0:1]
        n_prev = n_ref[h]
        c_prev = c_ref[h]

        logf_row = logf_rows[M_HEADS + h:M_HEADS + h + 1, :]
        b_row = b_rows[M_HEADS + h:M_HEADS + h + 1, :]
        a_row = g_rows[h:h + 1, :] - b_row
        b_col = jnp.sum(jnp.where(tril, logf_row, 0.0), axis=-1, keepdims=True)
        a_col = jnp.sum(jnp.where(eye, a_row, 0.0), axis=-1, keepdims=True)
        b_last = jnp.sum(logf_row, axis=-1, keepdims=True)

        a_mat = jnp.where(tril, a_row, -jnp.inf)
        m_row = jnp.maximum(m_prev, jnp.max(a_mat, axis=-1, keepdims=True))
        w_intra = jnp.exp(a_mat - m_row)
        w_inter = jnp.exp(m_prev - m_row)

        s = lax.dot_general(qb, kb, (((1,), (1,)), ((), ())),
                            preferred_element_type=F32) * w_intra
        num = (jnp.dot(s.astype(BF16), v, preferred_element_type=F32)
               + w_inter * jnp.dot(qb, c_prev.astype(BF16), preferred_element_type=F32))
        den = (jnp.sum(s, axis=-1, keepdims=True)
               + w_inter * jnp.sum(q * n_prev, axis=-1, keepdims=True))
        hh = num / jnp.maximum(jnp.abs(den), jnp.exp(-(b_col + m_row)))

        m_last = jnp.maximum(m_prev, jnp.max(a_row, axis=-1, keepdims=True))
        decay = jnp.exp(m_prev - m_last)
        kw = k * jnp.exp(a_col - m_last)
        c_ref[h] = decay * c_prev + lax.dot_general(
            kw.astype(BF16), v, (((0,), (0,)), ((), ())), preferred_element_type=F32)
        n_ref[h] = decay * n_prev + jnp.sum(kw, axis=0, keepdims=True)
        m_ref[h] = jnp.broadcast_to(b_last + m_last, (1, 128))

        sl = slice(h * M_V_DIM, (h + 1) * M_V_DIM)
        hn = hh * lax.rsqrt(jnp.mean(hh * hh, axis=-1, keepdims=True) + EPS) * ng_ref[:, sl]
        out_ref[:, sl] = (hn * jax.nn.sigmoid(o_ref[:, sl])).astype(out_ref.dtype)


def _mlstm(mqk, mv, mo, g_rows, conv_w, conv_b, gate_b, norm_g):
    B, S, _ = mqk.shape
    L = MLSTM_CHUNK
    tok = lambda n: pl.BlockSpec((None, L, n), lambda b, c: (b, c, 0))
    return pl.pallas_call(
        _mlstm_kernel,
        grid=(B, S // L),
        in_specs=[tok(2 * M_QK_WIDTH), tok(M_V_WIDTH), tok(M_V_WIDTH),
                  pl.BlockSpec((None, 2 * M_HEADS, L), lambda b, c: (b, 0, c)),
                  _const_spec((CONV_K, 2 * M_QK_WIDTH)), _const_spec((1, 2 * M_QK_WIDTH)),
                  _const_spec((2 * M_HEADS, 1)), _const_spec((1, M_V_WIDTH))],
        out_specs=tok(M_V_WIDTH),
        out_shape=jax.ShapeDtypeStruct((B, S, M_V_WIDTH), BF16),
        scratch_shapes=[pltpu.VMEM((CONV_HALO + L, 2 * M_QK_WIDTH), F32),
                        pltpu.VMEM((M_HEADS, M_QK_DIM, M_V_DIM), F32),
                        pltpu.VMEM((M_HEADS, 1, M_QK_DIM), F32),
                        pltpu.VMEM((M_HEADS, 1, 128), F32)],
        compiler_params=pltpu.CompilerParams(
            dimension_semantics=("arbitrary", "arbitrary"), vmem_limit_bytes=VMEM_LIMIT_BYTES),
        name="mlstm",
    )(mqk, mv, mo, g_rows, conv_w, conv_b, gate_b, norm_g)


def _swa_kernel(sink_ref, rel_ref, bucket_ref, q_ref, kp_ref, kc_ref, vp_ref, vc_ref,
                out_ref, bias_ref):
    W = WINDOW
    nb = SWA_BLOCKS
    i = pl.program_id(1)

    @pl.when((pl.program_id(0) == 0) & (i == 0))
    def _():
        bucket = bucket_ref[...]
        neg = jnp.full((W, 2 * W), NEG_BIG, F32)
        for hq in range(A_Q_HEADS):
            bias_ref[1, hq] = neg

        def fill(bb, carry):
            hit = bucket == bb
            for hq in range(A_Q_HEADS):
                bias_ref[1, hq] = jnp.where(hit, rel_ref[bb, hq], bias_ref[1, hq])
            return carry
        lax.fori_loop(0, N_BUCKETS, fill, 0)
        is_prev = lax.broadcasted_iota(jnp.int32, (W, 2 * W), 1) < W
        for hq in range(A_Q_HEADS):
            bias_ref[0, hq] = jnp.where(is_prev, NEG_BIG, bias_ref[1, hq])

    lo = lax.broadcasted_iota(jnp.int32, (W, 2 * A_HEAD_DIM), 1) < A_HEAD_DIM
    lo2 = lax.broadcasted_iota(jnp.int32, (2 * W, 2 * A_HEAD_DIM), 1) < A_HEAD_DIM
    zero = jnp.zeros((), BF16)
    ones_lo = jnp.where(lo2, 1.0, 0.0).astype(BF16)
    ones_hi = jnp.where(lo2, 0.0, 1.0).astype(BF16)
    nt = (((1,), (1,)), ((), ()))

    for blk in range(nb):
        rows = slice(blk * W, (blk + 1) * W)
        var = jnp.minimum(i, 1) if blk == 0 else 1
        for kv in range(A_KV_HEADS):
            ls = slice(kv * 2 * A_HEAD_DIM, (kv + 1) * 2 * A_HEAD_DIM)
            if blk == 0:
                k_prev, v_prev = kp_ref[:, ls], vp_ref[:, ls]
            else:
                k_prev, v_prev = kc_ref[(blk - 1) * W:blk * W, ls], vc_ref[(blk - 1) * W:blk * W, ls]
            k2 = jnp.concatenate([k_prev, kc_ref[rows, ls]], axis=0)
            v2 = jnp.concatenate([v_prev, vc_ref[rows, ls]], axis=0)
            k_even = jnp.where(lo2, k2, zero)
            k_odd = jnp.where(lo2, zero, k2)
            v_even = jnp.concatenate([jnp.where(lo2, v2, zero), ones_lo], axis=1)
            v_odd = jnp.concatenate([jnp.where(lo2, zero, v2), ones_hi], axis=1)
            for pair in range(A_GROUP // 2):
                pi = kv * (A_GROUP // 2) + pair
                cols = slice(pi * 2 * A_HEAD_DIM, (pi + 1) * 2 * A_HEAD_DIM)
                q2 = q_ref[rows, cols]
                sink_e = sink_ref[2 * pi]
                sink_o = sink_ref[2 * pi + 1]
                s_e = lax.dot_general(q2, k_even, nt, preferred_element_type=F32) + bias_ref[var, 2 * pi]
                s_o = lax.dot_general(q2, k_odd, nt, preferred_element_type=F32) + bias_ref[var, 2 * pi + 1]
                m_e = jnp.maximum(jnp.max(s_e, axis=-1, keepdims=True), sink_e)
                m_o = jnp.maximum(jnp.max(s_o, axis=-1, keepdims=True), sink_o)
                p_e = jnp.exp(s_e - m_e).astype(BF16)
                p_o = jnp.exp(s_o - m_o).astype(BF16)
                acc = (jnp.dot(p_e, v_even, preferred_element_type=F32)
                       + jnp.dot(p_o, v_odd, preferred_element_type=F32))
                sink_term = jnp.where(lo, jnp.exp(sink_e - m_e), jnp.exp(sink_o - m_o))
                out_ref[rows, cols] = (acc[:, :2 * A_HEAD_DIM]
                                       / (acc[:, 2 * A_HEAD_DIM:] + sink_term)).astype(out_ref.dtype)


def _swa(aq, ak2, av2, sinks, rel_bias, bucket):
    B, S, _ = aq.shape
    W = WINDOW
    tq = SWA_BLOCKS * W
    kvw = 2 * A_KV_WIDTH
    cur = lambda n: pl.BlockSpec((None, tq, n), lambda b, i: (b, i, 0))
    prev = pl.BlockSpec((None, W, kvw), lambda b, i: (b, jnp.maximum(i * SWA_BLOCKS - 1, 0), 0))
    smem = pl.BlockSpec(memory_space=pltpu.SMEM)
    return pl.pallas_call(
        _swa_kernel,
        grid=(B, S // tq),
        in_specs=[smem, smem, _const_spec((W, 2 * W)),
                  cur(A_Q_WIDTH), prev, cur(kvw), prev, cur(kvw)],
        out_specs=cur(A_Q_WIDTH),
        out_shape=jax.ShapeDtypeStruct((B, S, A_Q_WIDTH), BF16),
        scratch_shapes=[pltpu.VMEM((2, A_Q_HEADS, W, 2 * W), F32)],
        compiler_params=pltpu.CompilerParams(
            dimension_semantics=("arbitrary", "arbitrary"), vmem_limit_bytes=VMEM_LIMIT_BYTES),
        name="swa",
    )(sinks, rel_bias, bucket, aq, ak2, ak2, av2, av2)


def _merge_kernel(x_ref, hm_ref, ha_ref, ga_ref, gb_ref, wm_ref, wa_ref, wo_ref, out_ref):
    y = (jax.nn.sigmoid(ga_ref[...]) * jnp.dot(hm_ref[...], wm_ref[...], preferred_element_type=F32)
         + jax.nn.sigmoid(gb_ref[...]) * jnp.dot(ha_ref[...], wa_ref[...], preferred_element_type=F32))
    out_ref[...] = x_ref[...] + jnp.dot(y.astype(BF16), wo_ref[...], preferred_element_type=F32)


def _merge(x2, hm, ha, ga, gb, wm, wa, wo):
    T = x2.shape[0]
    tm = PROJ_TM
    row = pl.BlockSpec((tm, D_MODEL), lambda i: (i, 0))
    wspec = _const_spec((D_MODEL, D_MODEL))
    return pl.pallas_call(
        _merge_kernel,
        grid=(T // tm,),
        in_specs=[row, row, row, row, row, wspec, wspec, wspec],
        out_specs=row,
        out_shape=jax.ShapeDtypeStruct((T, D_MODEL), F32),
        compiler_params=pltpu.CompilerParams(
            dimension_semantics=("parallel",), vmem_limit_bytes=VMEM_LIMIT_BYTES),
        name="merge",
    )(x2, hm, ha, ga, gb, wm, wa, wo)


def _mlp_kernel(x_ref, g_ref, wu_ref, wd_ref, fg_ref, out_ref, *, final):
    x = x_ref[...]
    h = _rmsnorm(x, g_ref[...]).astype(BF16)
    u = jnp.maximum(jnp.dot(h, wu_ref[...], preferred_element_type=F32), 0.0)
    x = x + jnp.dot((u * u).astype(BF16), wd_ref[...], preferred_element_type=F32)
    if final:
        x = _rmsnorm(x, fg_ref[...])
    out_ref[...] = x


def _mlp(x2, g, wu, wd, fg, final):
    T = x2.shape[0]
    tm = PROJ_TM
    row = pl.BlockSpec((tm, D_MODEL), lambda i: (i, 0))
    return pl.pallas_call(
        functools.partial(_mlp_kernel, final=final),
        grid=(T // tm,),
        in_specs=[row, _const_spec((1, D_MODEL)), _const_spec((D_MODEL, D_FF)),
                  _const_spec((D_FF, D_MODEL)), _const_spec((1, D_MODEL))],
        out_specs=row,
        out_shape=jax.ShapeDtypeStruct((T, D_MODEL), F32),
        compiler_params=pltpu.CompilerParams(
            dimension_semantics=("parallel",), vmem_limit_bytes=VMEM_LIMIT_BYTES),
        name="mlp",
    )(x2, g, wu, wd, fg)


def _t5_bucket(n):
    max_exact = N_BUCKETS // 2
    n = np.maximum(n, 0)
    large = max_exact + (np.log(np.maximum(n, 1) / max_exact)
                         / np.log(MAX_DISTANCE / max_exact)
                         * (N_BUCKETS - max_exact)).astype(np.int32)
    large = np.minimum(large, N_BUCKETS - 1)
    return np.where(n < max_exact, n, large).astype(np.int32)


def _bucket_table():
    W = WINDOW
    dist = np.arange(W)[:, None] + W - np.arange(2 * W)[None, :]
    valid = (dist >= 0) & (dist < W)
    return jnp.asarray(np.where(valid, _t5_bucket(dist), -1).astype(np.int32))


def _dup_heads(w):
    w = w.reshape(w.shape[0], A_KV_HEADS, 1, A_HEAD_DIM)
    return jnp.broadcast_to(w, (w.shape[0], A_KV_HEADS, 2, A_HEAD_DIM)).reshape(w.shape[0], -1)


def _split_w_in(w):
    offs = np.cumsum((0,) + IN_SIZES)
    parts = [w[:, offs[i]:offs[i + 1]] for i in range(len(IN_SIZES))]
    wqk, wv, wo, wi, wf, waq, wak, wav, wga, wgb = parts
    wg = jnp.pad(jnp.concatenate([wi, wf], axis=1), ((0, 0), (0, 128 - 2 * M_HEADS)))
    waq = waq * (A_HEAD_DIM ** -0.5)
    wak, wav = _dup_heads(wak), _dup_heads(wav)
    return [p.astype(BF16) for p in (wqk, wv, wo, wg, waq, wak, wav, wga, wgb)]


def kernel(x, norm_mix_g, w_in, conv_w, conv_b, b_igate, b_fgate, mlstm_norm_g, attn_sinks,
           rel_bias, w_branch_m, w_branch_a, w_out, norm_mlp_g, w_up, w_down, final_norm_g):
    B, S, D = x.shape
    T = B * S
    bucket = _bucket_table()
    x2 = x.reshape(T, D)
    for l in range(DEPTH):
        ws = _split_w_in(w_in[l])
        mqk, mv, mo, gates, aq, ak, av, ga, gb = _inproj(x2, norm_mix_g[l].reshape(1, D), ws)
        g_rows = gates.reshape(B, S, 2 * M_HEADS).transpose(0, 2, 1)
        gate_b = jnp.concatenate([b_igate[l], b_fgate[l]]).reshape(2 * M_HEADS, 1)
        hm = _mlstm(mqk.reshape(B, S, -1), mv.reshape(B, S, -1), mo.reshape(B, S, -1), g_rows,
                    conv_w[l], conv_b[l].reshape(1, -1), gate_b, mlstm_norm_g[l].reshape(1, -1))
        ha = _swa(aq.reshape(B, S, -1), ak.reshape(B, S, -1), av.reshape(B, S, -1),
                  attn_sinks[l], rel_bias, bucket)
        x2 = _merge(x2, hm.reshape(T, -1), ha.reshape(T, -1), ga, gb,
                    w_branch_m[l].astype(BF16), w_branch_a[l].astype(BF16), w_out[l].astype(BF16))
        x2 = _mlp(x2, norm_mlp_g[l].reshape(1, D), w_up[l].astype(BF16), w_down[l].astype(BF16),
                  final_norm_g.reshape(1, D), final=(l == DEPTH - 1))
    return x2.reshape(B, S, D)
```

```python
import functools

import numpy as np
import jax
import jax.numpy as jnp
from jax import lax
from jax.experimental import pallas as pl
from jax.experimental.pallas import tpu as pltpu

D_MODEL = 1024
DEPTH = 2
M_HEADS = 4
M_QK_DIM = 128
M_V_DIM = 256
M_QK_WIDTH = M_HEADS * M_QK_DIM
M_V_WIDTH = M_HEADS * M_V_DIM
CONV_K = 4
A_Q_HEADS = 16
A_KV_HEADS = 4
A_HEAD_DIM = 64
A_GROUP = A_Q_HEADS // A_KV_HEADS
A_Q_WIDTH = A_Q_HEADS * A_HEAD_DIM
A_KV_WIDTH = A_KV_HEADS * A_HEAD_DIM
WINDOW = 128
N_BUCKETS = 32
MAX_DISTANCE = 128
D_FF = 4 * D_MODEL
EPS = 1e-6
IN_SIZES = (2 * M_QK_WIDTH, M_V_WIDTH, M_V_WIDTH, M_HEADS, M_HEADS,
            A_Q_WIDTH, A_KV_WIDTH, A_KV_WIDTH, D_MODEL, D_MODEL)

PROJ_TM = 256
MLSTM_CHUNK = 128
MLSTM_BLOCKS = 2
CONV_HALO = 8
SWA_BLOCKS = 2
NEG_BIG = -1e30
VMEM_LIMIT_BYTES = 56 * 1024 * 1024

F32 = jnp.float32
BF16 = jnp.bfloat16


def _const_spec(shape):
    nd = len(shape)
    return pl.BlockSpec(shape, lambda *_: (0,) * nd, pipeline_mode=pl.Buffered(1))


def _rmsnorm(x, g):
    return x * lax.rsqrt(jnp.mean(x * x, axis=-1, keepdims=True) + EPS) * g


def _inproj_kernel(x_ref, g_ref, cw_ref, cb_ref, w_qk, w_v, w_o, w_g, w_aq, w_ak, w_av, w_ga, w_gb,
                   o_qk, o_v, o_o, o_g, o_aq, o_ak, o_av, o_ga, o_gb, halo_ref, *, tiles_per_seq):
    tm = x_ref.shape[0]
    h = _rmsnorm(x_ref[...], g_ref[...]).astype(BF16)

    @pl.when(pl.program_id(0) % tiles_per_seq == 0)
    def _():
        halo_ref[...] = jnp.zeros_like(halo_ref)

    r = jnp.dot(h, w_qk[...], preferred_element_type=F32)
    prev = halo_ref[...]
    halo_ref[...] = r[tm - CONV_HALO:, :]
    head_row = lax.broadcasted_iota(jnp.int32, (CONV_HALO, 1), 0)
    y = cb_ref[...] + r * cw_ref[CONV_K - 1:CONV_K, :]
    for s in range(1, CONV_K):
        rolled = pltpu.roll(r, s, axis=0)
        head = jnp.where(head_row < s, pltpu.roll(prev, s, axis=0), rolled[:CONV_HALO])
        shifted = jnp.concatenate([head, rolled[CONV_HALO:]], axis=0)
        y = y + shifted * cw_ref[CONV_K - 1 - s:CONV_K - s, :]
    y = y * jax.nn.sigmoid(y)
    o_qk[:, :M_QK_WIDTH] = (y[:, :M_QK_WIDTH] * (M_QK_DIM ** -0.5)).astype(o_qk.dtype)
    o_qk[:, M_QK_WIDTH:] = y[:, M_QK_WIDTH:].astype(o_qk.dtype)

    for w, o in ((w_v, o_v), (w_aq, o_aq), (w_ak, o_ak), (w_av, o_av)):
        o[...] = jnp.dot(h, w[...], preferred_element_type=F32).astype(o.dtype)
    for w, o in ((w_o, o_o), (w_ga, o_ga), (w_gb, o_gb)):
        o[...] = jax.nn.sigmoid(jnp.dot(h, w[...], preferred_element_type=F32)).astype(o.dtype)
    o_g[...] = jnp.dot(h, w_g[...], preferred_element_type=F32)[:, :2 * M_HEADS]


def _inproj(x2, g, conv_w, conv_b, ws, seq_len):
    T = x2.shape[0]
    tm = PROJ_TM
    row = lambda n: pl.BlockSpec((tm, n), lambda i: (i, 0))
    out_widths = (2 * M_QK_WIDTH, M_V_WIDTH, M_V_WIDTH, 2 * M_HEADS, A_Q_WIDTH,
                  2 * A_KV_WIDTH, 2 * A_KV_WIDTH, D_MODEL, D_MODEL)
    out_dtypes = (BF16, BF16, BF16, F32, BF16, BF16, BF16, BF16, BF16)
    return pl.pallas_call(
        functools.partial(_inproj_kernel, tiles_per_seq=seq_len // tm),
        grid=(T // tm,),
        in_specs=[row(D_MODEL), _const_spec((1, D_MODEL)), _const_spec((CONV_K, 2 * M_QK_WIDTH)),
                  _const_spec((1, 2 * M_QK_WIDTH))] + [_const_spec(w.shape) for w in ws],
        out_specs=[row(n) for n in out_widths],
        out_shape=[jax.ShapeDtypeStruct((T, n), d) for n, d in zip(out_widths, out_dtypes)],
        scratch_shapes=[pltpu.VMEM((CONV_HALO, 2 * M_QK_WIDTH), F32)],
        compiler_params=pltpu.CompilerParams(
            dimension_semantics=("arbitrary",), vmem_limit_bytes=VMEM_LIMIT_BYTES),
        name="inproj",
    )(x2, g, conv_w, conv_b, *ws)


def _mlstm_kernel(qk_ref, v_ref, og_ref, grow_ref, gb_ref, ng_ref,
                  out_ref, c_ref, n_ref, m_ref):
    @pl.when(pl.program_id(1) == 0)
    def _():
        c_ref[...] = jnp.zeros_like(c_ref)
        n_ref[...] = jnp.zeros_like(n_ref)
        m_ref[...] = jnp.zeros_like(m_ref)

    for blk in range(MLSTM_BLOCKS):
        _mlstm_chunk(blk, qk_ref, v_ref, og_ref, grow_ref, gb_ref, ng_ref, out_ref, c_ref, n_ref, m_ref)


def _mlstm_chunk(blk, qk_ref, v_ref, og_ref, grow_ref, gb_ref, ng_ref, out_ref, c_ref, n_ref, m_ref):
    L = MLSTM_CHUNK
    rows = slice(blk * L, (blk + 1) * L)
    row_i = lax.broadcasted_iota(jnp.int32, (L, L), 0)
    col_j = lax.broadcasted_iota(jnp.int32, (L, L), 1)
    tril = col_j <= row_i
    eye = col_j == row_i
    triu_f = (row_i <= col_j).astype(F32)

    g_rows = grow_ref[:, rows] + gb_ref[...]
    logf_rows = jax.nn.log_sigmoid(g_rows)
    b_rows = jnp.dot(logf_rows, triu_f, preferred_element_type=F32,
                     precision=lax.Precision.HIGHEST)

    for h in range(M_HEADS):
        qb = qk_ref[rows, h * M_QK_DIM:(h + 1) * M_QK_DIM]
        kb = qk_ref[rows, M_QK_WIDTH + h * M_QK_DIM:M_QK_WIDTH + (h + 1) * M_QK_DIM]
        v = v_ref[rows, h * M_V_DIM:(h + 1) * M_V_DIM]
        q = qb.astype(F32)
        k = kb.astype(F32)

        m_prev = m_ref[h][:, 0:1]
        n_prev = n_ref[h]
        c_prev = c_ref[h]

        logf_row = logf_rows[M_HEADS + h:M_HEADS + h + 1, :]
        b_row = b_rows[M_HEADS + h:M_HEADS + h + 1, :]
        a_row = g_rows[h:h + 1, :] - b_row
        b_col = jnp.sum(jnp.where(tril, logf_row, 0.0), axis=-1, keepdims=True)
        a_col = jnp.sum(jnp.where(eye, a_row, 0.0), axis=-1, keepdims=True)
        b_last = jnp.sum(logf_row, axis=-1, keepdims=True)

        a_mat = jnp.where(tril, a_row, -jnp.inf)
        m_row = jnp.maximum(m_prev, jnp.max(a_mat, axis=-1, keepdims=True))
        w_intra = jnp.exp(a_mat - m_row)
        w_inter = jnp.exp(m_prev - m_row)

        s = lax.dot_general(qb, kb, (((1,), (1,)), ((), ())),
                            preferred_element_type=F32) * w_intra
        num = (jnp.dot(s.astype(BF16), v, preferred_element_type=F32)
               + w_inter * jnp.dot(qb, c_prev.astype(BF16), preferred_element_type=F32))
        den = (jnp.sum(s, axis=-1, keepdims=True)
               + w_inter * jnp.sum(q * n_prev, axis=-1, keepdims=True))
        hh = num / jnp.maximum(jnp.abs(den), jnp.exp(-(b_col + m_row)))

        m_last = jnp.maximum(m_prev, jnp.max(a_row, axis=-1, keepdims=True))
        decay = jnp.exp(m_prev - m_last)
        kw = k * jnp.exp(a_col - m_last)
        c_ref[h] = decay * c_prev + lax.dot_general(
            kw.astype(BF16), v, (((0,), (0,)), ((), ())), preferred_element_type=F32)
        n_ref[h] = decay * n_prev + jnp.sum(kw, axis=0, keepdims=True)
        m_ref[h] = jnp.broadcast_to(b_last + m_last, (1, 128))

        sl = slice(h * M_V_DIM, (h + 1) * M_V_DIM)
        hn = hh * lax.rsqrt(jnp.mean(hh * hh, axis=-1, keepdims=True) + EPS) * ng_ref[:, sl]
        out_ref[rows, sl] = hn.astype(BF16) * og_ref[rows, sl]


def _mlstm(mqk, mv, og, g_rows, gate_b, norm_g):
    B, S, _ = mqk.shape
    tl = MLSTM_BLOCKS * MLSTM_CHUNK
    tok = lambda n: pl.BlockSpec((None, tl, n), lambda b, c: (b, c, 0))
    return pl.pallas_call(
        _mlstm_kernel,
        grid=(B, S // tl),
        in_specs=[tok(2 * M_QK_WIDTH), tok(M_V_WIDTH), tok(M_V_WIDTH),
                  pl.BlockSpec((None, 2 * M_HEADS, tl), lambda b, c: (b, 0, c)),
                  _const_spec((2 * M_HEADS, 1)), _const_spec((1, M_V_WIDTH))],
        out_specs=tok(M_V_WIDTH),
        out_shape=jax.ShapeDtypeStruct((B, S, M_V_WIDTH), BF16),
        scratch_shapes=[pltpu.VMEM((M_HEADS, M_QK_DIM, M_V_DIM), F32),
                        pltpu.VMEM((M_HEADS, 1, M_QK_DIM), F32),
                        pltpu.VMEM((M_HEADS, 1, 128), F32)],
        compiler_params=pltpu.CompilerParams(
            dimension_semantics=("arbitrary", "arbitrary"), vmem_limit_bytes=VMEM_LIMIT_BYTES),
        name="mlstm",
    )(mqk, mv, og, g_rows, gate_b, norm_g)


def _swa_kernel(sink_ref, rel_ref, bucket_ref, q_ref, kp_ref, kc_ref, vp_ref, vc_ref,
                out_ref, bias_ref):
    W = WINDOW
    nb = SWA_BLOCKS
    i = pl.program_id(1)

    @pl.when((pl.program_id(0) == 0) & (i == 0))
    def _():
        bucket = bucket_ref[...]
        neg = jnp.full((W, 2 * W), NEG_BIG, F32)
        for hq in range(A_Q_HEADS):
            bias_ref[1, hq] = neg

        def fill(bb, carry):
            hit = bucket == bb
            for hq in range(A_Q_HEADS):
                bias_ref[1, hq] = jnp.where(hit, rel_ref[bb, hq], bias_ref[1, hq])
            return carry
        lax.fori_loop(0, N_BUCKETS, fill, 0)
        is_prev = lax.broadcasted_iota(jnp.int32, (W, 2 * W), 1) < W
        for hq in range(A_Q_HEADS):
            bias_ref[0, hq] = jnp.where(is_prev, NEG_BIG, bias_ref[1, hq])

    lo = lax.broadcasted_iota(jnp.int32, (W, 2 * A_HEAD_DIM), 1) < A_HEAD_DIM
    lo2 = lax.broadcasted_iota(jnp.int32, (2 * W, 2 * A_HEAD_DIM), 1) < A_HEAD_DIM
    zero = jnp.zeros((), BF16)
    ones_lo = jnp.where(lo2, 1.0, 0.0).astype(BF16)
    ones_hi = jnp.where(lo2, 0.0, 1.0).astype(BF16)
    nt = (((1,), (1,)), ((), ()))

    for blk in range(nb):
        rows = slice(blk * W, (blk + 1) * W)
        var = jnp.minimum(i, 1) if blk == 0 else 1
        for kv in range(A_KV_HEADS):
            ls = slice(kv * 2 * A_HEAD_DIM, (kv + 1) * 2 * A_HEAD_DIM)
            if blk == 0:
                k_prev, v_prev = kp_ref[:, ls], vp_ref[:, ls]
            else:
                k_prev, v_prev = kc_ref[(blk - 1) * W:blk * W, ls], vc_ref[(blk - 1) * W:blk * W, ls]
            k2 = jnp.concatenate([k_prev, kc_ref[rows, ls]], axis=0)
            v2 = jnp.concatenate([v_prev, vc_ref[rows, ls]], axis=0)
            k_even = jnp.where(lo2, k2, zero)
            k_odd = jnp.where(lo2, zero, k2)
            v_even = jnp.concatenate([jnp.where(lo2, v2, zero), ones_lo], axis=1)
            v_odd = jnp.concatenate([jnp.where(lo2, zero, v2), ones_hi], axis=1)
            for pair in range(A_GROUP // 2):
                pi = kv * (A_GROUP // 2) + pair
                cols = slice(pi * 2 * A_HEAD_DIM, (pi + 1) * 2 * A_HEAD_DIM)
                q2 = q_ref[rows, cols]
                sink_e = sink_ref[2 * pi]
                sink_o = sink_ref[2 * pi + 1]
                s_e = lax.dot_general(q2, k_even, nt, preferred_element_type=F32) + bias_ref[var, 2 * pi]
                s_o = lax.dot_general(q2, k_odd, nt, preferred_element_type=F32) + bias_ref[var, 2 * pi + 1]
                m_e = jnp.maximum(jnp.max(s_e, axis=-1, keepdims=True), sink_e)
                m_o = jnp.maximum(jnp.max(s_o, axis=-1, keepdims=True), sink_o)
                p_e = jnp.exp(s_e - m_e).astype(BF16)
                p_o = jnp.exp(s_o - m_o).astype(BF16)
                acc = (jnp.dot(p_e, v_even, preferred_element_type=F32)
                       + jnp.dot(p_o, v_odd, preferred_element_type=F32))
                sink_term = jnp.where(lo, jnp.exp(sink_e - m_e), jnp.exp(sink_o - m_o))
                out_ref[rows, cols] = (acc[:, :2 * A_HEAD_DIM]
                                       / (acc[:, 2 * A_HEAD_DIM:] + sink_term)).astype(out_ref.dtype)


def _swa(aq, ak2, av2, sinks, rel_bias, bucket):
    B, S, _ = aq.shape
    W = WINDOW
    tq = SWA_BLOCKS * W
    kvw = 2 * A_KV_WIDTH
    cur = lambda n: pl.BlockSpec((None, tq, n), lambda b, i: (b, i, 0))
    prev = pl.BlockSpec((None, W, kvw), lambda b, i: (b, jnp.maximum(i * SWA_BLOCKS - 1, 0), 0))
    smem = pl.BlockSpec(memory_space=pltpu.SMEM)
    return pl.pallas_call(
        _swa_kernel,
        grid=(B, S // tq),
        in_specs=[smem, smem, _const_spec((W, 2 * W)),
                  cur(A_Q_WIDTH), prev, cur(kvw), prev, cur(kvw)],
        out_specs=cur(A_Q_WIDTH),
        out_shape=jax.ShapeDtypeStruct((B, S, A_Q_WIDTH), BF16),
        scratch_shapes=[pltpu.VMEM((2, A_Q_HEADS, W, 2 * W), F32)],
        compiler_params=pltpu.CompilerParams(
            dimension_semantics=("arbitrary", "arbitrary"), vmem_limit_bytes=VMEM_LIMIT_BYTES),
        name="swa",
    )(sinks, rel_bias, bucket, aq, ak2, ak2, av2, av2)


def _merge_kernel(x_ref, hm_ref, ha_ref, ga_ref, gb_ref, wm_ref, wa_ref, wo_ref, out_ref):
    y = (ga_ref[...] * jnp.dot(hm_ref[...], wm_ref[...], preferred_element_type=F32)
         + gb_ref[...] * jnp.dot(ha_ref[...], wa_ref[...], preferred_element_type=F32))
    out_ref[...] = x_ref[...] + jnp.dot(y.astype(BF16), wo_ref[...], preferred_element_type=F32)


def _merge(x2, hm, ha, ga, gb, wm, wa, wo):
    T = x2.shape[0]
    tm = PROJ_TM
    row = pl.BlockSpec((tm, D_MODEL), lambda i: (i, 0))
    wspec = _const_spec((D_MODEL, D_MODEL))
    return pl.pallas_call(
        _merge_kernel,
        grid=(T // tm,),
        in_specs=[row, row, row, row, row, wspec, wspec, wspec],
        out_specs=row,
        out_shape=jax.ShapeDtypeStruct((T, D_MODEL), F32),
        compiler_params=pltpu.CompilerParams(
            dimension_semantics=("parallel",), vmem_limit_bytes=VMEM_LIMIT_BYTES),
        name="merge",
    )(x2, hm, ha, ga, gb, wm, wa, wo)


def _mlp_kernel(x_ref, g_ref, wu_ref, wd_ref, fg_ref, out_ref, *, final):
    x = x_ref[...]
    h = _rmsnorm(x, g_ref[...]).astype(BF16)
    u = jnp.maximum(jnp.dot(h, wu_ref[...], preferred_element_type=F32), 0.0)
    x = x + jnp.dot((u * u).astype(BF16), wd_ref[...], preferred_element_type=F32)
    if final:
        x = _rmsnorm(x, fg_ref[...])
    out_ref[...] = x


def _mlp(x2, g, wu, wd, fg, final):
    T = x2.shape[0]
    tm = PROJ_TM
    row = pl.BlockSpec((tm, D_MODEL), lambda i: (i, 0))
    return pl.pallas_call(
        functools.partial(_mlp_kernel, final=final),
        grid=(T // tm,),
        in_specs=[row, _const_spec((1, D_MODEL)), _const_spec((D_MODEL, D_FF)),
                  _const_spec((D_FF, D_MODEL)), _const_spec((1, D_MODEL))],
        out_specs=row,
        out_shape=jax.ShapeDtypeStruct((T, D_MODEL), F32),
        compiler_params=pltpu.CompilerParams(
            dimension_semantics=("parallel",), vmem_limit_bytes=VMEM_LIMIT_BYTES),
        name="mlp",
    )(x2, g, wu, wd, fg)


def _t5_bucket(n):
    max_exact = N_BUCKETS // 2
    n = np.maximum(n, 0)
    large = max_exact + (np.log(np.maximum(n, 1) / max_exact)
                         / np.log(MAX_DISTANCE / max_exact)
                         * (N_BUCKETS - max_exact)).astype(np.int32)
    large = np.minimum(large, N_BUCKETS - 1)
    return np.where(n < max_exact, n, large).astype(np.int32)


def _bucket_table():
    W = WINDOW
    dist = np.arange(W)[:, None] + W - np.arange(2 * W)[None, :]
    valid = (dist >= 0) & (dist < W)
    return jnp.asarray(np.where(valid, _t5_bucket(dist), -1).astype(np.int32))


def _dup_heads(w):
    w = w.reshape(w.shape[0], A_KV_HEADS, 1, A_HEAD_DIM)
    return jnp.broadcast_to(w, (w.shape[0], A_KV_HEADS, 2, A_HEAD_DIM)).reshape(w.shape[0], -1)


def _split_w_in(w):
    offs = np.cumsum((0,) + IN_SIZES)
    parts = [w[:, offs[i]:offs[i + 1]] for i in range(len(IN_SIZES))]
    wqk, wv, wo, wi, wf, waq, wak, wav, wga, wgb = parts
    wg = jnp.pad(jnp.concatenate([wi, wf], axis=1), ((0, 0), (0, 128 - 2 * M_HEADS)))
    waq = waq * (A_HEAD_DIM ** -0.5)
    wak, wav = _dup_heads(wak), _dup_heads(wav)
    return [p.astype(BF16) for p in (wqk, wv, wo, wg, waq, wak, wav, wga, wgb)]


def kernel(x, norm_mix_g, w_in, conv_w, conv_b, b_igate, b_fgate, mlstm_norm_g, attn_sinks,
           rel_bias, w_branch_m, w_branch_a, w_out, norm_mlp_g, w_up, w_down, final_norm_g):
    B, S, D = x.shape
    T = B * S
    bucket = _bucket_table()
    x2 = x.reshape(T, D)
    for l in range(DEPTH):
        ws = _split_w_in(w_in[l])
        mqk, mv, og, gates, aq, ak, av, ga, gb = _inproj(
            x2, norm_mix_g[l].reshape(1, D), conv_w[l], conv_b[l].reshape(1, -1), ws, S)
        g_rows = gates.reshape(B, S, 2 * M_HEADS).transpose(0, 2, 1)
        gate_b = jnp.concatenate([b_igate[l], b_fgate[l]]).reshape(2 * M_HEADS, 1)
        hm = _mlstm(mqk.reshape(B, S, -1), mv.reshape(B, S, -1), og.reshape(B, S, -1), g_rows,
                    gate_b, mlstm_norm_g[l].reshape(1, -1))
        ha = _swa(aq.reshape(B, S, -1), ak.reshape(B, S, -1), av.reshape(B, S, -1),
                  attn_sinks[l], rel_bias, bucket)
        x2 = _merge(x2, hm.reshape(T, -1), ha.reshape(T, -1), ga, gb,
                    w_branch_m[l].astype(BF16), w_branch_a[l].astype(BF16), w_out[l].astype(BF16))
        x2 = _mlp(x2, norm_mlp_g[l].reshape(1, D), w_up[l].astype(BF16), w_down[l].astype(BF16),
                  final_norm_g.reshape(1, D), final=(l == DEPTH - 1))
    return x2.reshape(B, S, D)
```

```python
import functools

import numpy as np
import jax
import jax.numpy as jnp
from jax import lax
from jax.experimental import pallas as pl
from jax.experimental.pallas import tpu as pltpu

D_MODEL = 1024
DEPTH = 2
M_HEADS = 4
M_QK_DIM = 128
M_V_DIM = 256
M_QK_WIDTH = M_HEADS * M_QK_DIM
M_V_WIDTH = M_HEADS * M_V_DIM
CONV_K = 4
A_Q_HEADS = 16
A_KV_HEADS = 4
A_HEAD_DIM = 64
A_GROUP = A_Q_HEADS // A_KV_HEADS
A_Q_WIDTH = A_Q_HEADS * A_HEAD_DIM
A_KV_WIDTH = A_KV_HEADS * A_HEAD_DIM
WINDOW = 128
N_BUCKETS = 32
MAX_DISTANCE = 128
D_FF = 4 * D_MODEL
EPS = 1e-6
IN_SIZES = (2 * M_QK_WIDTH, M_V_WIDTH, M_V_WIDTH, M_HEADS, M_HEADS,
            A_Q_WIDTH, A_KV_WIDTH, A_KV_WIDTH, D_MODEL, D_MODEL)

PROJ_TM = 256
MLSTM_CHUNK = 128
MLSTM_BLOCKS = 2
CONV_HALO = 8
SWA_BLOCKS = 2
NEG_BIG = -1e30
VMEM_LIMIT_BYTES = 56 * 1024 * 1024

F32 = jnp.float32
BF16 = jnp.bfloat16


def _const_spec(shape):
    nd = len(shape)
    return pl.BlockSpec(shape, lambda *_: (0,) * nd, pipeline_mode=pl.Buffered(1))


def _rmsnorm(x, g):
    return x * lax.rsqrt(jnp.mean(x * x, axis=-1, keepdims=True) + EPS) * g


def _inproj_kernel(x_ref, g_ref, cw_ref, cb_ref, w_qk, w_v, w_o, w_g, w_aq, w_ak, w_av, w_ga, w_gb,
                   o_qk, o_v, o_o, o_g, o_aq, o_ak, o_av, o_ga, o_gb, halo_ref, *, tiles_per_seq):
    tm = x_ref.shape[0]
    h = _rmsnorm(x_ref[...], g_ref[...]).astype(BF16)

    @pl.when(pl.program_id(0) % tiles_per_seq == 0)
    def _():
        halo_ref[...] = jnp.zeros_like(halo_ref)

    head_row = lax.broadcasted_iota(jnp.int32, (CONV_HALO, 1), 0)

    def conv_group(cols, scale):
        r = jnp.dot(h, w_qk[:, cols], preferred_element_type=F32)
        prev = halo_ref[:, cols]
        halo_ref[:, cols] = r[tm - CONV_HALO:, :]
        y = cb_ref[:, cols] + r * cw_ref[CONV_K - 1:CONV_K, cols]
        for s in range(1, CONV_K):
            rolled = pltpu.roll(r, s, axis=0)
            head = jnp.where(head_row < s, pltpu.roll(prev, s, axis=0), rolled[:CONV_HALO])
            shifted = jnp.concatenate([head, rolled[CONV_HALO:]], axis=0)
            y = y + shifted * cw_ref[CONV_K - 1 - s:CONV_K - s, cols]
        y = y * jax.nn.sigmoid(y)
        o_qk[:, cols] = (y * scale if scale != 1.0 else y).astype(o_qk.dtype)

    def plain(w, o):
        o[...] = jnp.dot(h, w[...], preferred_element_type=F32).astype(o.dtype)

    def gate(w, o):
        o[...] = jax.nn.sigmoid(jnp.dot(h, w[...], preferred_element_type=F32)).astype(o.dtype)

    gw = 2 * M_QK_WIDTH // 4
    others = (lambda: plain(w_v, o_v), lambda: (plain(w_aq, o_aq), plain(w_ak, o_ak), plain(w_av, o_av)),
              lambda: gate(w_o, o_o), lambda: (gate(w_ga, o_ga), gate(w_gb, o_gb)))
    for c in range(4):
        conv_group(slice(c * gw, (c + 1) * gw), M_QK_DIM ** -0.5 if c < 2 else 1.0)
        others[c]()
    o_g[...] = jnp.dot(h, w_g[...], preferred_element_type=F32)[:, :2 * M_HEADS]


def _inproj(x2, g, conv_w, conv_b, ws, seq_len):
    T = x2.shape[0]
    tm = PROJ_TM
    row = lambda n: pl.BlockSpec((tm, n), lambda i: (i, 0))
    out_widths = (2 * M_QK_WIDTH, M_V_WIDTH, M_V_WIDTH, 2 * M_HEADS, A_Q_WIDTH,
                  2 * A_KV_WIDTH, 2 * A_KV_WIDTH, D_MODEL, D_MODEL)
    out_dtypes = (BF16, BF16, BF16, F32, BF16, BF16, BF16, BF16, BF16)
    return pl.pallas_call(
        functools.partial(_inproj_kernel, tiles_per_seq=seq_len // tm),
        grid=(T // tm,),
        in_specs=[row(D_MODEL), _const_spec((1, D_MODEL)), _const_spec((CONV_K, 2 * M_QK_WIDTH)),
                  _const_spec((1, 2 * M_QK_WIDTH))] + [_const_spec(w.shape) for w in ws],
        out_specs=[row(n) for n in out_widths],
        out_shape=[jax.ShapeDtypeStruct((T, n), d) for n, d in zip(out_widths, out_dtypes)],
        scratch_shapes=[pltpu.VMEM((CONV_HALO, 2 * M_QK_WIDTH), F32)],
        compiler_params=pltpu.CompilerParams(
            dimension_semantics=("arbitrary",), vmem_limit_bytes=VMEM_LIMIT_BYTES),
        name="inproj",
    )(x2, g, conv_w, conv_b, *ws)


def _mlstm_init(c_ref, n_ref, m_ref):
    @pl.when(pl.program_id(1) == 0)
    def _():
        c_ref[...] = jnp.zeros_like(c_ref)
        n_ref[...] = jnp.zeros_like(n_ref)
        m_ref[...] = jnp.zeros_like(m_ref)


def _mlstm_gates(blk, grow_ref, gb_ref):
    L = MLSTM_CHUNK
    rows = slice(blk * L, (blk + 1) * L)
    row_i = lax.broadcasted_iota(jnp.int32, (L, L), 0)
    col_j = lax.broadcasted_iota(jnp.int32, (L, L), 1)
    triu_f = (row_i <= col_j).astype(F32)
    g_rows = grow_ref[:, rows] + gb_ref[...]
    logf_rows = jax.nn.log_sigmoid(g_rows)
    b_rows = jnp.dot(logf_rows, triu_f, preferred_element_type=F32,
                     precision=lax.Precision.HIGHEST)
    return g_rows, logf_rows, b_rows


def _mlstm_head(blk, h, gates, qk_ref, v_ref, og_ref, ng_ref, out_ref, c_ref, n_ref, m_ref):
    L = MLSTM_CHUNK
    rows = slice(blk * L, (blk + 1) * L)
    row_i = lax.broadcasted_iota(jnp.int32, (L, L), 0)
    col_j = lax.broadcasted_iota(jnp.int32, (L, L), 1)
    tril = col_j <= row_i
    eye = col_j == row_i
    g_rows, logf_rows, b_rows = gates
    qb = qk_ref[rows, h * M_QK_DIM:(h + 1) * M_QK_DIM]
    kb = qk_ref[rows, M_QK_WIDTH + h * M_QK_DIM:M_QK_WIDTH + (h + 1) * M_QK_DIM]
    v = v_ref[rows, h * M_V_DIM:(h + 1) * M_V_DIM]
    q = qb.astype(F32)
    k = kb.astype(F32)

    m_prev = m_ref[h][:, 0:1]
    n_prev = n_ref[h]
    c_prev = c_ref[h]

    logf_row = logf_rows[M_HEADS + h:M_HEADS + h + 1, :]
    b_row = b_rows[M_HEADS + h:M_HEADS + h + 1, :]
    a_row = g_rows[h:h + 1, :] - b_row
    b_col = jnp.sum(jnp.where(tril, logf_row, 0.0), axis=-1, keepdims=True)
    a_col = jnp.sum(jnp.where(eye, a_row, 0.0), axis=-1, keepdims=True)
    b_last = jnp.sum(logf_row, axis=-1, keepdims=True)

    a_mat = jnp.where(tril, a_row, -jnp.inf)
    m_row = jnp.maximum(m_prev, jnp.max(a_mat, axis=-1, keepdims=True))
    w_intra = jnp.exp(a_mat - m_row)
    w_inter = jnp.exp(m_prev - m_row)

    s = lax.dot_general(qb, kb, (((1,), (1,)), ((), ())),
                        preferred_element_type=F32) * w_intra
    num = (jnp.dot(s.astype(BF16), v, preferred_element_type=F32)
           + w_inter * jnp.dot(qb, c_prev.astype(BF16), preferred_element_type=F32))
    den = (jnp.sum(s, axis=-1, keepdims=True)
           + w_inter * jnp.sum(q * n_prev, axis=-1, keepdims=True))
    hh = num / jnp.maximum(jnp.abs(den), jnp.exp(-(b_col + m_row)))

    m_last = jnp.maximum(m_prev, jnp.max(a_row, axis=-1, keepdims=True))
    decay = jnp.exp(m_prev - m_last)
    kw = k * jnp.exp(a_col - m_last)
    c_ref[h] = decay * c_prev + lax.dot_general(
        kw.astype(BF16), v, (((0,), (0,)), ((), ())), preferred_element_type=F32)
    n_ref[h] = decay * n_prev + jnp.sum(kw, axis=0, keepdims=True)
    m_ref[h] = jnp.broadcast_to(b_last + m_last, (1, 128))

    sl = slice(h * M_V_DIM, (h + 1) * M_V_DIM)
    hn = hh * lax.rsqrt(jnp.mean(hh * hh, axis=-1, keepdims=True) + EPS) * ng_ref[:, sl]
    out_ref[rows, sl] = hn.astype(BF16) * og_ref[rows, sl]


def _swa_init(rel_ref, bucket_ref, bias_ref):
    W = WINDOW

    @pl.when((pl.program_id(0) == 0) & (pl.program_id(1) == 0))
    def _():
        bucket = bucket_ref[...]
        neg = jnp.full((W, 2 * W), NEG_BIG, F32)
        for hq in range(A_Q_HEADS):
            bias_ref[1, hq] = neg

        def fill(bb, carry):
            hit = bucket == bb
            for hq in range(A_Q_HEADS):
                bias_ref[1, hq] = jnp.where(hit, rel_ref[bb, hq], bias_ref[1, hq])
            return carry
        lax.fori_loop(0, N_BUCKETS, fill, 0)
        is_prev = lax.broadcasted_iota(jnp.int32, (W, 2 * W), 1) < W
        for hq in range(A_Q_HEADS):
            bias_ref[0, hq] = jnp.where(is_prev, NEG_BIG, bias_ref[1, hq])


def _swa_kv_head(blk, kv, sink_ref, q_ref, kp_ref, kc_ref, vp_ref, vc_ref, out_ref, bias_ref):
    W = WINDOW
    lo = lax.broadcasted_iota(jnp.int32, (W, 2 * A_HEAD_DIM), 1) < A_HEAD_DIM
    lo2 = lax.broadcasted_iota(jnp.int32, (2 * W, 2 * A_HEAD_DIM), 1) < A_HEAD_DIM
    zero = jnp.zeros((), BF16)
    ones_lo = jnp.where(lo2, 1.0, 0.0).astype(BF16)
    ones_hi = jnp.where(lo2, 0.0, 1.0).astype(BF16)
    nt = (((1,), (1,)), ((), ()))
    rows = slice(blk * W, (blk + 1) * W)
    var = jnp.minimum(pl.program_id(1), 1) if blk == 0 else 1
    ls = slice(kv * 2 * A_HEAD_DIM, (kv + 1) * 2 * A_HEAD_DIM)
    if blk == 0:
        k_prev, v_prev = kp_ref[:, ls], vp_ref[:, ls]
    else:
        k_prev, v_prev = kc_ref[(blk - 1) * W:blk * W, ls], vc_ref[(blk - 1) * W:blk * W, ls]
    k2 = jnp.concatenate([k_prev, kc_ref[rows, ls]], axis=0)
    v2 = jnp.concatenate([v_prev, vc_ref[rows, ls]], axis=0)
    k_even = jnp.where(lo2, k2, zero)
    k_odd = jnp.where(lo2, zero, k2)
    v_even = jnp.concatenate([jnp.where(lo2, v2, zero), ones_lo], axis=1)
    v_odd = jnp.concatenate([jnp.where(lo2, zero, v2), ones_hi], axis=1)
    for pair in range(A_GROUP // 2):
        pi = kv * (A_GROUP // 2) + pair
        cols = slice(pi * 2 * A_HEAD_DIM, (pi + 1) * 2 * A_HEAD_DIM)
        q2 = q_ref[rows, cols]
        sink_e = sink_ref[2 * pi]
        sink_o = sink_ref[2 * pi + 1]
        s_e = lax.dot_general(q2, k_even, nt, preferred_element_type=F32) + bias_ref[var, 2 * pi]
        s_o = lax.dot_general(q2, k_odd, nt, preferred_element_type=F32) + bias_ref[var, 2 * pi + 1]
        m_e = jnp.maximum(jnp.max(s_e, axis=-1, keepdims=True), sink_e)
        m_o = jnp.maximum(jnp.max(s_o, axis=-1, keepdims=True), sink_o)
        p_e = jnp.exp(s_e - m_e).astype(BF16)
        p_o = jnp.exp(s_o - m_o).astype(BF16)
        acc = (jnp.dot(p_e, v_even, preferred_element_type=F32)
               + jnp.dot(p_o, v_odd, preferred_element_type=F32))
        sink_term = jnp.where(lo, jnp.exp(sink_e - m_e), jnp.exp(sink_o - m_o))
        out_ref[rows, cols] = (acc[:, :2 * A_HEAD_DIM]
                               / (acc[:, 2 * A_HEAD_DIM:] + sink_term)).astype(out_ref.dtype)


def _mixers_kernel(sink_ref, rel_ref, bucket_ref, qk_ref, v_ref, og_ref, grow_ref, gb_ref, ng_ref,
                   q_ref, kp_ref, kc_ref, vp_ref, vc_ref,
                   hm_ref, ha_ref, c_ref, n_ref, m_ref, bias_ref):
    _mlstm_init(c_ref, n_ref, m_ref)
    _swa_init(rel_ref, bucket_ref, bias_ref)
    assert A_KV_HEADS == M_HEADS
    for blk in range(SWA_BLOCKS):
        for u in range(A_KV_HEADS):
            _swa_kv_head(blk, u, sink_ref, q_ref, kp_ref, kc_ref, vp_ref, vc_ref, ha_ref, bias_ref)
        gates = _mlstm_gates(blk, grow_ref, gb_ref)
        for u in range(M_HEADS):
            _mlstm_head(blk, u, gates, qk_ref, v_ref, og_ref, ng_ref, hm_ref, c_ref, n_ref, m_ref)


def _mixers(mqk, mv, og, g_rows, gate_b, norm_g, aq, ak2, av2, sinks, rel_bias, bucket):
    B, S, _ = aq.shape
    W = WINDOW
    tq = SWA_BLOCKS * W
    assert tq == MLSTM_BLOCKS * MLSTM_CHUNK
    kvw = 2 * A_KV_WIDTH
    cur = lambda n: pl.BlockSpec((None, tq, n), lambda b, i: (b, i, 0))
    prev = pl.BlockSpec((None, W, kvw), lambda b, i: (b, jnp.maximum(i * SWA_BLOCKS - 1, 0), 0))
    smem = pl.BlockSpec(memory_space=pltpu.SMEM)
    return pl.pallas_call(
        _mixers_kernel,
        grid=(B, S // tq),
        in_specs=[smem, smem, _const_spec((W, 2 * W)),
                  cur(2 * M_QK_WIDTH), cur(M_V_WIDTH), cur(M_V_WIDTH),
                  pl.BlockSpec((None, 2 * M_HEADS, tq), lambda b, i: (b, 0, i)),
                  _const_spec((2 * M_HEADS, 1)), _const_spec((1, M_V_WIDTH)),
                  cur(A_Q_WIDTH), prev, cur(kvw), prev, cur(kvw)],
        out_specs=[cur(M_V_WIDTH), cur(A_Q_WIDTH)],
        out_shape=[jax.ShapeDtypeStruct((B, S, M_V_WIDTH), BF16),
                   jax.ShapeDtypeStruct((B, S, A_Q_WIDTH), BF16)],
        scratch_shapes=[pltpu.VMEM((M_HEADS, M_QK_DIM, M_V_DIM), F32),
                        pltpu.VMEM((M_HEADS, 1, M_QK_DIM), F32),
                        pltpu.VMEM((M_HEADS, 1, 128), F32),
                        pltpu.VMEM((2, A_Q_HEADS, W, 2 * W), F32)],
        compiler_params=pltpu.CompilerParams(
            dimension_semantics=("arbitrary", "arbitrary"), vmem_limit_bytes=VMEM_LIMIT_BYTES),
        name="mixers",
    )(sinks, rel_bias, bucket, mqk, mv, og, g_rows, gate_b, norm_g, aq, ak2, ak2, av2, av2)


def _merge_kernel(x_ref, hm_ref, ha_ref, ga_ref, gb_ref, wm_ref, wa_ref, wo_ref, out_ref):
    y = (ga_ref[...] * jnp.dot(hm_ref[...], wm_ref[...], preferred_element_type=F32)
         + gb_ref[...] * jnp.dot(ha_ref[...], wa_ref[...], preferred_element_type=F32))
    out_ref[...] = x_ref[...] + jnp.dot(y.astype(BF16), wo_ref[...], preferred_element_type=F32)


def _merge(x2, hm, ha, ga, gb, wm, wa, wo):
    T = x2.shape[0]
    tm = PROJ_TM
    row = pl.BlockSpec((tm, D_MODEL), lambda i: (i, 0))
    wspec = _const_spec((D_MODEL, D_MODEL))
    return pl.pallas_call(
        _merge_kernel,
        grid=(T // tm,),
        in_specs=[row, row, row, row, row, wspec, wspec, wspec],
        out_specs=row,
        out_shape=jax.ShapeDtypeStruct((T, D_MODEL), F32),
        compiler_params=pltpu.CompilerParams(
            dimension_semantics=("parallel",), vmem_limit_bytes=VMEM_LIMIT_BYTES),
        name="merge",
    )(x2, hm, ha, ga, gb, wm, wa, wo)


def _mlp_kernel(x_ref, g_ref, wu_ref, wd_ref, fg_ref, out_ref, *, final):
    x = x_ref[...]
    h = _rmsnorm(x, g_ref[...]).astype(BF16)
    u = jnp.maximum(jnp.dot(h, wu_ref[...], preferred_element_type=F32), 0.0)
    x = x + jnp.dot((u * u).astype(BF16), wd_ref[...], preferred_element_type=F32)
    if final:
        x = _rmsnorm(x, fg_ref[...])
    out_ref[...] = x


def _mlp(x2, g, wu, wd, fg, final):
    T = x2.shape[0]
    tm = PROJ_TM
    row = pl.BlockSpec((tm, D_MODEL), lambda i: (i, 0))
    return pl.pallas_call(
        functools.partial(_mlp_kernel, final=final),
        grid=(T // tm,),
        in_specs=[row, _const_spec((1, D_MODEL)), _const_spec((D_MODEL, D_FF)),
                  _const_spec((D_FF, D_MODEL)), _const_spec((1, D_MODEL))],
        out_specs=row,
        out_shape=jax.ShapeDtypeStruct((T, D_MODEL), F32),
        compiler_params=pltpu.CompilerParams(
            dimension_semantics=("parallel",), vmem_limit_bytes=VMEM_LIMIT_BYTES),
        name="mlp",
    )(x2, g, wu, wd, fg)


def _t5_bucket(n):
    max_exact = N_BUCKETS // 2
    n = np.maximum(n, 0)
    large = max_exact + (np.log(np.maximum(n, 1) / max_exact)
                         / np.log(MAX_DISTANCE / max_exact)
                         * (N_BUCKETS - max_exact)).astype(np.int32)
    large = np.minimum(large, N_BUCKETS - 1)
    return np.where(n < max_exact, n, large).astype(np.int32)


def _bucket_table():
    W = WINDOW
    dist = np.arange(W)[:, None] + W - np.arange(2 * W)[None, :]
    valid = (dist >= 0) & (dist < W)
    return jnp.asarray(np.where(valid, _t5_bucket(dist), -1).astype(np.int32))


def _dup_heads(w):
    w = w.reshape(w.shape[0], A_KV_HEADS, 1, A_HEAD_DIM)
    return jnp.broadcast_to(w, (w.shape[0], A_KV_HEADS, 2, A_HEAD_DIM)).reshape(w.shape[0], -1)


def _split_w_in(w):
    offs = np.cumsum((0,) + IN_SIZES)
    parts = [w[:, offs[i]:offs[i + 1]] for i in range(len(IN_SIZES))]
    wqk, wv, wo, wi, wf, waq, wak, wav, wga, wgb = parts
    wg = jnp.pad(jnp.concatenate([wi, wf], axis=1), ((0, 0), (0, 128 - 2 * M_HEADS)))
    waq = waq * (A_HEAD_DIM ** -0.5)
    wak, wav = _dup_heads(wak), _dup_heads(wav)
    return [p.astype(BF16) for p in (wqk, wv, wo, wg, waq, wak, wav, wga, wgb)]


def kernel(x, norm_mix_g, w_in, conv_w, conv_b, b_igate, b_fgate, mlstm_norm_g, attn_sinks,
           rel_bias, w_branch_m, w_branch_a, w_out, norm_mlp_g, w_up, w_down, final_norm_g):
    B, S, D = x.shape
    T = B * S
    bucket = _bucket_table()
    x2 = x.reshape(T, D)
    for l in range(DEPTH):
        ws = _split_w_in(w_in[l])
        mqk, mv, og, gates, aq, ak, av, ga, gb = _inproj(
            x2, norm_mix_g[l].reshape(1, D), conv_w[l], conv_b[l].reshape(1, -1), ws, S)
        g_rows = gates.reshape(B, S, 2 * M_HEADS).transpose(0, 2, 1)
        gate_b = jnp.concatenate([b_igate[l], b_fgate[l]]).reshape(2 * M_HEADS, 1)
        hm, ha = _mixers(mqk.reshape(B, S, -1), mv.reshape(B, S, -1), og.reshape(B, S, -1), g_rows,
                         gate_b, mlstm_norm_g[l].reshape(1, -1),
                         aq.reshape(B, S, -1), ak.reshape(B, S, -1), av.reshape(B, S, -1),
                         attn_sinks[l], rel_bias, bucket)
        x2 = _merge(x2, hm.reshape(T, -1), ha.reshape(T, -1), ga, gb,
                    w_branch_m[l].astype(BF16), w_branch_a[l].astype(BF16), w_out[l].astype(BF16))
        x2 = _mlp(x2, norm_mlp_g[l].reshape(1, D), w_up[l].astype(BF16), w_down[l].astype(BF16),
                  final_norm_g.reshape(1, D), final=(l == DEPTH - 1))
    return x2.reshape(B, S, D)
```

```python
import functools

import numpy as np
import jax
import jax.numpy as jnp
from jax import lax
from jax.experimental import pallas as pl
from jax.experimental.pallas import tpu as pltpu

D_MODEL = 1024
DEPTH = 2
M_HEADS = 4
M_QK_DIM = 128
M_V_DIM = 256
M_QK_WIDTH = M_HEADS * M_QK_DIM
M_V_WIDTH = M_HEADS * M_V_DIM
CONV_K = 4
A_Q_HEADS = 16
A_KV_HEADS = 4
A_HEAD_DIM = 64
A_GROUP = A_Q_HEADS // A_KV_HEADS
A_Q_WIDTH = A_Q_HEADS * A_HEAD_DIM
A_KV_WIDTH = A_KV_HEADS * A_HEAD_DIM
WINDOW = 128
N_BUCKETS = 32
MAX_DISTANCE = 128
D_FF = 4 * D_MODEL
EPS = 1e-6
IN_SIZES = (2 * M_QK_WIDTH, M_V_WIDTH, M_V_WIDTH, M_HEADS, M_HEADS,
            A_Q_WIDTH, A_KV_WIDTH, A_KV_WIDTH, D_MODEL, D_MODEL)

INPROJ_TM = 512
POST_TM = 512
PROJ_SUB = 256
MLSTM_CHUNK = 128
MLSTM_BLOCKS = 2
CONV_HALO = 8
SWA_BLOCKS = 2
NEG_BIG = -1e30
VMEM_LIMIT_BYTES = 56 * 1024 * 1024

F32 = jnp.float32
BF16 = jnp.bfloat16


def _const_spec(shape):
    nd = len(shape)
    return pl.BlockSpec(shape, lambda *_: (0,) * nd, pipeline_mode=pl.Buffered(1))


def _rmsnorm(x, g):
    return x * lax.rsqrt(jnp.mean(x * x, axis=-1, keepdims=True) + EPS) * g


_INPROJ_GROUPS = (("qk", 2 * M_QK_WIDTH), ("o", M_V_WIDTH), ("ga", D_MODEL), ("gb", D_MODEL),
                  ("v", M_V_WIDTH), ("aq", A_Q_WIDTH), ("ak", 2 * A_KV_WIDTH), ("av", 2 * A_KV_WIDTH),
                  ("g", 128))
_INPROJ_OFFS = dict(zip((n for n, _ in _INPROJ_GROUPS),
                        np.cumsum([0] + [w for _, w in _INPROJ_GROUPS])[:-1].tolist()))
_INPROJ_WIDTH = sum(w for _, w in _INPROJ_GROUPS)


def _inproj_kernel(x_ref, g_ref, cw_ref, cb_ref, w_ref,
                   o_qk, o_o, o_ga, o_gb, o_v, o_aq, o_ak, o_av, o_g, halo_ref, *, tiles_per_seq):
    @pl.when(pl.program_id(0) % tiles_per_seq == 0)
    def _():
        halo_ref[...] = jnp.zeros_like(halo_ref)

    outs = dict(qk=o_qk, o=o_o, ga=o_ga, gb=o_gb, v=o_v, aq=o_aq, ak=o_ak, av=o_av, g=o_g)
    sub = PROJ_SUB
    for t in range(x_ref.shape[0] // sub):
        _inproj_rows(slice(t * sub, (t + 1) * sub), x_ref, g_ref, cw_ref, cb_ref, w_ref, outs, halo_ref)


def _inproj_rows(rows, x_ref, g_ref, cw_ref, cb_ref, w_ref, outs, halo_ref):
    n_rows = rows.stop - rows.start
    h = _rmsnorm(x_ref[rows, :], g_ref[...]).astype(BF16)
    res = jnp.dot(h, w_ref[...], preferred_element_type=F32)
    group = lambda name, width: res[:, _INPROJ_OFFS[name]:_INPROJ_OFFS[name] + width]

    head_row = lax.broadcasted_iota(jnp.int32, (CONV_HALO, 1), 0)
    gw = 2 * M_QK_WIDTH // 4
    for c in range(4):
        cols = slice(c * gw, (c + 1) * gw)
        r = res[:, cols]
        prev = halo_ref[:, cols]
        halo_ref[:, cols] = r[n_rows - CONV_HALO:, :]
        y = cb_ref[:, cols] + r * cw_ref[CONV_K - 1:CONV_K, cols]
        for s in range(1, CONV_K):
            rolled = pltpu.roll(r, s, axis=0)
            head = jnp.where(head_row < s, pltpu.roll(prev, s, axis=0), rolled[:CONV_HALO])
            shifted = jnp.concatenate([head, rolled[CONV_HALO:]], axis=0)
            y = y + shifted * cw_ref[CONV_K - 1 - s:CONV_K - s, cols]
        y = y * jax.nn.sigmoid(y)
        if c < 2:
            y = y * (M_QK_DIM ** -0.5)
        outs["qk"][rows, cols] = y.astype(BF16)

    for name in ("o", "ga", "gb"):
        outs[name][rows, :] = jax.nn.sigmoid(group(name, D_MODEL)).astype(BF16)
    for name, width in (("v", M_V_WIDTH), ("aq", A_Q_WIDTH), ("ak", 2 * A_KV_WIDTH), ("av", 2 * A_KV_WIDTH)):
        outs[name][rows, :] = group(name, width).astype(BF16)
    outs["g"][rows, :] = group("g", 2 * M_HEADS)


def _inproj(x2, g, conv_w, conv_b, w_all, seq_len):
    T = x2.shape[0]
    tm = INPROJ_TM
    row = lambda n: pl.BlockSpec((tm, n), lambda i: (i, 0))
    out_widths = [w for _, w in _INPROJ_GROUPS[:-1]] + [2 * M_HEADS]
    out_dtypes = [BF16] * (len(_INPROJ_GROUPS) - 1) + [F32]
    return pl.pallas_call(
        functools.partial(_inproj_kernel, tiles_per_seq=seq_len // tm),
        grid=(T // tm,),
        in_specs=[row(D_MODEL), _const_spec((1, D_MODEL)), _const_spec((CONV_K, 2 * M_QK_WIDTH)),
                  _const_spec((1, 2 * M_QK_WIDTH)), _const_spec((D_MODEL, _INPROJ_WIDTH))],
        out_specs=[row(n) for n in out_widths],
        out_shape=[jax.ShapeDtypeStruct((T, n), d) for n, d in zip(out_widths, out_dtypes)],
        scratch_shapes=[pltpu.VMEM((CONV_HALO, 2 * M_QK_WIDTH), F32)],
        compiler_params=pltpu.CompilerParams(
            dimension_semantics=("arbitrary",), vmem_limit_bytes=VMEM_LIMIT_BYTES),
        name="inproj",
    )(x2, g, conv_w, conv_b, w_all)


def _mlstm_init(c_ref, n_ref, m_ref):
    @pl.when(pl.program_id(1) == 0)
    def _():
        c_ref[...] = jnp.zeros_like(c_ref)
        n_ref[...] = jnp.zeros_like(n_ref)
        m_ref[...] = jnp.zeros_like(m_ref)


def _mlstm_gates(blk, grow_ref, gb_ref):
    L = MLSTM_CHUNK
    rows = slice(blk * L, (blk + 1) * L)
    row_i = lax.broadcasted_iota(jnp.int32, (L, L), 0)
    col_j = lax.broadcasted_iota(jnp.int32, (L, L), 1)
    triu_f = (row_i <= col_j).astype(F32)
    g_rows = grow_ref[:, rows] + gb_ref[...]
    logf_rows = jax.nn.log_sigmoid(g_rows)
    b_rows = jnp.dot(logf_rows, triu_f, preferred_element_type=F32,
                     precision=lax.Precision.HIGHEST)
    return g_rows, logf_rows, b_rows


def _mlstm_head(blk, h, gates, qk_ref, v_ref, og_ref, ng_ref, out_ref, c_ref, n_ref, m_ref):
    L = MLSTM_CHUNK
    rows = slice(blk * L, (blk + 1) * L)
    row_i = lax.broadcasted_iota(jnp.int32, (L, L), 0)
    col_j = lax.broadcasted_iota(jnp.int32, (L, L), 1)
    tril = col_j <= row_i
    eye = col_j == row_i
    g_rows, logf_rows, b_rows = gates
    qb = qk_ref[rows, h * M_QK_DIM:(h + 1) * M_QK_DIM]
    kb = qk_ref[rows, M_QK_WIDTH + h * M_QK_DIM:M_QK_WIDTH + (h + 1) * M_QK_DIM]
    v = v_ref[rows, h * M_V_DIM:(h + 1) * M_V_DIM]
    q = qb.astype(F32)
    k = kb.astype(F32)

    m_prev = m_ref[h][:, 0:1]
    n_prev = n_ref[h]
    c_prev = c_ref[h]

    logf_row = logf_rows[M_HEADS + h:M_HEADS + h + 1, :]
    b_row = b_rows[M_HEADS + h:M_HEADS + h + 1, :]
    a_row = g_rows[h:h + 1, :] - b_row
    b_col = jnp.sum(jnp.where(tril, logf_row, 0.0), axis=-1, keepdims=True)
    a_col = jnp.sum(jnp.where(eye, a_row, 0.0), axis=-1, keepdims=True)
    b_last = jnp.sum(logf_row, axis=-1, keepdims=True)

    a_mat = jnp.where(tril, a_row, -jnp.inf)
    m_row = jnp.maximum(m_prev, jnp.max(a_mat, axis=-1, keepdims=True))
    w_intra = jnp.exp(a_mat - m_row)
    w_inter = jnp.exp(m_prev - m_row)

    s = lax.dot_general(qb, kb, (((1,), (1,)), ((), ())),
                        preferred_element_type=F32) * w_intra
    num = (jnp.dot(s.astype(BF16), v, preferred_element_type=F32)
           + w_inter * jnp.dot(qb, c_prev.astype(BF16), preferred_element_type=F32))
    den = (jnp.sum(s, axis=-1, keepdims=True)
           + w_inter * jnp.sum(q * n_prev, axis=-1, keepdims=True))
    hh = num / jnp.maximum(jnp.abs(den), jnp.exp(-(b_col + m_row)))

    m_last = jnp.maximum(m_prev, jnp.max(a_row, axis=-1, keepdims=True))
    decay = jnp.exp(m_prev - m_last)
    kw = k * jnp.exp(a_col - m_last)
    c_ref[h] = decay * c_prev + lax.dot_general(
        kw.astype(BF16), v, (((0,), (0,)), ((), ())), preferred_element_type=F32)
    n_ref[h] = decay * n_prev + jnp.sum(kw, axis=0, keepdims=True)
    m_ref[h] = jnp.broadcast_to(b_last + m_last, (1, 128))

    sl = slice(h * M_V_DIM, (h + 1) * M_V_DIM)
    hn = hh * lax.rsqrt(jnp.mean(hh * hh, axis=-1, keepdims=True) + EPS) * ng_ref[:, sl]
    out_ref[rows, sl] = hn.astype(BF16) * og_ref[rows, sl]


def _swa_init(rel_ref, bucket_ref, bias_ref):
    W = WINDOW

    @pl.when((pl.program_id(0) == 0) & (pl.program_id(1) == 0))
    def _():
        bucket = bucket_ref[...]
        neg = jnp.full((W, 2 * W), NEG_BIG, F32)
        for hq in range(A_Q_HEADS):
            bias_ref[1, hq] = neg

        def fill(bb, carry):
            hit = bucket == bb
            for hq in range(A_Q_HEADS):
                bias_ref[1, hq] = jnp.where(hit, rel_ref[bb, hq], bias_ref[1, hq])
            return carry
        lax.fori_loop(0, N_BUCKETS, fill, 0)
        is_prev = lax.broadcasted_iota(jnp.int32, (W, 2 * W), 1) < W
        for hq in range(A_Q_HEADS):
            bias_ref[0, hq] = jnp.where(is_prev, NEG_BIG, bias_ref[1, hq])


def _swa_kv_head(blk, kv, sink_ref, q_ref, kp_ref, kc_ref, vp_ref, vc_ref, out_ref, bias_ref):
    W = WINDOW
    lo = lax.broadcasted_iota(jnp.int32, (W, 2 * A_HEAD_DIM), 1) < A_HEAD_DIM
    lo2 = lax.broadcasted_iota(jnp.int32, (2 * W, 2 * A_HEAD_DIM), 1) < A_HEAD_DIM
    zero = jnp.zeros((), BF16)
    ones_lo = jnp.where(lo2, 1.0, 0.0).astype(BF16)
    ones_hi = jnp.where(lo2, 0.0, 1.0).astype(BF16)
    nt = (((1,), (1,)), ((), ()))
    rows = slice(blk * W, (blk + 1) * W)
    var = jnp.minimum(pl.program_id(1), 1) if blk == 0 else 1
    ls = slice(kv * 2 * A_HEAD_DIM, (kv + 1) * 2 * A_HEAD_DIM)
    if blk == 0:
        k_prev, v_prev = kp_ref[:, ls], vp_ref[:, ls]
    else:
        k_prev, v_prev = kc_ref[(blk - 1) * W:blk * W, ls], vc_ref[(blk - 1) * W:blk * W, ls]
    k2 = jnp.concatenate([k_prev, kc_ref[rows, ls]], axis=0)
    v2 = jnp.concatenate([v_prev, vc_ref[rows, ls]], axis=0)
    k_even = jnp.where(lo2, k2, zero)
    k_odd = jnp.where(lo2, zero, k2)
    v_even = jnp.concatenate([jnp.where(lo2, v2, zero), ones_lo], axis=1)
    v_odd = jnp.concatenate([jnp.where(lo2, zero, v2), ones_hi], axis=1)
    for pair in range(A_GROUP // 2):
        pi = kv * (A_GROUP // 2) + pair
        cols = slice(pi * 2 * A_HEAD_DIM, (pi + 1) * 2 * A_HEAD_DIM)
        q2 = q_ref[rows, cols]
        sink_e = sink_ref[2 * pi]
        sink_o = sink_ref[2 * pi + 1]
        s_e = lax.dot_general(q2, k_even, nt, preferred_element_type=F32) + bias_ref[var, 2 * pi]
        s_o = lax.dot_general(q2, k_odd, nt, preferred_element_type=F32) + bias_ref[var, 2 * pi + 1]
        m_e = jnp.maximum(jnp.max(s_e, axis=-1, keepdims=True), sink_e)
        m_o = jnp.maximum(jnp.max(s_o, axis=-1, keepdims=True), sink_o)
        p_e = jnp.exp(s_e - m_e).astype(BF16)
        p_o = jnp.exp(s_o - m_o).astype(BF16)
        acc = (jnp.dot(p_e, v_even, preferred_element_type=F32)
               + jnp.dot(p_o, v_odd, preferred_element_type=F32))
        sink_term = jnp.where(lo, jnp.exp(sink_e - m_e), jnp.exp(sink_o - m_o))
        out_ref[rows, cols] = (acc[:, :2 * A_HEAD_DIM]
                               / (acc[:, 2 * A_HEAD_DIM:] + sink_term)).astype(out_ref.dtype)


def _mixers_kernel(sink_ref, rel_ref, bucket_ref, qk_ref, v_ref, og_ref, grow_ref, gb_ref, ng_ref,
                   q_ref, kp_ref, kc_ref, vp_ref, vc_ref,
                   hm_ref, ha_ref, c_ref, n_ref, m_ref, bias_ref):
    _mlstm_init(c_ref, n_ref, m_ref)
    _swa_init(rel_ref, bucket_ref, bias_ref)
    assert A_KV_HEADS == M_HEADS
    for blk in range(SWA_BLOCKS):
        for u in range(A_KV_HEADS):
            _swa_kv_head(blk, u, sink_ref, q_ref, kp_ref, kc_ref, vp_ref, vc_ref, ha_ref, bias_ref)
        gates = _mlstm_gates(blk, grow_ref, gb_ref)
        for u in range(M_HEADS):
            _mlstm_head(blk, u, gates, qk_ref, v_ref, og_ref, ng_ref, hm_ref, c_ref, n_ref, m_ref)


def _mixers(mqk, mv, og, g_rows, gate_b, norm_g, aq, ak2, av2, sinks, rel_bias, bucket):
    B, S, _ = aq.shape
    W = WINDOW
    tq = SWA_BLOCKS * W
    assert tq == MLSTM_BLOCKS * MLSTM_CHUNK
    kvw = 2 * A_KV_WIDTH
    cur = lambda n: pl.BlockSpec((None, tq, n), lambda b, i: (b, i, 0))
    prev = pl.BlockSpec((None, W, kvw), lambda b, i: (b, jnp.maximum(i * SWA_BLOCKS - 1, 0), 0))
    smem = pl.BlockSpec(memory_space=pltpu.SMEM)
    return pl.pallas_call(
        _mixers_kernel,
        grid=(B, S // tq),
        in_specs=[smem, smem, _const_spec((W, 2 * W)),
                  cur(2 * M_QK_WIDTH), cur(M_V_WIDTH), cur(M_V_WIDTH),
                  pl.BlockSpec((None, 2 * M_HEADS, tq), lambda b, i: (b, 0, i)),
                  _const_spec((2 * M_HEADS, 1)), _const_spec((1, M_V_WIDTH)),
                  cur(A_Q_WIDTH), prev, cur(kvw), prev, cur(kvw)],
        out_specs=[cur(M_V_WIDTH), cur(A_Q_WIDTH)],
        out_shape=[jax.ShapeDtypeStruct((B, S, M_V_WIDTH), BF16),
                   jax.ShapeDtypeStruct((B, S, A_Q_WIDTH), BF16)],
        scratch_shapes=[pltpu.VMEM((M_HEADS, M_QK_DIM, M_V_DIM), F32),
                        pltpu.VMEM((M_HEADS, 1, M_QK_DIM), F32),
                        pltpu.VMEM((M_HEADS, 1, 128), F32),
                        pltpu.VMEM((2, A_Q_HEADS, W, 2 * W), F32)],
        compiler_params=pltpu.CompilerParams(
            dimension_semantics=("arbitrary", "arbitrary"), vmem_limit_bytes=VMEM_LIMIT_BYTES),
        name="mixers",
    )(sinks, rel_bias, bucket, mqk, mv, og, g_rows, gate_b, norm_g, aq, ak2, ak2, av2, av2)


def _post_kernel(x_ref, hm_ref, ha_ref, ga_ref, gb_ref, wm_ref, wa_ref, wo_ref,
                 g_ref, wu_ref, wd_ref, fg_ref, out_ref, *, final):
    sub = PROJ_SUB
    for t in range(x_ref.shape[0] // sub):
        rows = slice(t * sub, (t + 1) * sub)
        y = (ga_ref[rows, :] * jnp.dot(hm_ref[rows, :], wm_ref[...], preferred_element_type=F32)
             + gb_ref[rows, :] * jnp.dot(ha_ref[rows, :], wa_ref[...], preferred_element_type=F32))
        x = x_ref[rows, :] + jnp.dot(y.astype(BF16), wo_ref[...], preferred_element_type=F32)
        h = _rmsnorm(x, g_ref[...]).astype(BF16)
        u = jnp.maximum(jnp.dot(h, wu_ref[...], preferred_element_type=F32), 0.0)
        x = x + jnp.dot((u * u).astype(BF16), wd_ref[...], preferred_element_type=F32)
        if final:
            x = _rmsnorm(x, fg_ref[...])
        out_ref[rows, :] = x


def _post(x2, hm, ha, ga, gb, wm, wa, wo, g, wu, wd, fg, final):
    T = x2.shape[0]
    tm = POST_TM
    row = pl.BlockSpec((tm, D_MODEL), lambda i: (i, 0))
    wspec = _const_spec((D_MODEL, D_MODEL))
    vec = _const_spec((1, D_MODEL))
    return pl.pallas_call(
        functools.partial(_post_kernel, final=final),
        grid=(T // tm,),
        in_specs=[row, row, row, row, row, wspec, wspec, wspec,
                  vec, _const_spec((D_MODEL, D_FF)), _const_spec((D_FF, D_MODEL)), vec],
        out_specs=row,
        out_shape=jax.ShapeDtypeStruct((T, D_MODEL), F32),
        compiler_params=pltpu.CompilerParams(
            dimension_semantics=("parallel",), vmem_limit_bytes=VMEM_LIMIT_BYTES),
        name="post",
    )(x2, hm, ha, ga, gb, wm, wa, wo, g, wu, wd, fg)


def _t5_bucket(n):
    max_exact = N_BUCKETS // 2
    n = np.maximum(n, 0)
    large = max_exact + (np.log(np.maximum(n, 1) / max_exact)
                         / np.log(MAX_DISTANCE / max_exact)
                         * (N_BUCKETS - max_exact)).astype(np.int32)
    large = np.minimum(large, N_BUCKETS - 1)
    return np.where(n < max_exact, n, large).astype(np.int32)


def _bucket_table():
    W = WINDOW
    dist = np.arange(W)[:, None] + W - np.arange(2 * W)[None, :]
    valid = (dist >= 0) & (dist < W)
    return jnp.asarray(np.where(valid, _t5_bucket(dist), -1).astype(np.int32))


def _dup_heads(w):
    w = w.reshape(w.shape[0], A_KV_HEADS, 1, A_HEAD_DIM)
    return jnp.broadcast_to(w, (w.shape[0], A_KV_HEADS, 2, A_HEAD_DIM)).reshape(w.shape[0], -1)


def _split_w_in(w):
    offs = np.cumsum((0,) + IN_SIZES)
    parts = [w[:, offs[i]:offs[i + 1]] for i in range(len(IN_SIZES))]
    wqk, wv, wo, wi, wf, waq, wak, wav, wga, wgb = parts
    wg = jnp.pad(jnp.concatenate([wi, wf], axis=1), ((0, 0), (0, 128 - 2 * M_HEADS)))
    waq = waq * (A_HEAD_DIM ** -0.5)
    wak, wav = _dup_heads(wak), _dup_heads(wav)
    parts = dict(qk=wqk, o=wo, ga=wga, gb=wgb, v=wv, aq=waq, ak=wak, av=wav, g=wg)
    return jnp.concatenate([parts[n] for n, _ in _INPROJ_GROUPS], axis=1).astype(BF16)


def kernel(x, norm_mix_g, w_in, conv_w, conv_b, b_igate, b_fgate, mlstm_norm_g, attn_sinks,
           rel_bias, w_branch_m, w_branch_a, w_out, norm_mlp_g, w_up, w_down, final_norm_g):
    B, S, D = x.shape
    T = B * S
    bucket = _bucket_table()
    x2 = x.reshape(T, D)
    for l in range(DEPTH):
        ws = _split_w_in(w_in[l])
        mqk, og, ga, gb, mv, aq, ak, av, gates = _inproj(
            x2, norm_mix_g[l].reshape(1, D), conv_w[l], conv_b[l].reshape(1, -1), ws, S)
        g_rows = gates.reshape(B, S, 2 * M_HEADS).transpose(0, 2, 1)
        gate_b = jnp.concatenate([b_igate[l], b_fgate[l]]).reshape(2 * M_HEADS, 1)
        hm, ha = _mixers(mqk.reshape(B, S, -1), mv.reshape(B, S, -1), og.reshape(B, S, -1), g_rows,
                         gate_b, mlstm_norm_g[l].reshape(1, -1),
                         aq.reshape(B, S, -1), ak.reshape(B, S, -1), av.reshape(B, S, -1),
                         attn_sinks[l], rel_bias, bucket)
        x2 = _post(x2, hm.reshape(T, -1), ha.reshape(T, -1), ga, gb,
                   w_branch_m[l].astype(BF16), w_branch_a[l].astype(BF16), w_out[l].astype(BF16),
                   norm_mlp_g[l].reshape(1, D), w_up[l].astype(BF16), w_down[l].astype(BF16),
                   final_norm_g.reshape(1, D), final=(l == DEPTH - 1))
    return x2.reshape(B, S, D)
```

```python
import functools

import numpy as np
import jax
import jax.numpy as jnp
from jax import lax
from jax.experimental import pallas as pl
from jax.experimental.pallas import tpu as pltpu

D_MODEL = 1024
DEPTH = 2
M_HEADS = 4
M_QK_DIM = 128
M_V_DIM = 256
M_QK_WIDTH = M_HEADS * M_QK_DIM
M_V_WIDTH = M_HEADS * M_V_DIM
CONV_K = 4
A_Q_HEADS = 16
A_KV_HEADS = 4
A_HEAD_DIM = 64
A_GROUP = A_Q_HEADS // A_KV_HEADS
A_Q_WIDTH = A_Q_HEADS * A_HEAD_DIM
A_KV_WIDTH = A_KV_HEADS * A_HEAD_DIM
WINDOW = 128
N_BUCKETS = 32
MAX_DISTANCE = 128
D_FF = 4 * D_MODEL
EPS = 1e-6
IN_SIZES = (2 * M_QK_WIDTH, M_V_WIDTH, M_V_WIDTH, M_HEADS, M_HEADS,
            A_Q_WIDTH, A_KV_WIDTH, A_KV_WIDTH, D_MODEL, D_MODEL)

INPROJ_TM = 512
POST_TM = 512
PROJ_SUB = 256
MLSTM_CHUNK = 128
MLSTM_BLOCKS = 2
CONV_HALO = 8
SWA_BLOCKS = 2
NEG_BIG = -1e30
VMEM_LIMIT_BYTES = 56 * 1024 * 1024

F32 = jnp.float32
BF16 = jnp.bfloat16


def _const_spec(shape):
    nd = len(shape)
    return pl.BlockSpec(shape, lambda *_: (0,) * nd, pipeline_mode=pl.Buffered(1))


def _rmsnorm(x, g):
    return x * lax.rsqrt(jnp.mean(x * x, axis=-1, keepdims=True) + EPS) * g


_INPROJ_GROUPS = (("qk", 2 * M_QK_WIDTH), ("v", M_V_WIDTH), ("o", M_V_WIDTH),
                  ("aq", A_Q_WIDTH), ("ak", A_KV_WIDTH), ("av", A_KV_WIDTH),
                  ("ga", D_MODEL), ("gb", D_MODEL), ("g", 128))
_INPROJ_OFFS = dict(zip((n for n, _ in _INPROJ_GROUPS),
                        np.cumsum([0] + [w for _, w in _INPROJ_GROUPS])[:-1].tolist()))
_INPROJ_WIDTH = sum(w for _, w in _INPROJ_GROUPS)


def _inproj_kernel(x_ref, g_ref, cw_ref, cb_ref, w_ref, *refs, tiles_per_seq):
    out_refs, halo_ref = refs[:-1], refs[-1]

    @pl.when(pl.program_id(0) % tiles_per_seq == 0)
    def _():
        halo_ref[...] = jnp.zeros_like(halo_ref)

    outs = dict(zip((n for n, _ in _INPROJ_GROUPS), out_refs))
    sub = PROJ_SUB
    for t in range(x_ref.shape[0] // sub):
        _inproj_rows(slice(t * sub, (t + 1) * sub), x_ref, g_ref, cw_ref, cb_ref, w_ref, outs, halo_ref)


def _inproj_rows(rows, x_ref, g_ref, cw_ref, cb_ref, w_ref, outs, halo_ref):
    n_rows = rows.stop - rows.start
    h = _rmsnorm(x_ref[rows, :], g_ref[...]).astype(BF16)
    res = jnp.dot(h, w_ref[...], preferred_element_type=F32)
    group = lambda name, width: res[:, _INPROJ_OFFS[name]:_INPROJ_OFFS[name] + width]

    head_row = lax.broadcasted_iota(jnp.int32, (CONV_HALO, 1), 0)
    gw = 2 * M_QK_WIDTH // 4
    for c in range(4):
        cols = slice(c * gw, (c + 1) * gw)
        r = res[:, cols]
        prev = halo_ref[:, cols]
        halo_ref[:, cols] = r[n_rows - CONV_HALO:, :]
        y = cb_ref[:, cols] + r * cw_ref[CONV_K - 1:CONV_K, cols]
        for s in range(1, CONV_K):
            rolled = pltpu.roll(r, s, axis=0)
            head = jnp.where(head_row < s, pltpu.roll(prev, s, axis=0), rolled[:CONV_HALO])
            shifted = jnp.concatenate([head, rolled[CONV_HALO:]], axis=0)
            y = y + shifted * cw_ref[CONV_K - 1 - s:CONV_K - s, cols]
        y = y * jax.nn.sigmoid(y)
        if c < 2:
            y = y * (M_QK_DIM ** -0.5)
        outs["qk"][rows, cols] = y.astype(BF16)

    for name, width in _INPROJ_GROUPS[1:-1]:
        val = group(name, width)
        if name in ("o", "ga", "gb"):
            val = jax.nn.sigmoid(val)
        if name in ("ak", "av"):
            half = A_HEAD_DIM
            swapped = [pltpu.roll(val[:, c:c + 2 * half], half, axis=1) for c in range(0, width, 2 * half)]
            val = jnp.concatenate([val] + swapped, axis=1)
        outs[name][rows, :] = val.astype(BF16)
    outs["g"][rows, :] = group("g", 2 * M_HEADS)


def _inproj(x2, g, conv_w, conv_b, w_all, seq_len):
    T = x2.shape[0]
    tm = INPROJ_TM
    row = lambda n: pl.BlockSpec((tm, n), lambda i: (i, 0))
    out_widths = [2 * w if n in ("ak", "av") else w for n, w in _INPROJ_GROUPS[:-1]] + [2 * M_HEADS]
    out_dtypes = [BF16] * (len(_INPROJ_GROUPS) - 1) + [F32]
    return pl.pallas_call(
        functools.partial(_inproj_kernel, tiles_per_seq=seq_len // tm),
        grid=(T // tm,),
        in_specs=[row(D_MODEL), _const_spec((1, D_MODEL)), _const_spec((CONV_K, 2 * M_QK_WIDTH)),
                  _const_spec((1, 2 * M_QK_WIDTH)), _const_spec((D_MODEL, _INPROJ_WIDTH))],
        out_specs=[row(n) for n in out_widths],
        out_shape=[jax.ShapeDtypeStruct((T, n), d) for n, d in zip(out_widths, out_dtypes)],
        scratch_shapes=[pltpu.VMEM((CONV_HALO, 2 * M_QK_WIDTH), F32)],
        compiler_params=pltpu.CompilerParams(
            dimension_semantics=("arbitrary",), vmem_limit_bytes=VMEM_LIMIT_BYTES),
        name="inproj",
    )(x2, g, conv_w, conv_b, w_all)


def _mlstm_init(c_ref, n_ref, m_ref):
    @pl.when(pl.program_id(1) == 0)
    def _():
        c_ref[...] = jnp.zeros_like(c_ref)
        n_ref[...] = jnp.zeros_like(n_ref)
        m_ref[...] = jnp.zeros_like(m_ref)


def _mlstm_gates(blk, grow_ref, gb_ref):
    L = MLSTM_CHUNK
    rows = slice(blk * L, (blk + 1) * L)
    row_i = lax.broadcasted_iota(jnp.int32, (L, L), 0)
    col_j = lax.broadcasted_iota(jnp.int32, (L, L), 1)
    triu_f = (row_i <= col_j).astype(F32)
    g_rows = grow_ref[:, rows] + gb_ref[...]
    logf_rows = jax.nn.log_sigmoid(g_rows)
    b_rows = jnp.dot(logf_rows, triu_f, preferred_element_type=F32,
                     precision=lax.Precision.HIGHEST)
    return g_rows, logf_rows, b_rows


def _mlstm_head(blk, h, gates, qk_ref, v_ref, og_ref, ng_ref, out_ref, c_ref, n_ref, m_ref):
    L = MLSTM_CHUNK
    rows = slice(blk * L, (blk + 1) * L)
    row_i = lax.broadcasted_iota(jnp.int32, (L, L), 0)
    col_j = lax.broadcasted_iota(jnp.int32, (L, L), 1)
    tril = col_j <= row_i
    eye = col_j == row_i
    g_rows, logf_rows, b_rows = gates
    qb = qk_ref[rows, h * M_QK_DIM:(h + 1) * M_QK_DIM]
    kb = qk_ref[rows, M_QK_WIDTH + h * M_QK_DIM:M_QK_WIDTH + (h + 1) * M_QK_DIM]
    v = v_ref[rows, h * M_V_DIM:(h + 1) * M_V_DIM]
    q = qb.astype(F32)
    k = kb.astype(F32)

    m_prev = m_ref[h][:, 0:1]
    n_prev = n_ref[h]
    c_prev = c_ref[h]

    logf_row = logf_rows[M_HEADS + h:M_HEADS + h + 1, :]
    b_row = b_rows[M_HEADS + h:M_HEADS + h + 1, :]
    a_row = g_rows[h:h + 1, :] - b_row
    b_col = jnp.sum(jnp.where(tril, logf_row, 0.0), axis=-1, keepdims=True)
    a_col = jnp.sum(jnp.where(eye, a_row, 0.0), axis=-1, keepdims=True)
    b_last = jnp.sum(logf_row, axis=-1, keepdims=True)

    a_mat = jnp.where(tril, a_row, -jnp.inf)
    m_row = jnp.maximum(m_prev, jnp.max(a_mat, axis=-1, keepdims=True))
    w_intra = jnp.exp(a_mat - m_row)
    w_inter = jnp.exp(m_prev - m_row)

    s = lax.dot_general(qb, kb, (((1,), (1,)), ((), ())),
                        preferred_element_type=F32) * w_intra
    num = (jnp.dot(s.astype(BF16), v, preferred_element_type=F32)
           + w_inter * jnp.dot(qb, c_prev.astype(BF16), preferred_element_type=F32))
    den = (jnp.sum(s, axis=-1, keepdims=True)
           + w_inter * jnp.sum(q * n_prev, axis=-1, keepdims=True))
    hh = num / jnp.maximum(jnp.abs(den), jnp.exp(-(b_col + m_row)))

    m_last = jnp.maximum(m_prev, jnp.max(a_row, axis=-1, keepdims=True))
    decay = jnp.exp(m_prev - m_last)
    kw = k * jnp.exp(a_col - m_last)
    c_ref[h] = decay * c_prev + lax.dot_general(
        kw.astype(BF16), v, (((0,), (0,)), ((), ())), preferred_element_type=F32)
    n_ref[h] = decay * n_prev + jnp.sum(kw, axis=0, keepdims=True)
    m_ref[h] = jnp.broadcast_to(b_last + m_last, (1, 128))

    sl = slice(h * M_V_DIM, (h + 1) * M_V_DIM)
    hn = hh * lax.rsqrt(jnp.mean(hh * hh, axis=-1, keepdims=True) + EPS) * ng_ref[:, sl]
    out_ref[rows, sl] = hn.astype(BF16) * og_ref[rows, sl]


def _swa_init(rel_ref, bucket_ref, bias_ref):
    W = WINDOW

    @pl.when((pl.program_id(0) == 0) & (pl.program_id(1) == 0))
    def _():
        bucket = bucket_ref[...]
        neg = jnp.full((W, 2 * W), NEG_BIG, F32)
        for hq in range(A_Q_HEADS):
            bias_ref[1, hq] = neg

        def fill(bb, carry):
            hit = bucket == bb
            for hq in range(A_Q_HEADS):
                bias_ref[1, hq] = jnp.where(hit, rel_ref[bb, hq], bias_ref[1, hq])
            return carry
        lax.fori_loop(0, N_BUCKETS, fill, 0)
        is_prev = lax.broadcasted_iota(jnp.int32, (W, 2 * W), 1) < W
        for hq in range(A_Q_HEADS):
            bias_ref[0, hq] = jnp.where(is_prev, NEG_BIG, bias_ref[1, hq])


def _swa_kv_head(blk, kv, sink_ref, q_ref, kp_ref, kc_ref, vp_ref, vc_ref, out_ref, bias_ref):
    W = WINDOW
    lo = lax.broadcasted_iota(jnp.int32, (W, 2 * A_HEAD_DIM), 1) < A_HEAD_DIM
    lo2 = lax.broadcasted_iota(jnp.int32, (2 * W, 2 * A_HEAD_DIM), 1) < A_HEAD_DIM
    zero = jnp.zeros((), BF16)
    ones_lo = jnp.where(lo2, 1.0, 0.0).astype(BF16)
    ones_hi = jnp.where(lo2, 0.0, 1.0).astype(BF16)
    nt = (((1,), (1,)), ((), ()))
    rows = slice(blk * W, (blk + 1) * W)
    var = jnp.minimum(pl.program_id(1), 1) if blk == 0 else 1
    def tile(ref, rws, swapped):
        c0 = (A_KV_WIDTH if swapped else 0) + (kv // 2) * 2 * A_HEAD_DIM
        return ref[rws, c0:c0 + 2 * A_HEAD_DIM]

    def prev_cur(p_ref, c_ref, swapped):
        prev = (tile(p_ref, slice(None), swapped) if blk == 0
                else tile(c_ref, slice((blk - 1) * W, blk * W), swapped))
        return jnp.concatenate([prev, tile(c_ref, rows, swapped)], axis=0)

    in_low = kv % 2 == 1
    k_even = jnp.where(lo2, prev_cur(kp_ref, kc_ref, in_low), zero)
    k_odd = jnp.where(lo2, zero, prev_cur(kp_ref, kc_ref, not in_low))
    v_even = jnp.concatenate([jnp.where(lo2, prev_cur(vp_ref, vc_ref, in_low), zero), ones_lo], axis=1)
    v_odd = jnp.concatenate([jnp.where(lo2, zero, prev_cur(vp_ref, vc_ref, not in_low)), ones_hi], axis=1)
    for pair in range(A_GROUP // 2):
        pi = kv * (A_GROUP // 2) + pair
        cols = slice(pi * 2 * A_HEAD_DIM, (pi + 1) * 2 * A_HEAD_DIM)
        q2 = q_ref[rows, cols]
        sink_e = sink_ref[2 * pi]
        sink_o = sink_ref[2 * pi + 1]
        s_e = lax.dot_general(q2, k_even, nt, preferred_element_type=F32) + bias_ref[var, 2 * pi]
        s_o = lax.dot_general(q2, k_odd, nt, preferred_element_type=F32) + bias_ref[var, 2 * pi + 1]
        m_e = jnp.maximum(jnp.max(s_e, axis=-1, keepdims=True), sink_e)
        m_o = jnp.maximum(jnp.max(s_o, axis=-1, keepdims=True), sink_o)
        p_e = jnp.exp(s_e - m_e).astype(BF16)
        p_o = jnp.exp(s_o - m_o).astype(BF16)
        acc = (jnp.dot(p_e, v_even, preferred_element_type=F32)
               + jnp.dot(p_o, v_odd, preferred_element_type=F32))
        sink_term = jnp.where(lo, jnp.exp(sink_e - m_e), jnp.exp(sink_o - m_o))
        out_ref[rows, cols] = (acc[:, :2 * A_HEAD_DIM]
                               / (acc[:, 2 * A_HEAD_DIM:] + sink_term)).astype(out_ref.dtype)


def _mixers_kernel(sink_ref, rel_ref, bucket_ref, qk_ref, v_ref, og_ref, grow_ref, gb_ref, ng_ref,
                   q_ref, kp_ref, kc_ref, vp_ref, vc_ref,
                   hm_ref, ha_ref, c_ref, n_ref, m_ref, bias_ref):
    _mlstm_init(c_ref, n_ref, m_ref)
    _swa_init(rel_ref, bucket_ref, bias_ref)
    assert A_KV_HEADS == M_HEADS
    for blk in range(SWA_BLOCKS):
        for u in range(A_KV_HEADS):
            _swa_kv_head(blk, u, sink_ref, q_ref, kp_ref, kc_ref, vp_ref, vc_ref, ha_ref, bias_ref)
        gates = _mlstm_gates(blk, grow_ref, gb_ref)
        for u in range(M_HEADS):
            _mlstm_head(blk, u, gates, qk_ref, v_ref, og_ref, ng_ref, hm_ref, c_ref, n_ref, m_ref)


def _mixers(mqk, mv, og, g_rows, gate_b, norm_g, aq, ak2, av2, sinks, rel_bias, bucket):
    B, S, _ = aq.shape
    W = WINDOW
    tq = SWA_BLOCKS * W
    assert tq == MLSTM_BLOCKS * MLSTM_CHUNK
    kvw = 2 * A_KV_WIDTH
    cur = lambda n: pl.BlockSpec((None, tq, n), lambda b, i: (b, i, 0))
    prev = pl.BlockSpec((None, W, kvw), lambda b, i: (b, jnp.maximum(i * SWA_BLOCKS - 1, 0), 0))
    smem = pl.BlockSpec(memory_space=pltpu.SMEM)
    return pl.pallas_call(
        _mixers_kernel,
        grid=(B, S // tq),
        in_specs=[smem, smem, _const_spec((W, 2 * W)),
                  cur(2 * M_QK_WIDTH), cur(M_V_WIDTH), cur(M_V_WIDTH),
                  pl.BlockSpec((None, 2 * M_HEADS, tq), lambda b, i: (b, 0, i)),
                  _const_spec((2 * M_HEADS, 1)), _const_spec((1, M_V_WIDTH)),
                  cur(A_Q_WIDTH), prev, cur(kvw), prev, cur(kvw)],
        out_specs=[cur(M_V_WIDTH), cur(A_Q_WIDTH)],
        out_shape=[jax.ShapeDtypeStruct((B, S, M_V_WIDTH), BF16),
                   jax.ShapeDtypeStruct((B, S, A_Q_WIDTH), BF16)],
        scratch_shapes=[pltpu.VMEM((M_HEADS, M_QK_DIM, M_V_DIM), F32),
                        pltpu.VMEM((M_HEADS, 1, M_QK_DIM), F32),
                        pltpu.VMEM((M_HEADS, 1, 128), F32),
                        pltpu.VMEM((2, A_Q_HEADS, W, 2 * W), F32)],
        compiler_params=pltpu.CompilerParams(
            dimension_semantics=("arbitrary", "arbitrary"), vmem_limit_bytes=VMEM_LIMIT_BYTES),
        name="mixers",
    )(sinks, rel_bias, bucket, mqk, mv, og, g_rows, gate_b, norm_g, aq, ak2, ak2, av2, av2)


def _post_kernel(x_ref, hm_ref, ha_ref, ga_ref, gb_ref, wm_ref, wa_ref, wo_ref,
                 g_ref, wu_ref, wd_ref, fg_ref, out_ref, *, final):
    sub = PROJ_SUB
    for t in range(x_ref.shape[0] // sub):
        rows = slice(t * sub, (t + 1) * sub)
        y = (ga_ref[rows, :] * jnp.dot(hm_ref[rows, :], wm_ref[...], preferred_element_type=F32)
             + gb_ref[rows, :] * jnp.dot(ha_ref[rows, :], wa_ref[...], preferred_element_type=F32))
        x = x_ref[rows, :] + jnp.dot(y.astype(BF16), wo_ref[...], preferred_element_type=F32)
        h = _rmsnorm(x, g_ref[...]).astype(BF16)
        u = jnp.maximum(jnp.dot(h, wu_ref[...], preferred_element_type=F32), 0.0)
        x = x + jnp.dot((u * u).astype(BF16), wd_ref[...], preferred_element_type=F32)
        if final:
            x = _rmsnorm(x, fg_ref[...])
        out_ref[rows, :] = x


def _post(x2, hm, ha, ga, gb, wm, wa, wo, g, wu, wd, fg, final):
    T = x2.shape[0]
    tm = POST_TM
    row = pl.BlockSpec((tm, D_MODEL), lambda i: (i, 0))
    wspec = _const_spec((D_MODEL, D_MODEL))
    vec = _const_spec((1, D_MODEL))
    return pl.pallas_call(
        functools.partial(_post_kernel, final=final),
        grid=(T // tm,),
        in_specs=[row, row, row, row, row, wspec, wspec, wspec,
                  vec, _const_spec((D_MODEL, D_FF)), _const_spec((D_FF, D_MODEL)), vec],
        out_specs=row,
        out_shape=jax.ShapeDtypeStruct((T, D_MODEL), F32),
        compiler_params=pltpu.CompilerParams(
            dimension_semantics=("parallel",), vmem_limit_bytes=VMEM_LIMIT_BYTES),
        name="post",
    )(x2, hm, ha, ga, gb, wm, wa, wo, g, wu, wd, fg)


def _t5_bucket(n):
    max_exact = N_BUCKETS // 2
    n = np.maximum(n, 0)
    large = max_exact + (np.log(np.maximum(n, 1) / max_exact)
                         / np.log(MAX_DISTANCE / max_exact)
                         * (N_BUCKETS - max_exact)).astype(np.int32)
    large = np.minimum(large, N_BUCKETS - 1)
    return np.where(n < max_exact, n, large).astype(np.int32)


def _bucket_table():
    W = WINDOW
    dist = np.arange(W)[:, None] + W - np.arange(2 * W)[None, :]
    valid = (dist >= 0) & (dist < W)
    return jnp.asarray(np.where(valid, _t5_bucket(dist), -1).astype(np.int32))


def _regroup_w_in(w):
    n_gate = 2 * M_HEADS
    gate0 = sum(IN_SIZES[:3])
    head = w[:, :gate0]
    tail = w[:, gate0 + n_gate:]
    col_scale = np.ones((tail.shape[1],), np.float32)
    col_scale[:A_Q_WIDTH] = A_HEAD_DIM ** -0.5
    gates = jnp.pad(w[:, gate0:gate0 + n_gate], ((0, 0), (0, 128 - n_gate)))
    return jnp.concatenate([head, tail * col_scale, gates], axis=1).astype(BF16)


def kernel(x, norm_mix_g, w_in, conv_w, conv_b, b_igate, b_fgate, mlstm_norm_g, attn_sinks,
           rel_bias, w_branch_m, w_branch_a, w_out, norm_mlp_g, w_up, w_down, final_norm_g):
    B, S, D = x.shape
    T = B * S
    bucket = _bucket_table()
    x2 = x.reshape(T, D)
    for l in range(DEPTH):
        mqk, mv, og, aq, ak, av, ga, gb, gates = _inproj(
            x2, norm_mix_g[l].reshape(1, D), conv_w[l], conv_b[l].reshape(1, -1),
            _regroup_w_in(w_in[l]), S)
        g_rows = gates.reshape(B, S, 2 * M_HEADS).transpose(0, 2, 1)
        gate_b = jnp.concatenate([b_igate[l], b_fgate[l]]).reshape(2 * M_HEADS, 1)
        hm, ha = _mixers(mqk.reshape(B, S, -1), mv.reshape(B, S, -1), og.reshape(B, S, -1), g_rows,
                         gate_b, mlstm_norm_g[l].reshape(1, -1),
                         aq.reshape(B, S, -1), ak.reshape(B, S, -1), av.reshape(B, S, -1),
                         attn_sinks[l], rel_bias, bucket)
        x2 = _post(x2, hm.reshape(T, -1), ha.reshape(T, -1), ga, gb,
                   w_branch_m[l].astype(BF16), w_branch_a[l].astype(BF16), w_out[l].astype(BF16),
                   norm_mlp_g[l].reshape(1, D), w_up[l].astype(BF16), w_down[l].astype(BF16),
                   final_norm_g.reshape(1, D), final=(l == DEPTH - 1))
    return x2.reshape(B, S, D)
```

```python
import functools

import numpy as np
import jax
import jax.numpy as jnp
from jax import lax
from jax.experimental import pallas as pl
from jax.experimental.pallas import tpu as pltpu

D_MODEL = 1024
DEPTH = 2
M_HEADS = 4
M_QK_DIM = 128
M_V_DIM = 256
M_QK_WIDTH = M_HEADS * M_QK_DIM
M_V_WIDTH = M_HEADS * M_V_DIM
CONV_K = 4
A_Q_HEADS = 16
A_KV_HEADS = 4
A_HEAD_DIM = 64
A_GROUP = A_Q_HEADS // A_KV_HEADS
A_Q_WIDTH = A_Q_HEADS * A_HEAD_DIM
A_KV_WIDTH = A_KV_HEADS * A_HEAD_DIM
WINDOW = 128
N_BUCKETS = 32
MAX_DISTANCE = 128
D_FF = 4 * D_MODEL
EPS = 1e-6
IN_SIZES = (2 * M_QK_WIDTH, M_V_WIDTH, M_V_WIDTH, M_HEADS, M_HEADS,
            A_Q_WIDTH, A_KV_WIDTH, A_KV_WIDTH, D_MODEL, D_MODEL)

INPROJ_TM = 512
POST_TM = 512
PROJ_SUB = 256
REGROUP_ROWS = 256
MLSTM_CHUNK = 128
MLSTM_BLOCKS = 2
CONV_HALO = 8
SWA_BLOCKS = 2
NEG_BIG = -1e30
VMEM_LIMIT_BYTES = 56 * 1024 * 1024

F32 = jnp.float32
BF16 = jnp.bfloat16


def _const_spec(shape):
    nd = len(shape)
    return pl.BlockSpec(shape, lambda *_: (0,) * nd, pipeline_mode=pl.Buffered(1))


def _rmsnorm(x, g):
    return x * lax.rsqrt(jnp.mean(x * x, axis=-1, keepdims=True) + EPS) * g


_INPROJ_GROUPS = (("qk", 2 * M_QK_WIDTH), ("v", M_V_WIDTH), ("o", M_V_WIDTH),
                  ("aq", A_Q_WIDTH), ("ak", A_KV_WIDTH), ("av", A_KV_WIDTH),
                  ("ga", D_MODEL), ("gb", D_MODEL), ("g", 128))
_INPROJ_OFFS = dict(zip((n for n, _ in _INPROJ_GROUPS),
                        np.cumsum([0] + [w for _, w in _INPROJ_GROUPS])[:-1].tolist()))
_INPROJ_WIDTH = sum(w for _, w in _INPROJ_GROUPS)


def _inproj_kernel(x_ref, g_ref, cw_ref, cb_ref, w_ref, *refs, tiles_per_seq):
    out_refs, halo_ref = refs[:-1], refs[-1]

    @pl.when(pl.program_id(0) % tiles_per_seq == 0)
    def _():
        halo_ref[...] = jnp.zeros_like(halo_ref)

    outs = dict(zip((n for n, _ in _INPROJ_GROUPS), out_refs))
    sub = PROJ_SUB
    for t in range(x_ref.shape[0] // sub):
        _inproj_rows(slice(t * sub, (t + 1) * sub), x_ref, g_ref, cw_ref, cb_ref, w_ref, outs, halo_ref)


def _inproj_rows(rows, x_ref, g_ref, cw_ref, cb_ref, w_ref, outs, halo_ref):
    n_rows = rows.stop - rows.start
    h = _rmsnorm(x_ref[rows, :], g_ref[...]).astype(BF16)
    res = jnp.dot(h, w_ref[...], preferred_element_type=F32)
    group = lambda name, width: res[:, _INPROJ_OFFS[name]:_INPROJ_OFFS[name] + width]

    head_row = lax.broadcasted_iota(jnp.int32, (CONV_HALO, 1), 0)
    gw = 2 * M_QK_WIDTH // 4
    for c in range(4):
        cols = slice(c * gw, (c + 1) * gw)
        r = res[:, cols]
        prev = halo_ref[:, cols]
        halo_ref[:, cols] = r[n_rows - CONV_HALO:, :]
        y = cb_ref[:, cols] + r * cw_ref[CONV_K - 1:CONV_K, cols]
        for s in range(1, CONV_K):
            rolled = pltpu.roll(r, s, axis=0)
            head = jnp.where(head_row < s, pltpu.roll(prev, s, axis=0), rolled[:CONV_HALO])
            shifted = jnp.concatenate([head, rolled[CONV_HALO:]], axis=0)
            y = y + shifted * cw_ref[CONV_K - 1 - s:CONV_K - s, cols]
        y = y * jax.nn.sigmoid(y)
        if c < 2:
            y = y * (M_QK_DIM ** -0.5)
        outs["qk"][rows, cols] = y.astype(BF16)

    for name, width in _INPROJ_GROUPS[1:-1]:
        val = group(name, width)
        if name in ("o", "ga", "gb"):
            val = jax.nn.sigmoid(val)
        if name in ("ak", "av"):
            half = A_HEAD_DIM
            swapped = [pltpu.roll(val[:, c:c + 2 * half], half, axis=1) for c in range(0, width, 2 * half)]
            val = jnp.concatenate([val] + swapped, axis=1)
        outs[name][rows, :] = val.astype(BF16)
    outs["g"][:, rows] = group("g", 128).T[:2 * M_HEADS, :]


def _inproj(x2, g, conv_w, conv_b, w_all, seq_len):
    T = x2.shape[0]
    tm = INPROJ_TM
    row = lambda n: pl.BlockSpec((tm, n), lambda i: (i, 0))
    out_widths = [2 * w if n in ("ak", "av") else w for n, w in _INPROJ_GROUPS[:-1]] + [2 * M_HEADS]
    return pl.pallas_call(
        functools.partial(_inproj_kernel, tiles_per_seq=seq_len // tm),
        grid=(T // tm,),
        in_specs=[row(D_MODEL), _const_spec((1, D_MODEL)), _const_spec((CONV_K, 2 * M_QK_WIDTH)),
                  _const_spec((1, 2 * M_QK_WIDTH)), _const_spec((D_MODEL, _INPROJ_WIDTH))],
        out_specs=[row(n) for n in out_widths[:-1]] + [pl.BlockSpec((2 * M_HEADS, tm), lambda i: (0, i))],
        out_shape=([jax.ShapeDtypeStruct((T, n), BF16) for n in out_widths[:-1]]
                   + [jax.ShapeDtypeStruct((2 * M_HEADS, T), F32)]),
        scratch_shapes=[pltpu.VMEM((CONV_HALO, 2 * M_QK_WIDTH), F32)],
        compiler_params=pltpu.CompilerParams(
            dimension_semantics=("arbitrary",), vmem_limit_bytes=VMEM_LIMIT_BYTES),
        name="inproj",
    )(x2, g, conv_w, conv_b, w_all)


def _mlstm_init(c_ref, n_ref, m_ref):
    @pl.when(pl.program_id(1) == 0)
    def _():
        c_ref[...] = jnp.zeros_like(c_ref)
        n_ref[...] = jnp.zeros_like(n_ref)
        m_ref[...] = jnp.zeros_like(m_ref)


def _mlstm_gates(blk, grow_ref, gb_ref):
    L = MLSTM_CHUNK
    rows = slice(blk * L, (blk + 1) * L)
    row_i = lax.broadcasted_iota(jnp.int32, (L, L), 0)
    col_j = lax.broadcasted_iota(jnp.int32, (L, L), 1)
    triu_f = (row_i <= col_j).astype(F32)
    g_rows = grow_ref[:, rows] + gb_ref[...]
    logf_rows = jax.nn.log_sigmoid(g_rows)
    b_rows = jnp.dot(logf_rows, triu_f, preferred_element_type=F32,
                     precision=lax.Precision.HIGHEST)
    return g_rows, logf_rows, b_rows


def _mlstm_head(blk, h, gates, qk_ref, v_ref, og_ref, ng_ref, out_ref, c_ref, n_ref, m_ref):
    L = MLSTM_CHUNK
    rows = slice(blk * L, (blk + 1) * L)
    row_i = lax.broadcasted_iota(jnp.int32, (L, L), 0)
    col_j = lax.broadcasted_iota(jnp.int32, (L, L), 1)
    tril = col_j <= row_i
    eye = col_j == row_i
    g_rows, logf_rows, b_rows = gates
    qb = qk_ref[rows, h * M_QK_DIM:(h + 1) * M_QK_DIM]
    kb = qk_ref[rows, M_QK_WIDTH + h * M_QK_DIM:M_QK_WIDTH + (h + 1) * M_QK_DIM]
    v = v_ref[rows, h * M_V_DIM:(h + 1) * M_V_DIM]
    q = qb.astype(F32)
    k = kb.astype(F32)

    m_prev = m_ref[h][:, 0:1]
    n_prev = n_ref[h]
    c_prev = c_ref[h]

    logf_row = logf_rows[M_HEADS + h:M_HEADS + h + 1, :]
    b_row = b_rows[M_HEADS + h:M_HEADS + h + 1, :]
    a_row = g_rows[h:h + 1, :] - b_row
    b_col = jnp.sum(jnp.where(tril, logf_row, 0.0), axis=-1, keepdims=True)
    a_col = jnp.sum(jnp.where(eye, a_row, 0.0), axis=-1, keepdims=True)
    b_last = jnp.sum(logf_row, axis=-1, keepdims=True)

    a_mat = jnp.where(tril, a_row, -jnp.inf)
    m_row = jnp.maximum(m_prev, jnp.max(a_mat, axis=-1, keepdims=True))
    w_intra = jnp.exp(a_mat - m_row)
    w_inter = jnp.exp(m_prev - m_row)

    s = lax.dot_general(qb, kb, (((1,), (1,)), ((), ())),
                        preferred_element_type=F32) * w_intra
    num = (jnp.dot(s.astype(BF16), v, preferred_element_type=F32)
           + w_inter * jnp.dot(qb, c_prev.astype(BF16), preferred_element_type=F32))
    den = (jnp.sum(s, axis=-1, keepdims=True)
           + w_inter * jnp.sum(q * n_prev, axis=-1, keepdims=True))
    hh = num / jnp.maximum(jnp.abs(den), jnp.exp(-(b_col + m_row)))

    m_last = jnp.maximum(m_prev, jnp.max(a_row, axis=-1, keepdims=True))
    decay = jnp.exp(m_prev - m_last)
    kw = k * jnp.exp(a_col - m_last)
    c_ref[h] = decay * c_prev + lax.dot_general(
        kw.astype(BF16), v, (((0,), (0,)), ((), ())), preferred_element_type=F32)
    n_ref[h] = decay * n_prev + jnp.sum(kw, axis=0, keepdims=True)
    m_ref[h] = jnp.broadcast_to(b_last + m_last, (1, 128))

    sl = slice(h * M_V_DIM, (h + 1) * M_V_DIM)
    hn = hh * lax.rsqrt(jnp.mean(hh * hh, axis=-1, keepdims=True) + EPS) * ng_ref[:, sl]
    out_ref[rows, sl] = hn.astype(BF16) * og_ref[rows, sl]


def _swa_init(rel_ref, bucket_ref, bias_ref):
    W = WINDOW

    @pl.when((pl.program_id(0) == 0) & (pl.program_id(1) == 0))
    def _():
        bucket = bucket_ref[...]
        neg = jnp.full((W, 2 * W), NEG_BIG, F32)
        for hq in range(A_Q_HEADS):
            bias_ref[1, hq] = neg

        def fill(bb, carry):
            hit = bucket == bb
            for hq in range(A_Q_HEADS):
                bias_ref[1, hq] = jnp.where(hit, rel_ref[bb, hq], bias_ref[1, hq])
            return carry
        lax.fori_loop(0, N_BUCKETS, fill, 0)
        is_prev = lax.broadcasted_iota(jnp.int32, (W, 2 * W), 1) < W
        for hq in range(A_Q_HEADS):
            bias_ref[0, hq] = jnp.where(is_prev, NEG_BIG, bias_ref[1, hq])


def _swa_kv_head(blk, kv, sink_ref, q_ref, kp_ref, kc_ref, vp_ref, vc_ref, out_ref, bias_ref):
    W = WINDOW
    lo = lax.broadcasted_iota(jnp.int32, (W, 2 * A_HEAD_DIM), 1) < A_HEAD_DIM
    lo2 = lax.broadcasted_iota(jnp.int32, (2 * W, 2 * A_HEAD_DIM), 1) < A_HEAD_DIM
    zero = jnp.zeros((), BF16)
    ones_lo = jnp.where(lo2, 1.0, 0.0).astype(BF16)
    ones_hi = jnp.where(lo2, 0.0, 1.0).astype(BF16)
    nt = (((1,), (1,)), ((), ()))
    rows = slice(blk * W, (blk + 1) * W)
    var = jnp.minimum(pl.program_id(1), 1) if blk == 0 else 1
    def tile(ref, rws, swapped):
        c0 = (A_KV_WIDTH if swapped else 0) + (kv // 2) * 2 * A_HEAD_DIM
        return ref[rws, c0:c0 + 2 * A_HEAD_DIM]

    def prev_cur(p_ref, c_ref, swapped):
        prev = (tile(p_ref, slice(None), swapped) if blk == 0
                else tile(c_ref, slice((blk - 1) * W, blk * W), swapped))
        return jnp.concatenate([prev, tile(c_ref, rows, swapped)], axis=0)

    in_low = kv % 2 == 1
    k_even = jnp.where(lo2, prev_cur(kp_ref, kc_ref, in_low), zero)
    k_odd = jnp.where(lo2, zero, prev_cur(kp_ref, kc_ref, not in_low))
    v_even = jnp.concatenate([jnp.where(lo2, prev_cur(vp_ref, vc_ref, in_low), zero), ones_lo], axis=1)
    v_odd = jnp.concatenate([jnp.where(lo2, zero, prev_cur(vp_ref, vc_ref, not in_low)), ones_hi], axis=1)
    for pair in range(A_GROUP // 2):
        pi = kv * (A_GROUP // 2) + pair
        cols = slice(pi * 2 * A_HEAD_DIM, (pi + 1) * 2 * A_HEAD_DIM)
        q2 = q_ref[rows, cols]
        sink_e = sink_ref[2 * pi]
        sink_o = sink_ref[2 * pi + 1]
        s_e = lax.dot_general(q2, k_even, nt, preferred_element_type=F32) + bias_ref[var, 2 * pi]
        s_o = lax.dot_general(q2, k_odd, nt, preferred_element_type=F32) + bias_ref[var, 2 * pi + 1]
        m_e = jnp.maximum(jnp.max(s_e, axis=-1, keepdims=True), sink_e)
        m_o = jnp.maximum(jnp.max(s_o, axis=-1, keepdims=True), sink_o)
        p_e = jnp.exp(s_e - m_e).astype(BF16)
        p_o = jnp.exp(s_o - m_o).astype(BF16)
        acc = (jnp.dot(p_e, v_even, preferred_element_type=F32)
               + jnp.dot(p_o, v_odd, preferred_element_type=F32))
        sink_term = jnp.where(lo, jnp.exp(sink_e - m_e), jnp.exp(sink_o - m_o))
        out_ref[rows, cols] = (acc[:, :2 * A_HEAD_DIM]
                               / (acc[:, 2 * A_HEAD_DIM:] + sink_term)).astype(out_ref.dtype)


def _mixers_kernel(sink_ref, rel_ref, bucket_ref, qk_ref, v_ref, og_ref, grow_ref, gb_ref, ng_ref,
                   q_ref, kp_ref, kc_ref, vp_ref, vc_ref,
                   hm_ref, ha_ref, c_ref, n_ref, m_ref, bias_ref):
    _mlstm_init(c_ref, n_ref, m_ref)
    _swa_init(rel_ref, bucket_ref, bias_ref)
    assert A_KV_HEADS == M_HEADS
    for blk in range(SWA_BLOCKS):
        for u in range(A_KV_HEADS):
            _swa_kv_head(blk, u, sink_ref, q_ref, kp_ref, kc_ref, vp_ref, vc_ref, ha_ref, bias_ref)
        gates = _mlstm_gates(blk, grow_ref, gb_ref)
        for u in range(M_HEADS):
            _mlstm_head(blk, u, gates, qk_ref, v_ref, og_ref, ng_ref, hm_ref, c_ref, n_ref, m_ref)


def _mixers(mqk, mv, og, g_rows, gate_b, norm_g, aq, ak2, av2, sinks, rel_bias, bucket):
    B, S, _ = aq.shape
    W = WINDOW
    tq = SWA_BLOCKS * W
    assert tq == MLSTM_BLOCKS * MLSTM_CHUNK
    kvw = 2 * A_KV_WIDTH
    cur = lambda n: pl.BlockSpec((None, tq, n), lambda b, i: (b, i, 0))
    prev = pl.BlockSpec((None, W, kvw), lambda b, i: (b, jnp.maximum(i * SWA_BLOCKS - 1, 0), 0))
    smem = pl.BlockSpec(memory_space=pltpu.SMEM)
    return pl.pallas_call(
        _mixers_kernel,
        grid=(B, S // tq),
        in_specs=[smem, smem, _const_spec((W, 2 * W)),
                  cur(2 * M_QK_WIDTH), cur(M_V_WIDTH), cur(M_V_WIDTH),
                  pl.BlockSpec((2 * M_HEADS, tq), lambda b, i: (0, b * (S // tq) + i)),
                  _const_spec((2 * M_HEADS, 1)), _const_spec((1, M_V_WIDTH)),
                  cur(A_Q_WIDTH), prev, cur(kvw), prev, cur(kvw)],
        out_specs=[cur(M_V_WIDTH), cur(A_Q_WIDTH)],
        out_shape=[jax.ShapeDtypeStruct((B, S, M_V_WIDTH), BF16),
                   jax.ShapeDtypeStruct((B, S, A_Q_WIDTH), BF16)],
        scratch_shapes=[pltpu.VMEM((M_HEADS, M_QK_DIM, M_V_DIM), F32),
                        pltpu.VMEM((M_HEADS, 1, M_QK_DIM), F32),
                        pltpu.VMEM((M_HEADS, 1, 128), F32),
                        pltpu.VMEM((2, A_Q_HEADS, W, 2 * W), F32)],
        compiler_params=pltpu.CompilerParams(
            dimension_semantics=("arbitrary", "arbitrary"), vmem_limit_bytes=VMEM_LIMIT_BYTES),
        name="mixers",
    )(sinks, rel_bias, bucket, mqk, mv, og, g_rows, gate_b, norm_g, aq, ak2, ak2, av2, av2)


def _post_kernel(x_ref, hm_ref, ha_ref, ga_ref, gb_ref, wm_ref, wa_ref, wo_ref,
                 g_ref, wu_ref, wd_ref, fg_ref, out_ref, *, final):
    sub = PROJ_SUB
    for t in range(x_ref.shape[0] // sub):
        rows = slice(t * sub, (t + 1) * sub)
        y = (ga_ref[rows, :] * jnp.dot(hm_ref[rows, :], wm_ref[...], preferred_element_type=F32)
             + gb_ref[rows, :] * jnp.dot(ha_ref[rows, :], wa_ref[...], preferred_element_type=F32))
        x = x_ref[rows, :] + jnp.dot(y.astype(BF16), wo_ref[...], preferred_element_type=F32)
        h = _rmsnorm(x, g_ref[...]).astype(BF16)
        u = jnp.maximum(jnp.dot(h, wu_ref[...], preferred_element_type=F32), 0.0)
        x = x + jnp.dot((u * u).astype(BF16), wd_ref[...], preferred_element_type=F32)
        if final:
            x = _rmsnorm(x, fg_ref[...])
        out_ref[rows, :] = x


def _post(x2, hm, ha, ga, gb, wm, wa, wo, g, wu, wd, fg, final):
    T = x2.shape[0]
    tm = POST_TM
    row = pl.BlockSpec((tm, D_MODEL), lambda i: (i, 0))
    wspec = _const_spec((D_MODEL, D_MODEL))
    vec = _const_spec((1, D_MODEL))
    return pl.pallas_call(
        functools.partial(_post_kernel, final=final),
        grid=(T // tm,),
        in_specs=[row, row, row, row, row, wspec, wspec, wspec,
                  vec, _const_spec((D_MODEL, D_FF)), _const_spec((D_FF, D_MODEL)), vec],
        out_specs=row,
        out_shape=jax.ShapeDtypeStruct((T, D_MODEL), F32),
        compiler_params=pltpu.CompilerParams(
            dimension_semantics=("parallel",), vmem_limit_bytes=VMEM_LIMIT_BYTES),
        name="post",
    )(x2, hm, ha, ga, gb, wm, wa, wo, g, wu, wd, fg)


def _t5_bucket(n):
    max_exact = N_BUCKETS // 2
    n = np.maximum(n, 0)
    large = max_exact + (np.log(np.maximum(n, 1) / max_exact)
                         / np.log(MAX_DISTANCE / max_exact)
                         * (N_BUCKETS - max_exact)).astype(np.int32)
    large = np.minimum(large, N_BUCKETS - 1)
    return np.where(n < max_exact, n, large).astype(np.int32)


def _bucket_table():
    W = WINDOW
    dist = np.arange(W)[:, None] + W - np.arange(2 * W)[None, :]
    valid = (dist >= 0) & (dist < W)
    return jnp.asarray(np.where(valid, _t5_bucket(dist), -1).astype(np.int32))


def _regroup_kernel(w_ref, out_ref):
    n_gate = 2 * M_HEADS
    gate0 = sum(IN_SIZES[:3])
    tail0 = gate0 + n_gate
    n_tail = sum(IN_SIZES[5:])
    out_ref[:, :gate0] = w_ref[:, :gate0].astype(BF16)
    tail = w_ref[:, tail0:tail0 + n_tail]
    out_ref[:, gate0:gate0 + A_Q_WIDTH] = (tail[:, :A_Q_WIDTH] * (A_HEAD_DIM ** -0.5)).astype(BF16)
    out_ref[:, gate0 + A_Q_WIDTH:gate0 + n_tail] = tail[:, A_Q_WIDTH:].astype(BF16)
    out_ref[:, gate0 + n_tail:] = jnp.zeros((out_ref.shape[0], 128), BF16)
    out_ref[:, gate0 + n_tail:gate0 + n_tail + n_gate] = w_ref[:, gate0:tail0].astype(BF16)


def _regroup_w_in(w_in):
    depth, d, n_in = w_in.shape
    tr = REGROUP_ROWS
    return pl.pallas_call(
        _regroup_kernel,
        grid=(depth, d // tr),
        in_specs=[pl.BlockSpec((None, tr, n_in), lambda l, i: (l, i, 0))],
        out_specs=pl.BlockSpec((None, tr, _INPROJ_WIDTH), lambda l, i: (l, i, 0)),
        out_shape=jax.ShapeDtypeStruct((depth, d, _INPROJ_WIDTH), BF16),
        compiler_params=pltpu.CompilerParams(
            dimension_semantics=("parallel", "parallel"), vmem_limit_bytes=VMEM_LIMIT_BYTES),
        name="regroup",
    )(w_in)


def kernel(x, norm_mix_g, w_in, conv_w, conv_b, b_igate, b_fgate, mlstm_norm_g, attn_sinks,
           rel_bias, w_branch_m, w_branch_a, w_out, norm_mlp_g, w_up, w_down, final_norm_g):
    B, S, D = x.shape
    T = B * S
    bucket = _bucket_table()
    w_all = _regroup_w_in(w_in)
    x2 = x.reshape(T, D)
    for l in range(DEPTH):
        mqk, mv, og, aq, ak, av, ga, gb, g_rows = _inproj(
            x2, norm_mix_g[l].reshape(1, D), conv_w[l], conv_b[l].reshape(1, -1),
            w_all[l], S)
        gate_b = jnp.concatenate([b_igate[l], b_fgate[l]]).reshape(2 * M_HEADS, 1)
        hm, ha = _mixers(mqk.reshape(B, S, -1), mv.reshape(B, S, -1), og.reshape(B, S, -1), g_rows,
                         gate_b, mlstm_norm_g[l].reshape(1, -1),
                         aq.reshape(B, S, -1), ak.reshape(B, S, -1), av.reshape(B, S, -1),
                         attn_sinks[l], rel_bias, bucket)
        x2 = _post(x2, hm.reshape(T, -1), ha.reshape(T, -1), ga, gb,
                   w_branch_m[l].astype(BF16), w_branch_a[l].astype(BF16), w_out[l].astype(BF16),
                   norm_mlp_g[l].reshape(1, D), w_up[l].astype(BF16), w_down[l].astype(BF16),
                   final_norm_g.reshape(1, D), final=(l == DEPTH - 1))
    return x2.reshape(B, S, D)
```

```python
import functools

import numpy as np
import jax
import jax.numpy as jnp
from jax import lax
from jax.experimental import pallas as pl
from jax.experimental.pallas import tpu as pltpu

D_MODEL = 1024
DEPTH = 2
M_HEADS = 4
M_QK_DIM = 128
M_V_DIM = 256
M_QK_WIDTH = M_HEADS * M_QK_DIM
M_V_WIDTH = M_HEADS * M_V_DIM
CONV_K = 4
A_Q_HEADS = 16
A_KV_HEADS = 4
A_HEAD_DIM = 64
A_GROUP = A_Q_HEADS // A_KV_HEADS
A_Q_WIDTH = A_Q_HEADS * A_HEAD_DIM
A_KV_WIDTH = A_KV_HEADS * A_HEAD_DIM
WINDOW = 128
N_BUCKETS = 32
MAX_DISTANCE = 128
D_FF = 4 * D_MODEL
EPS = 1e-6
IN_SIZES = (2 * M_QK_WIDTH, M_V_WIDTH, M_V_WIDTH, M_HEADS, M_HEADS,
            A_Q_WIDTH, A_KV_WIDTH, A_KV_WIDTH, D_MODEL, D_MODEL)

INPROJ_TM = 512
POST_TM = 512
PROJ_SUB = 256
REGROUP_COLS = 256
MLSTM_CHUNK = 128
MLSTM_BLOCKS = 2
CONV_HALO = 8
SWA_BLOCKS = 2
NEG_BIG = -1e30
VMEM_LIMIT_BYTES = 56 * 1024 * 1024

F32 = jnp.float32
BF16 = jnp.bfloat16


def _const_spec(shape):
    nd = len(shape)
    return pl.BlockSpec(shape, lambda *_: (0,) * nd, pipeline_mode=pl.Buffered(1))


def _rmsnorm(x, g):
    return x * lax.rsqrt(jnp.mean(x * x, axis=-1, keepdims=True) + EPS) * g


_INPROJ_GROUPS = (("qk", 2 * M_QK_WIDTH), ("v", M_V_WIDTH), ("o", M_V_WIDTH),
                  ("aq", A_Q_WIDTH), ("ak", A_KV_WIDTH), ("av", A_KV_WIDTH),
                  ("ga", D_MODEL), ("gb", D_MODEL), ("g", 128))
_INPROJ_OFFS = dict(zip((n for n, _ in _INPROJ_GROUPS),
                        np.cumsum([0] + [w for _, w in _INPROJ_GROUPS])[:-1].tolist()))
_INPROJ_WIDTH = sum(w for _, w in _INPROJ_GROUPS)


def _inproj_kernel(x_ref, g_ref, cw_ref, cb_ref, w_ref, *refs, tiles_per_seq):
    out_refs, halo_ref = refs[:-1], refs[-1]

    @pl.when(pl.program_id(0) % tiles_per_seq == 0)
    def _():
        halo_ref[...] = jnp.zeros_like(halo_ref)

    outs = dict(zip((n for n, _ in _INPROJ_GROUPS), out_refs))
    sub = PROJ_SUB
    for t in range(x_ref.shape[0] // sub):
        _inproj_rows(slice(t * sub, (t + 1) * sub), x_ref, g_ref, cw_ref, cb_ref, w_ref, outs, halo_ref)


def _inproj_rows(rows, x_ref, g_ref, cw_ref, cb_ref, w_ref, outs, halo_ref):
    n_rows = rows.stop - rows.start
    h = _rmsnorm(x_ref[rows, :], g_ref[...]).astype(BF16)
    res = lax.dot_general(h, w_ref[...], (((1,), (1,)), ((), ())),
                          preferred_element_type=F32)
    group = lambda name, width: res[:, _INPROJ_OFFS[name]:_INPROJ_OFFS[name] + width]

    head_row = lax.broadcasted_iota(jnp.int32, (CONV_HALO, 1), 0)
    gw = 2 * M_QK_WIDTH // 4
    for c in range(4):
        cols = slice(c * gw, (c + 1) * gw)
        r = res[:, cols]
        prev = halo_ref[:, cols]
        halo_ref[:, cols] = r[n_rows - CONV_HALO:, :]
        y = cb_ref[:, cols] + r * cw_ref[CONV_K - 1:CONV_K, cols]
        for s in range(1, CONV_K):
            rolled = pltpu.roll(r, s, axis=0)
            head = jnp.where(head_row < s, pltpu.roll(prev, s, axis=0), rolled[:CONV_HALO])
            shifted = jnp.concatenate([head, rolled[CONV_HALO:]], axis=0)
            y = y + shifted * cw_ref[CONV_K - 1 - s:CONV_K - s, cols]
        y = y * jax.nn.sigmoid(y)
        if c < 2:
            y = y * (M_QK_DIM ** -0.5)
        outs["qk"][rows, cols] = y.astype(BF16)

    for name, width in _INPROJ_GROUPS[1:-1]:
        val = group(name, width)
        if name in ("o", "ga", "gb"):
            val = jax.nn.sigmoid(val)
        if name in ("ak", "av"):
            half = A_HEAD_DIM
            swapped = [pltpu.roll(val[:, c:c + 2 * half], half, axis=1) for c in range(0, width, 2 * half)]
            val = jnp.concatenate([val] + swapped, axis=1)
        outs[name][rows, :] = val.astype(BF16)
    outs["g"][:, rows] = group("g", 128).T[:2 * M_HEADS, :]


def _inproj(x2, g, conv_w, conv_b, w_all, seq_len):
    T = x2.shape[0]
    tm = INPROJ_TM
    row = lambda n: pl.BlockSpec((tm, n), lambda i: (i, 0))
    out_widths = [2 * w if n in ("ak", "av") else w for n, w in _INPROJ_GROUPS[:-1]] + [2 * M_HEADS]
    return pl.pallas_call(
        functools.partial(_inproj_kernel, tiles_per_seq=seq_len // tm),
        grid=(T // tm,),
        in_specs=[row(D_MODEL), _const_spec((1, D_MODEL)), _const_spec((CONV_K, 2 * M_QK_WIDTH)),
                  _const_spec((1, 2 * M_QK_WIDTH)), _const_spec((_INPROJ_WIDTH, D_MODEL))],
        out_specs=[row(n) for n in out_widths[:-1]] + [pl.BlockSpec((2 * M_HEADS, tm), lambda i: (0, i))],
        out_shape=([jax.ShapeDtypeStruct((T, n), BF16) for n in out_widths[:-1]]
                   + [jax.ShapeDtypeStruct((2 * M_HEADS, T), F32)]),
        scratch_shapes=[pltpu.VMEM((CONV_HALO, 2 * M_QK_WIDTH), F32)],
        compiler_params=pltpu.CompilerParams(
            dimension_semantics=("arbitrary",), vmem_limit_bytes=VMEM_LIMIT_BYTES),
        name="inproj",
    )(x2, g, conv_w, conv_b, w_all)


def _mlstm_init(c_ref, n_ref, m_ref):
    @pl.when(pl.program_id(1) == 0)
    def _():
        c_ref[...] = jnp.zeros_like(c_ref)
        n_ref[...] = jnp.zeros_like(n_ref)
        m_ref[...] = jnp.zeros_like(m_ref)


def _mlstm_gates(blk, grow_ref, gb_ref):
    L = MLSTM_CHUNK
    rows = slice(blk * L, (blk + 1) * L)
    row_i = lax.broadcasted_iota(jnp.int32, (L, L), 0)
    col_j = lax.broadcasted_iota(jnp.int32, (L, L), 1)
    triu_f = (row_i <= col_j).astype(F32)
    g_rows = grow_ref[:, rows] + gb_ref[...]
    logf_rows = jax.nn.log_sigmoid(g_rows)
    b_rows = jnp.dot(logf_rows, triu_f, preferred_element_type=F32,
                     precision=lax.Precision.HIGHEST)
    return g_rows, logf_rows, b_rows


def _mlstm_head(blk, h, gates, qk_ref, v_ref, og_ref, ng_ref, out_ref, c_ref, n_ref, m_ref):
    L = MLSTM_CHUNK
    rows = slice(blk * L, (blk + 1) * L)
    row_i = lax.broadcasted_iota(jnp.int32, (L, L), 0)
    col_j = lax.broadcasted_iota(jnp.int32, (L, L), 1)
    tril = col_j <= row_i
    eye = col_j == row_i
    g_rows, logf_rows, b_rows = gates
    qb = qk_ref[rows, h * M_QK_DIM:(h + 1) * M_QK_DIM]
    kb = qk_ref[rows, M_QK_WIDTH + h * M_QK_DIM:M_QK_WIDTH + (h + 1) * M_QK_DIM]
    v = v_ref[rows, h * M_V_DIM:(h + 1) * M_V_DIM]
    q = qb.astype(F32)
    k = kb.astype(F32)

    m_prev = m_ref[h][:, 0:1]
    n_prev = n_ref[h]
    c_prev = c_ref[h]

    logf_row = logf_rows[M_HEADS + h:M_HEADS + h + 1, :]
    b_row = b_rows[M_HEADS + h:M_HEADS + h + 1, :]
    a_row = g_rows[h:h + 1, :] - b_row
    b_col = jnp.sum(jnp.where(tril, logf_row, 0.0), axis=-1, keepdims=True)
    a_col = jnp.sum(jnp.where(eye, a_row, 0.0), axis=-1, keepdims=True)
    b_last = jnp.sum(logf_row, axis=-1, keepdims=True)

    a_mat = jnp.where(tril, a_row, -jnp.inf)
    m_row = jnp.maximum(m_prev, jnp.max(a_mat, axis=-1, keepdims=True))
    w_intra = jnp.exp(a_mat - m_row)
    w_inter = jnp.exp(m_prev - m_row)

    s = lax.dot_general(qb, kb, (((1,), (1,)), ((), ())),
                        preferred_element_type=F32) * w_intra
    num = (jnp.dot(s.astype(BF16), v, preferred_element_type=F32)
           + w_inter * jnp.dot(qb, c_prev.astype(BF16), preferred_element_type=F32))
    den = (jnp.sum(s, axis=-1, keepdims=True)
           + w_inter * jnp.sum(q * n_prev, axis=-1, keepdims=True))
    hh = num / jnp.maximum(jnp.abs(den), jnp.exp(-(b_col + m_row)))

    m_last = jnp.maximum(m_prev, jnp.max(a_row, axis=-1, keepdims=True))
    decay = jnp.exp(m_prev - m_last)
    kw = k * jnp.exp(a_col - m_last)
    c_ref[h] = decay * c_prev + lax.dot_general(
        kw.astype(BF16), v, (((0,), (0,)), ((), ())), preferred_element_type=F32)
    n_ref[h] = decay * n_prev + jnp.sum(kw, axis=0, keepdims=True)
    m_ref[h] = jnp.broadcast_to(b_last + m_last, (1, 128))

    sl = slice(h * M_V_DIM, (h + 1) * M_V_DIM)
    hn = hh * lax.rsqrt(jnp.mean(hh * hh, axis=-1, keepdims=True) + EPS) * ng_ref[:, sl]
    out_ref[rows, sl] = hn.astype(BF16) * og_ref[rows, sl]


def _swa_init(rel_ref, bucket_ref, bias_ref):
    W = WINDOW

    @pl.when((pl.program_id(0) == 0) & (pl.program_id(1) == 0))
    def _():
        bucket = bucket_ref[...]
        neg = jnp.full((W, 2 * W), NEG_BIG, F32)
        for hq in range(A_Q_HEADS):
            bias_ref[1, hq] = neg

        def fill(bb, carry):
            hit = bucket == bb
            for hq in range(A_Q_HEADS):
                bias_ref[1, hq] = jnp.where(hit, rel_ref[bb, hq], bias_ref[1, hq])
            return carry
        lax.fori_loop(0, N_BUCKETS, fill, 0)
        is_prev = lax.broadcasted_iota(jnp.int32, (W, 2 * W), 1) < W
        for hq in range(A_Q_HEADS):
            bias_ref[0, hq] = jnp.where(is_prev, NEG_BIG, bias_ref[1, hq])


def _swa_kv_head(blk, kv, sink_ref, q_ref, kp_ref, kc_ref, vp_ref, vc_ref, out_ref, bias_ref):
    W = WINDOW
    lo = lax.broadcasted_iota(jnp.int32, (W, 2 * A_HEAD_DIM), 1) < A_HEAD_DIM
    lo2 = lax.broadcasted_iota(jnp.int32, (2 * W, 2 * A_HEAD_DIM), 1) < A_HEAD_DIM
    zero = jnp.zeros((), BF16)
    ones_lo = jnp.where(lo2, 1.0, 0.0).astype(BF16)
    ones_hi = jnp.where(lo2, 0.0, 1.0).astype(BF16)
    nt = (((1,), (1,)), ((), ()))
    rows = slice(blk * W, (blk + 1) * W)
    var = jnp.minimum(pl.program_id(1), 1) if blk == 0 else 1
    def tile(ref, rws, swapped):
        c0 = (A_KV_WIDTH if swapped else 0) + (kv // 2) * 2 * A_HEAD_DIM
        return ref[rws, c0:c0 + 2 * A_HEAD_DIM]

    def prev_cur(p_ref, c_ref, swapped):
        prev = (tile(p_ref, slice(None), swapped) if blk == 0
                else tile(c_ref, slice((blk - 1) * W, blk * W), swapped))
        return jnp.concatenate([prev, tile(c_ref, rows, swapped)], axis=0)

    in_low = kv % 2 == 1
    k_even = jnp.where(lo2, prev_cur(kp_ref, kc_ref, in_low), zero)
    k_odd = jnp.where(lo2, zero, prev_cur(kp_ref, kc_ref, not in_low))
    v_even = jnp.concatenate([jnp.where(lo2, prev_cur(vp_ref, vc_ref, in_low), zero), ones_lo], axis=1)
    v_odd = jnp.concatenate([jnp.where(lo2, zero, prev_cur(vp_ref, vc_ref, not in_low)), ones_hi], axis=1)
    for pair in range(A_GROUP // 2):
        pi = kv * (A_GROUP // 2) + pair
        cols = slice(pi * 2 * A_HEAD_DIM, (pi + 1) * 2 * A_HEAD_DIM)
        q2 = q_ref[rows, cols]
        sink_e = sink_ref[2 * pi]
        sink_o = sink_ref[2 * pi + 1]
        s_e = lax.dot_general(q2, k_even, nt, preferred_element_type=F32) + bias_ref[var, 2 * pi]
        s_o = lax.dot_general(q2, k_odd, nt, preferred_element_type=F32) + bias_ref[var, 2 * pi + 1]
        m_e = jnp.maximum(jnp.max(s_e, axis=-1, keepdims=True), sink_e)
        m_o = jnp.maximum(jnp.max(s_o, axis=-1, keepdims=True), sink_o)
        p_e = jnp.exp(s_e - m_e).astype(BF16)
        p_o = jnp.exp(s_o - m_o).astype(BF16)
        acc = (jnp.dot(p_e, v_even, preferred_element_type=F32)
               + jnp.dot(p_o, v_odd, preferred_element_type=F32))
        sink_term = jnp.where(lo, jnp.exp(sink_e - m_e), jnp.exp(sink_o - m_o))
        out_ref[rows, cols] = (acc[:, :2 * A_HEAD_DIM]
                               / (acc[:, 2 * A_HEAD_DIM:] + sink_term)).astype(out_ref.dtype)


def _mixers_kernel(sink_ref, rel_ref, bucket_ref, qk_ref, v_ref, og_ref, grow_ref, gb_ref, ng_ref,
                   q_ref, kp_ref, kc_ref, vp_ref, vc_ref,
                   hm_ref, ha_ref, c_ref, n_ref, m_ref, bias_ref):
    _mlstm_init(c_ref, n_ref, m_ref)
    _swa_init(rel_ref, bucket_ref, bias_ref)
    assert A_KV_HEADS == M_HEADS
    for blk in range(SWA_BLOCKS):
        for u in range(A_KV_HEADS):
            _swa_kv_head(blk, u, sink_ref, q_ref, kp_ref, kc_ref, vp_ref, vc_ref, ha_ref, bias_ref)
        gates = _mlstm_gates(blk, grow_ref, gb_ref)
        for u in range(M_HEADS):
            _mlstm_head(blk, u, gates, qk_ref, v_ref, og_ref, ng_ref, hm_ref, c_ref, n_ref, m_ref)


def _mixers(mqk, mv, og, g_rows, gate_b, norm_g, aq, ak2, av2, sinks, rel_bias, bucket):
    B, S, _ = aq.shape
    W = WINDOW
    tq = SWA_BLOCKS * W
    assert tq == MLSTM_BLOCKS * MLSTM_CHUNK
    kvw = 2 * A_KV_WIDTH
    cur = lambda n: pl.BlockSpec((None, tq, n), lambda b, i: (b, i, 0))
    prev = pl.BlockSpec((None, W, kvw), lambda b, i: (b, jnp.maximum(i * SWA_BLOCKS - 1, 0), 0))
    smem = pl.BlockSpec(memory_space=pltpu.SMEM)
    return pl.pallas_call(
        _mixers_kernel,
        grid=(B, S // tq),
        in_specs=[smem, smem, _const_spec((W, 2 * W)),
                  cur(2 * M_QK_WIDTH), cur(M_V_WIDTH), cur(M_V_WIDTH),
                  pl.BlockSpec((2 * M_HEADS, tq), lambda b, i: (0, b * (S // tq) + i)),
                  _const_spec((2 * M_HEADS, 1)), _const_spec((1, M_V_WIDTH)),
                  cur(A_Q_WIDTH), prev, cur(kvw), prev, cur(kvw)],
        out_specs=[cur(M_V_WIDTH), cur(A_Q_WIDTH)],
        out_shape=[jax.ShapeDtypeStruct((B, S, M_V_WIDTH), BF16),
                   jax.ShapeDtypeStruct((B, S, A_Q_WIDTH), BF16)],
        scratch_shapes=[pltpu.VMEM((M_HEADS, M_QK_DIM, M_V_DIM), F32),
                        pltpu.VMEM((M_HEADS, 1, M_QK_DIM), F32),
                        pltpu.VMEM((M_HEADS, 1, 128), F32),
                        pltpu.VMEM((2, A_Q_HEADS, W, 2 * W), F32)],
        compiler_params=pltpu.CompilerParams(
            dimension_semantics=("arbitrary", "arbitrary"), vmem_limit_bytes=VMEM_LIMIT_BYTES),
        name="mixers",
    )(sinks, rel_bias, bucket, mqk, mv, og, g_rows, gate_b, norm_g, aq, ak2, ak2, av2, av2)


def _post_kernel(x_ref, hm_ref, ha_ref, ga_ref, gb_ref, wm_ref, wa_ref, wo_ref,
                 g_ref, wu_ref, wd_ref, fg_ref, out_ref, *, final):
    sub = PROJ_SUB
    for t in range(x_ref.shape[0] // sub):
        rows = slice(t * sub, (t + 1) * sub)
        y = (ga_ref[rows, :] * jnp.dot(hm_ref[rows, :], wm_ref[...], preferred_element_type=F32)
             + gb_ref[rows, :] * jnp.dot(ha_ref[rows, :], wa_ref[...], preferred_element_type=F32))
        x = x_ref[rows, :] + jnp.dot(y.astype(BF16), wo_ref[...], preferred_element_type=F32)
        h = _rmsnorm(x, g_ref[...]).astype(BF16)
        u = jnp.maximum(jnp.dot(h, wu_ref[...], preferred_element_type=F32), 0.0)
        x = x + jnp.dot((u * u).astype(BF16), wd_ref[...], preferred_element_type=F32)
        if final:
            x = _rmsnorm(x, fg_ref[...])
        out_ref[rows, :] = x


def _post(x2, hm, ha, ga, gb, wm, wa, wo, g, wu, wd, fg, final):
    T = x2.shape[0]
    tm = POST_TM
    row = pl.BlockSpec((tm, D_MODEL), lambda i: (i, 0))
    wspec = _const_spec((D_MODEL, D_MODEL))
    vec = _const_spec((1, D_MODEL))
    return pl.pallas_call(
        functools.partial(_post_kernel, final=final),
        grid=(T // tm,),
        in_specs=[row, row, row, row, row, wspec, wspec, wspec,
                  vec, _const_spec((D_MODEL, D_FF)), _const_spec((D_FF, D_MODEL)), vec],
        out_specs=row,
        out_shape=jax.ShapeDtypeStruct((T, D_MODEL), F32),
        compiler_params=pltpu.CompilerParams(
            dimension_semantics=("parallel",), vmem_limit_bytes=VMEM_LIMIT_BYTES),
        name="post",
    )(x2, hm, ha, ga, gb, wm, wa, wo, g, wu, wd, fg)


def _t5_bucket(n):
    max_exact = N_BUCKETS // 2
    n = np.maximum(n, 0)
    large = max_exact + (np.log(np.maximum(n, 1) / max_exact)
                         / np.log(MAX_DISTANCE / max_exact)
                         * (N_BUCKETS - max_exact)).astype(np.int32)
    large = np.minimum(large, N_BUCKETS - 1)
    return np.where(n < max_exact, n, large).astype(np.int32)


def _bucket_table():
    W = WINDOW
    dist = np.arange(W)[:, None] + W - np.arange(2 * W)[None, :]
    valid = (dist >= 0) & (dist < W)
    return jnp.asarray(np.where(valid, _t5_bucket(dist), -1).astype(np.int32))


def _regroup_kernel(w_ref, out_ref):
    n_gate = 2 * M_HEADS
    gate0 = sum(IN_SIZES[:3])
    tail0 = gate0 + n_gate
    n_tail = sum(IN_SIZES[5:])
    out_ref[:gate0, :] = w_ref[:gate0, :].astype(BF16)
    out_ref[gate0:gate0 + A_Q_WIDTH, :] = (
        w_ref[tail0:tail0 + A_Q_WIDTH, :] * (A_HEAD_DIM ** -0.5)).astype(BF16)
    out_ref[gate0 + A_Q_WIDTH:gate0 + n_tail, :] = w_ref[tail0 + A_Q_WIDTH:tail0 + n_tail, :].astype(BF16)
    gate_rows = jnp.concatenate(
        [w_ref[gate0:tail0, :], jnp.zeros((128 - n_gate, w_ref.shape[1]), F32)], axis=0)
    out_ref[gate0 + n_tail:, :] = gate_rows.astype(BF16)


def _regroup_w_in(w_in):
    w_t = jnp.swapaxes(w_in, 1, 2)
    depth, n_in, d = w_t.shape
    tc = REGROUP_COLS
    return pl.pallas_call(
        _regroup_kernel,
        grid=(depth, d // tc),
        in_specs=[pl.BlockSpec((None, n_in, tc), lambda l, i: (l, 0, i))],
        out_specs=pl.BlockSpec((None, _INPROJ_WIDTH, tc), lambda l, i: (l, 0, i)),
        out_shape=jax.ShapeDtypeStruct((depth, _INPROJ_WIDTH, d), BF16),
        compiler_params=pltpu.CompilerParams(
            dimension_semantics=("parallel", "parallel"), vmem_limit_bytes=VMEM_LIMIT_BYTES),
        name="regroup",
    )(w_t)


def kernel(x, norm_mix_g, w_in, conv_w, conv_b, b_igate, b_fgate, mlstm_norm_g, attn_sinks,
           rel_bias, w_branch_m, w_branch_a, w_out, norm_mlp_g, w_up, w_down, final_norm_g):
    B, S, D = x.shape
    T = B * S
    bucket = _bucket_table()
    w_all = _regroup_w_in(w_in)
    x2 = x.reshape(T, D)
    for l in range(DEPTH):
        mqk, mv, og, aq, ak, av, ga, gb, g_rows = _inproj(
            x2, norm_mix_g[l].reshape(1, D), conv_w[l], conv_b[l].reshape(1, -1),
            w_all[l], S)
        gate_b = jnp.concatenate([b_igate[l], b_fgate[l]]).reshape(2 * M_HEADS, 1)
        hm, ha = _mixers(mqk.reshape(B, S, -1), mv.reshape(B, S, -1), og.reshape(B, S, -1), g_rows,
                         gate_b, mlstm_norm_g[l].reshape(1, -1),
                         aq.reshape(B, S, -1), ak.reshape(B, S, -1), av.reshape(B, S, -1),
                         attn_sinks[l], rel_bias, bucket)
        x2 = _post(x2, hm.reshape(T, -1), ha.reshape(T, -1), ga, gb,
                   w_branch_m[l].astype(BF16), w_branch_a[l].astype(BF16), w_out[l].astype(BF16),
                   norm_mlp_g[l].reshape(1, D), w_up[l].astype(BF16), w_down[l].astype(BF16),
                   final_norm_g.reshape(1, D), final=(l == DEPTH - 1))
    return x2.reshape(B, S, D)
```

```python
import functools

import numpy as np
import jax
import jax.numpy as jnp
from jax import lax
from jax.experimental import pallas as pl
from jax.experimental.pallas import tpu as pltpu

D_MODEL = 1024
DEPTH = 2
M_HEADS = 4
M_QK_DIM = 128
M_V_DIM = 256
M_QK_WIDTH = M_HEADS * M_QK_DIM
M_V_WIDTH = M_HEADS * M_V_DIM
CONV_K = 4
A_Q_HEADS = 16
A_KV_HEADS = 4
A_HEAD_DIM = 64
A_GROUP = A_Q_HEADS // A_KV_HEADS
A_Q_WIDTH = A_Q_HEADS * A_HEAD_DIM
A_KV_WIDTH = A_KV_HEADS * A_HEAD_DIM
WINDOW = 128
N_BUCKETS = 32
MAX_DISTANCE = 128
D_FF = 4 * D_MODEL
EPS = 1e-6
IN_SIZES = (2 * M_QK_WIDTH, M_V_WIDTH, M_V_WIDTH, M_HEADS, M_HEADS,
            A_Q_WIDTH, A_KV_WIDTH, A_KV_WIDTH, D_MODEL, D_MODEL)

INPROJ_TM = 512
POST_TM = 512
PROJ_SUB = 256
REGROUP_COLS = 256
REGROUP_PIECE = 512
MLSTM_CHUNK = 128
MLSTM_BLOCKS = 2
CONV_HALO = 8
SWA_BLOCKS = 2
NEG_BIG = -1e30
VMEM_LIMIT_BYTES = 56 * 1024 * 1024

F32 = jnp.float32
BF16 = jnp.bfloat16


def _const_spec(shape):
    nd = len(shape)
    return pl.BlockSpec(shape, lambda *_: (0,) * nd, pipeline_mode=pl.Buffered(1))


def _rmsnorm(x, g):
    return x * lax.rsqrt(jnp.mean(x * x, axis=-1, keepdims=True) + EPS) * g


_INPROJ_GROUPS = (("qk", 2 * M_QK_WIDTH), ("v", M_V_WIDTH), ("o", M_V_WIDTH),
                  ("aq", A_Q_WIDTH), ("ak", A_KV_WIDTH), ("av", A_KV_WIDTH),
                  ("ga", D_MODEL), ("gb", D_MODEL), ("g", 128))
_INPROJ_OFFS = dict(zip((n for n, _ in _INPROJ_GROUPS),
                        np.cumsum([0] + [w for _, w in _INPROJ_GROUPS])[:-1].tolist()))
_INPROJ_WIDTH = sum(w for _, w in _INPROJ_GROUPS)


def _inproj_kernel(x_ref, g_ref, cw_ref, cb_ref, w_ref, *refs, tiles_per_seq):
    out_refs, halo_ref = refs[:-1], refs[-1]

    @pl.when(pl.program_id(0) % tiles_per_seq == 0)
    def _():
        halo_ref[...] = jnp.zeros_like(halo_ref)

    outs = dict(zip((n for n, _ in _INPROJ_GROUPS), out_refs))
    sub = PROJ_SUB
    for t in range(x_ref.shape[0] // sub):
        _inproj_rows(slice(t * sub, (t + 1) * sub), x_ref, g_ref, cw_ref, cb_ref, w_ref, outs, halo_ref)


def _inproj_rows(rows, x_ref, g_ref, cw_ref, cb_ref, w_ref, outs, halo_ref):
    n_rows = rows.stop - rows.start
    h = _rmsnorm(x_ref[rows, :], g_ref[...]).astype(BF16)
    res = jnp.dot(h, w_ref[...], preferred_element_type=F32)
    group = lambda name, width: res[:, _INPROJ_OFFS[name]:_INPROJ_OFFS[name] + width]

    head_row = lax.broadcasted_iota(jnp.int32, (CONV_HALO, 1), 0)
    gw = 2 * M_QK_WIDTH // 4
    for c in range(4):
        cols = slice(c * gw, (c + 1) * gw)
        r = res[:, cols]
        prev = halo_ref[:, cols]
        halo_ref[:, cols] = r[n_rows - CONV_HALO:, :]
        y = cb_ref[:, cols] + r * cw_ref[CONV_K - 1:CONV_K, cols]
        for s in range(1, CONV_K):
            rolled = pltpu.roll(r, s, axis=0)
            head = jnp.where(head_row < s, pltpu.roll(prev, s, axis=0), rolled[:CONV_HALO])
            shifted = jnp.concatenate([head, rolled[CONV_HALO:]], axis=0)
            y = y + shifted * cw_ref[CONV_K - 1 - s:CONV_K - s, cols]
        y = y * jax.nn.sigmoid(y)
        if c < 2:
            y = y * (M_QK_DIM ** -0.5)
        outs["qk"][rows, cols] = y.astype(BF16)

    for name, width in _INPROJ_GROUPS[1:-1]:
        val = group(name, width)
        if name in ("o", "ga", "gb"):
            val = jax.nn.sigmoid(val)
        if name in ("ak", "av"):
            half = A_HEAD_DIM
            swapped = [pltpu.roll(val[:, c:c + 2 * half], half, axis=1) for c in range(0, width, 2 * half)]
            val = jnp.concatenate([val] + swapped, axis=1)
        outs[name][rows, :] = val.astype(BF16)
    outs["g"][:, rows] = group("g", 128).T[:2 * M_HEADS, :]


def _inproj(x2, g, conv_w, conv_b, w_all, seq_len):
    T = x2.shape[0]
    tm = INPROJ_TM
    row = lambda n: pl.BlockSpec((tm, n), lambda i: (i, 0))
    out_widths = [2 * w if n in ("ak", "av") else w for n, w in _INPROJ_GROUPS[:-1]] + [2 * M_HEADS]
    return pl.pallas_call(
        functools.partial(_inproj_kernel, tiles_per_seq=seq_len // tm),
        grid=(T // tm,),
        in_specs=[row(D_MODEL), _const_spec((1, D_MODEL)), _const_spec((CONV_K, 2 * M_QK_WIDTH)),
                  _const_spec((1, 2 * M_QK_WIDTH)), _const_spec((D_MODEL, _INPROJ_WIDTH))],
        out_specs=[row(n) for n in out_widths[:-1]] + [pl.BlockSpec((2 * M_HEADS, tm), lambda i: (0, i))],
        out_shape=([jax.ShapeDtypeStruct((T, n), BF16) for n in out_widths[:-1]]
                   + [jax.ShapeDtypeStruct((2 * M_HEADS, T), F32)]),
        scratch_shapes=[pltpu.VMEM((CONV_HALO, 2 * M_QK_WIDTH), F32)],
        compiler_params=pltpu.CompilerParams(
            dimension_semantics=("arbitrary",), vmem_limit_bytes=VMEM_LIMIT_BYTES),
        name="inproj",
    )(x2, g, conv_w, conv_b, w_all)


def _mlstm_init(c_ref, n_ref, m_ref):
    @pl.when(pl.program_id(1) == 0)
    def _():
        c_ref[...] = jnp.zeros_like(c_ref)
        n_ref[...] = jnp.zeros_like(n_ref)
        m_ref[...] = jnp.zeros_like(m_ref)


def _mlstm_gates(blk, grow_ref, gb_ref):
    L = MLSTM_CHUNK
    rows = slice(blk * L, (blk + 1) * L)
    row_i = lax.broadcasted_iota(jnp.int32, (L, L), 0)
    col_j = lax.broadcasted_iota(jnp.int32, (L, L), 1)
    triu_f = (row_i <= col_j).astype(F32)
    g_rows = grow_ref[:, rows] + gb_ref[...]
    logf_rows = jax.nn.log_sigmoid(g_rows)
    b_rows = jnp.dot(logf_rows, triu_f, preferred_element_type=F32,
                     precision=lax.Precision.HIGHEST)
    return g_rows, logf_rows, b_rows


def _mlstm_head(blk, h, gates, qk_ref, v_ref, og_ref, ng_ref, out_ref, c_ref, n_ref, m_ref):
    L = MLSTM_CHUNK
    rows = slice(blk * L, (blk + 1) * L)
    row_i = lax.broadcasted_iota(jnp.int32, (L, L), 0)
    col_j = lax.broadcasted_iota(jnp.int32, (L, L), 1)
    tril = col_j <= row_i
    eye = col_j == row_i
    g_rows, logf_rows, b_rows = gates
    qb = qk_ref[rows, h * M_QK_DIM:(h + 1) * M_QK_DIM]
    kb = qk_ref[rows, M_QK_WIDTH + h * M_QK_DIM:M_QK_WIDTH + (h + 1) * M_QK_DIM]
    v = v_ref[rows, h * M_V_DIM:(h + 1) * M_V_DIM]
    q = qb.astype(F32)
    k = kb.astype(F32)

    m_prev = m_ref[h][:, 0:1]
    n_prev = n_ref[h]
    c_prev = c_ref[h]

    logf_row = logf_rows[M_HEADS + h:M_HEADS + h + 1, :]
    b_row = b_rows[M_HEADS + h:M_HEADS + h + 1, :]
    a_row = g_rows[h:h + 1, :] - b_row
    b_col = jnp.sum(jnp.where(tril, logf_row, 0.0), axis=-1, keepdims=True)
    a_col = jnp.sum(jnp.where(eye, a_row, 0.0), axis=-1, keepdims=True)
    b_last = jnp.sum(logf_row, axis=-1, keepdims=True)

    a_mat = jnp.where(tril, a_row, -jnp.inf)
    m_row = jnp.maximum(m_prev, jnp.max(a_mat, axis=-1, keepdims=True))
    w_intra = jnp.exp(a_mat - m_row)
    w_inter = jnp.exp(m_prev - m_row)

    s = lax.dot_general(qb, kb, (((1,), (1,)), ((), ())),
                        preferred_element_type=F32) * w_intra
    num = (jnp.dot(s.astype(BF16), v, preferred_element_type=F32)
           + w_inter * jnp.dot(qb, c_prev.astype(BF16), preferred_element_type=F32))
    den = (jnp.sum(s, axis=-1, keepdims=True)
           + w_inter * jnp.sum(q * n_prev, axis=-1, keepdims=True))
    hh = num / jnp.maximum(jnp.abs(den), jnp.exp(-(b_col + m_row)))

    m_last = jnp.maximum(m_prev, jnp.max(a_row, axis=-1, keepdims=True))
    decay = jnp.exp(m_prev - m_last)
    kw = k * jnp.exp(a_col - m_last)
    c_ref[h] = decay * c_prev + lax.dot_general(
        kw.astype(BF16), v, (((0,), (0,)), ((), ())), preferred_element_type=F32)
    n_ref[h] = decay * n_prev + jnp.sum(kw, axis=0, keepdims=True)
    m_ref[h] = jnp.broadcast_to(b_last + m_last, (1, 128))

    sl = slice(h * M_V_DIM, (h + 1) * M_V_DIM)
    hn = hh * lax.rsqrt(jnp.mean(hh * hh, axis=-1, keepdims=True) + EPS) * ng_ref[:, sl]
    out_ref[rows, sl] = hn.astype(BF16) * og_ref[rows, sl]


def _swa_init(rel_ref, bucket_ref, bias_ref):
    W = WINDOW

    @pl.when((pl.program_id(0) == 0) & (pl.program_id(1) == 0))
    def _():
        bucket = bucket_ref[...]
        neg = jnp.full((W, 2 * W), NEG_BIG, F32)
        for hq in range(A_Q_HEADS):
            bias_ref[1, hq] = neg

        def fill(bb, carry):
            hit = bucket == bb
            for hq in range(A_Q_HEADS):
                bias_ref[1, hq] = jnp.where(hit, rel_ref[bb, hq], bias_ref[1, hq])
            return carry
        lax.fori_loop(0, N_BUCKETS, fill, 0)
        is_prev = lax.broadcasted_iota(jnp.int32, (W, 2 * W), 1) < W
        for hq in range(A_Q_HEADS):
            bias_ref[0, hq] = jnp.where(is_prev, NEG_BIG, bias_ref[1, hq])


def _swa_kv_head(blk, kv, sink_ref, q_ref, kp_ref, kc_ref, vp_ref, vc_ref, out_ref, bias_ref):
    W = WINDOW
    lo = lax.broadcasted_iota(jnp.int32, (W, 2 * A_HEAD_DIM), 1) < A_HEAD_DIM
    lo2 = lax.broadcasted_iota(jnp.int32, (2 * W, 2 * A_HEAD_DIM), 1) < A_HEAD_DIM
    zero = jnp.zeros((), BF16)
    ones_lo = jnp.where(lo2, 1.0, 0.0).astype(BF16)
    ones_hi = jnp.where(lo2, 0.0, 1.0).astype(BF16)
    nt = (((1,), (1,)), ((), ()))
    rows = slice(blk * W, (blk + 1) * W)
    var = jnp.minimum(pl.program_id(1), 1) if blk == 0 else 1
    def tile(ref, rws, swapped):
        c0 = (A_KV_WIDTH if swapped else 0) + (kv // 2) * 2 * A_HEAD_DIM
        return ref[rws, c0:c0 + 2 * A_HEAD_DIM]

    def prev_cur(p_ref, c_ref, swapped):
        prev = (tile(p_ref, slice(None), swapped) if blk == 0
                else tile(c_ref, slice((blk - 1) * W, blk * W), swapped))
        return jnp.concatenate([prev, tile(c_ref, rows, swapped)], axis=0)

    in_low = kv % 2 == 1
    k_even = jnp.where(lo2, prev_cur(kp_ref, kc_ref, in_low), zero)
    k_odd = jnp.where(lo2, zero, prev_cur(kp_ref, kc_ref, not in_low))
    v_even = jnp.concatenate([jnp.where(lo2, prev_cur(vp_ref, vc_ref, in_low), zero), ones_lo], axis=1)
    v_odd = jnp.concatenate([jnp.where(lo2, zero, prev_cur(vp_ref, vc_ref, not in_low)), ones_hi], axis=1)
    for pair in range(A_GROUP // 2):
        pi = kv * (A_GROUP // 2) + pair
        cols = slice(pi * 2 * A_HEAD_DIM, (pi + 1) * 2 * A_HEAD_DIM)
        q2 = q_ref[rows, cols]
        sink_e = sink_ref[2 * pi]
        sink_o = sink_ref[2 * pi + 1]
        s_e = lax.dot_general(q2, k_even, nt, preferred_element_type=F32) + bias_ref[var, 2 * pi]
        s_o = lax.dot_general(q2, k_odd, nt, preferred_element_type=F32) + bias_ref[var, 2 * pi + 1]
        m_e = jnp.maximum(jnp.max(s_e, axis=-1, keepdims=True), sink_e)
        m_o = jnp.maximum(jnp.max(s_o, axis=-1, keepdims=True), sink_o)
        p_e = jnp.exp(s_e - m_e).astype(BF16)
        p_o = jnp.exp(s_o - m_o).astype(BF16)
        acc = (jnp.dot(p_e, v_even, preferred_element_type=F32)
               + jnp.dot(p_o, v_odd, preferred_element_type=F32))
        sink_term = jnp.where(lo, jnp.exp(sink_e - m_e), jnp.exp(sink_o - m_o))
        out_ref[rows, cols] = (acc[:, :2 * A_HEAD_DIM]
                               / (acc[:, 2 * A_HEAD_DIM:] + sink_term)).astype(out_ref.dtype)


def _mixers_kernel(sink_ref, rel_ref, bucket_ref, qk_ref, v_ref, og_ref, grow_ref, gb_ref, ng_ref,
                   q_ref, kp_ref, kc_ref, vp_ref, vc_ref,
                   hm_ref, ha_ref, c_ref, n_ref, m_ref, bias_ref):
    _mlstm_init(c_ref, n_ref, m_ref)
    _swa_init(rel_ref, bucket_ref, bias_ref)
    assert A_KV_HEADS == M_HEADS
    for blk in range(SWA_BLOCKS):
        for u in range(A_KV_HEADS):
            _swa_kv_head(blk, u, sink_ref, q_ref, kp_ref, kc_ref, vp_ref, vc_ref, ha_ref, bias_ref)
        gates = _mlstm_gates(blk, grow_ref, gb_ref)
        for u in range(M_HEADS):
            _mlstm_head(blk, u, gates, qk_ref, v_ref, og_ref, ng_ref, hm_ref, c_ref, n_ref, m_ref)


def _mixers(mqk, mv, og, g_rows, gate_b, norm_g, aq, ak2, av2, sinks, rel_bias, bucket):
    B, S, _ = aq.shape
    W = WINDOW
    tq = SWA_BLOCKS * W
    assert tq == MLSTM_BLOCKS * MLSTM_CHUNK
    kvw = 2 * A_KV_WIDTH
    cur = lambda n: pl.BlockSpec((None, tq, n), lambda b, i: (b, i, 0))
    prev = pl.BlockSpec((None, W, kvw), lambda b, i: (b, jnp.maximum(i * SWA_BLOCKS - 1, 0), 0))
    smem = pl.BlockSpec(memory_space=pltpu.SMEM)
    return pl.pallas_call(
        _mixers_kernel,
        grid=(B, S // tq),
        in_specs=[smem, smem, _const_spec((W, 2 * W)),
                  cur(2 * M_QK_WIDTH), cur(M_V_WIDTH), cur(M_V_WIDTH),
                  pl.BlockSpec((2 * M_HEADS, tq), lambda b, i: (0, b * (S // tq) + i)),
                  _const_spec((2 * M_HEADS, 1)), _const_spec((1, M_V_WIDTH)),
                  cur(A_Q_WIDTH), prev, cur(kvw), prev, cur(kvw)],
        out_specs=[cur(M_V_WIDTH), cur(A_Q_WIDTH)],
        out_shape=[jax.ShapeDtypeStruct((B, S, M_V_WIDTH), BF16),
                   jax.ShapeDtypeStruct((B, S, A_Q_WIDTH), BF16)],
        scratch_shapes=[pltpu.VMEM((M_HEADS, M_QK_DIM, M_V_DIM), F32),
                        pltpu.VMEM((M_HEADS, 1, M_QK_DIM), F32),
                        pltpu.VMEM((M_HEADS, 1, 128), F32),
                        pltpu.VMEM((2, A_Q_HEADS, W, 2 * W), F32)],
        compiler_params=pltpu.CompilerParams(
            dimension_semantics=("arbitrary", "arbitrary"), vmem_limit_bytes=VMEM_LIMIT_BYTES),
        name="mixers",
    )(sinks, rel_bias, bucket, mqk, mv, og, g_rows, gate_b, norm_g, aq, ak2, ak2, av2, av2)


def _post_kernel(x_ref, hm_ref, ha_ref, ga_ref, gb_ref, wm_ref, wa_ref, wo_ref,
                 g_ref, wu_ref, wd_ref, fg_ref, out_ref, *, final):
    sub = PROJ_SUB
    for t in range(x_ref.shape[0] // sub):
        rows = slice(t * sub, (t + 1) * sub)
        y = (ga_ref[rows, :] * jnp.dot(hm_ref[rows, :], wm_ref[...], preferred_element_type=F32)
             + gb_ref[rows, :] * jnp.dot(ha_ref[rows, :], wa_ref[...], preferred_element_type=F32))
        x = x_ref[rows, :] + jnp.dot(y.astype(BF16), wo_ref[...], preferred_element_type=F32)
        h = _rmsnorm(x, g_ref[...]).astype(BF16)
        u = jnp.maximum(jnp.dot(h, wu_ref[...], preferred_element_type=F32), 0.0)
        x = x + jnp.dot((u * u).astype(BF16), wd_ref[...], preferred_element_type=F32)
        if final:
            x = _rmsnorm(x, fg_ref[...])
        out_ref[rows, :] = x


def _post(x2, hm, ha, ga, gb, wm, wa, wo, g, wu, wd, fg, final):
    T = x2.shape[0]
    tm = POST_TM
    row = pl.BlockSpec((tm, D_MODEL), lambda i: (i, 0))
    wspec = _const_spec((D_MODEL, D_MODEL))
    vec = _const_spec((1, D_MODEL))
    return pl.pallas_call(
        functools.partial(_post_kernel, final=final),
        grid=(T // tm,),
        in_specs=[row, row, row, row, row, wspec, wspec, wspec,
                  vec, _const_spec((D_MODEL, D_FF)), _const_spec((D_FF, D_MODEL)), vec],
        out_specs=row,
        out_shape=jax.ShapeDtypeStruct((T, D_MODEL), F32),
        compiler_params=pltpu.CompilerParams(
            dimension_semantics=("parallel",), vmem_limit_bytes=VMEM_LIMIT_BYTES),
        name="post",
    )(x2, hm, ha, ga, gb, wm, wa, wo, g, wu, wd, fg)


def _t5_bucket(n):
    max_exact = N_BUCKETS // 2
    n = np.maximum(n, 0)
    large = max_exact + (np.log(np.maximum(n, 1) / max_exact)
                         / np.log(MAX_DISTANCE / max_exact)
                         * (N_BUCKETS - max_exact)).astype(np.int32)
    large = np.minimum(large, N_BUCKETS - 1)
    return np.where(n < max_exact, n, large).astype(np.int32)


def _bucket_table():
    W = WINDOW
    dist = np.arange(W)[:, None] + W - np.arange(2 * W)[None, :]
    valid = (dist >= 0) & (dist < W)
    return jnp.asarray(np.where(valid, _t5_bucket(dist), -1).astype(np.int32))


def _regroup_kernel(w_ref, out_ref):
    n_gate = 2 * M_HEADS
    gate0 = sum(IN_SIZES[:3])
    tail0 = gate0 + n_gate
    n_tail = sum(IN_SIZES[5:])
    step = REGROUP_PIECE
    for src, dst, n, scale in ((0, 0, gate0, 1.0),
                               (tail0, gate0, A_Q_WIDTH, A_HEAD_DIM ** -0.5),
                               (tail0 + A_Q_WIDTH, gate0 + A_Q_WIDTH, n_tail - A_Q_WIDTH, 1.0)):
        for o in range(0, n, step):
            piece = w_ref[src + o:src + o + step, :]
            if scale != 1.0:
                piece = piece * scale
            out_ref[:, dst + o:dst + o + step] = piece.T.astype(BF16)
    gate_rows = jnp.concatenate(
        [w_ref[gate0:tail0, :], jnp.zeros((128 - n_gate, w_ref.shape[1]), F32)], axis=0)
    out_ref[:, gate0 + n_tail:] = gate_rows.T.astype(BF16)


def _regroup_w_in(w_in):
    w_t = jnp.swapaxes(w_in, 1, 2)
    depth, n_in, d = w_t.shape
    tc = REGROUP_COLS
    return pl.pallas_call(
        _regroup_kernel,
        grid=(depth, d // tc),
        in_specs=[pl.BlockSpec((None, n_in, tc), lambda l, i: (l, 0, i))],
        out_specs=pl.BlockSpec((None, tc, _INPROJ_WIDTH), lambda l, i: (l, i, 0)),
        out_shape=jax.ShapeDtypeStruct((depth, d, _INPROJ_WIDTH), BF16),
        compiler_params=pltpu.CompilerParams(
            dimension_semantics=("parallel", "parallel"), vmem_limit_bytes=VMEM_LIMIT_BYTES),
        name="regroup",
    )(w_t)


def kernel(x, norm_mix_g, w_in, conv_w, conv_b, b_igate, b_fgate, mlstm_norm_g, attn_sinks,
           rel_bias, w_branch_m, w_branch_a, w_out, norm_mlp_g, w_up, w_down, final_norm_g):
    B, S, D = x.shape
    T = B * S
    bucket = _bucket_table()
    w_all = _regroup_w_in(w_in)
    x2 = x.reshape(T, D)
    for l in range(DEPTH):
        mqk, mv, og, aq, ak, av, ga, gb, g_rows = _inproj(
            x2, norm_mix_g[l].reshape(1, D), conv_w[l], conv_b[l].reshape(1, -1),
            w_all[l], S)
        gate_b = jnp.concatenate([b_igate[l], b_fgate[l]]).reshape(2 * M_HEADS, 1)
        hm, ha = _mixers(mqk.reshape(B, S, -1), mv.reshape(B, S, -1), og.reshape(B, S, -1), g_rows,
                         gate_b, mlstm_norm_g[l].reshape(1, -1),
                         aq.reshape(B, S, -1), ak.reshape(B, S, -1), av.reshape(B, S, -1),
                         attn_sinks[l], rel_bias, bucket)
        x2 = _post(x2, hm.reshape(T, -1), ha.reshape(T, -1), ga, gb,
                   w_branch_m[l].astype(BF16), w_branch_a[l].astype(BF16), w_out[l].astype(BF16),
                   norm_mlp_g[l].reshape(1, D), w_up[l].astype(BF16), w_down[l].astype(BF16),
                   final_norm_g.reshape(1, D), final=(l == DEPTH - 1))
    return x2.reshape(B, S, D)
```

```python
import functools

import numpy as np
import jax
import jax.numpy as jnp
from jax import lax
from jax.experimental import pallas as pl
from jax.experimental.pallas import tpu as pltpu

D_MODEL = 1024
DEPTH = 2
M_HEADS = 4
M_QK_DIM = 128
M_V_DIM = 256
M_QK_WIDTH = M_HEADS * M_QK_DIM
M_V_WIDTH = M_HEADS * M_V_DIM
CONV_K = 4
A_Q_HEADS = 16
A_KV_HEADS = 4
A_HEAD_DIM = 64
A_GROUP = A_Q_HEADS // A_KV_HEADS
A_Q_WIDTH = A_Q_HEADS * A_HEAD_DIM
A_KV_WIDTH = A_KV_HEADS * A_HEAD_DIM
WINDOW = 128
N_BUCKETS = 32
MAX_DISTANCE = 128
D_FF = 4 * D_MODEL
EPS = 1e-6
IN_SIZES = (2 * M_QK_WIDTH, M_V_WIDTH, M_V_WIDTH, M_HEADS, M_HEADS,
            A_Q_WIDTH, A_KV_WIDTH, A_KV_WIDTH, D_MODEL, D_MODEL)

INPROJ_TM = 512
POST_TM = 512
PROJ_SUB = 256
REGROUP_COLS = 256
REGROUP_PIECE = 512
MLSTM_CHUNK = 128
MLSTM_BLOCKS = 2
CONV_HALO = 8
SWA_BLOCKS = 2
NEG_BIG = -1e30
LOG2E = 1.4426950408889634
VMEM_LIMIT_BYTES = 56 * 1024 * 1024

F32 = jnp.float32
BF16 = jnp.bfloat16


def _const_spec(shape):
    nd = len(shape)
    return pl.BlockSpec(shape, lambda *_: (0,) * nd, pipeline_mode=pl.Buffered(1))


def _layer_spec(layer, shape):
    nd = len(shape)
    return pl.BlockSpec((None,) + tuple(shape), lambda *_: (layer,) + (0,) * nd,
                        pipeline_mode=pl.Buffered(1))


def _rmsnorm(x, g):
    return x * lax.rsqrt(jnp.mean(x * x, axis=-1, keepdims=True) + EPS) * g


_INPROJ_GROUPS = (("qk", 2 * M_QK_WIDTH), ("v", M_V_WIDTH), ("o", M_V_WIDTH),
                  ("aq", A_Q_WIDTH), ("ak", A_KV_WIDTH), ("av", A_KV_WIDTH),
                  ("ga", D_MODEL), ("gb", D_MODEL), ("g", 128))
_INPROJ_OFFS = dict(zip((n for n, _ in _INPROJ_GROUPS),
                        np.cumsum([0] + [w for _, w in _INPROJ_GROUPS])[:-1].tolist()))
_INPROJ_WIDTH = sum(w for _, w in _INPROJ_GROUPS)


def _inproj_kernel(x_ref, g_ref, cw_ref, cb_ref, w_ref, *refs, tiles_per_seq):
    out_refs, halo_ref = refs[:-1], refs[-1]

    @pl.when(pl.program_id(0) % tiles_per_seq == 0)
    def _():
        halo_ref[...] = jnp.zeros_like(halo_ref)

    outs = dict(zip((n for n, _ in _INPROJ_GROUPS), out_refs))
    sub = PROJ_SUB
    for t in range(x_ref.shape[0] // sub):
        _inproj_rows(slice(t * sub, (t + 1) * sub), x_ref, g_ref, cw_ref, cb_ref, w_ref, outs, halo_ref)


def _inproj_rows(rows, x_ref, g_ref, cw_ref, cb_ref, w_ref, outs, halo_ref):
    n_rows = rows.stop - rows.start
    h = _rmsnorm(x_ref[rows, :], g_ref[...]).astype(BF16)
    res = jnp.dot(h, w_ref[...], preferred_element_type=F32)
    group = lambda name, width: res[:, _INPROJ_OFFS[name]:_INPROJ_OFFS[name] + width]

    head_row = lax.broadcasted_iota(jnp.int32, (CONV_HALO, 1), 0)
    gw = 2 * M_QK_WIDTH // 4
    for c in range(4):
        cols = slice(c * gw, (c + 1) * gw)
        r = res[:, cols]
        prev = halo_ref[:, cols]
        halo_ref[:, cols] = r[n_rows - CONV_HALO:, :]
        y = cb_ref[:, cols] + r * cw_ref[CONV_K - 1:CONV_K, cols]
        for s in range(1, CONV_K):
            rolled = pltpu.roll(r, s, axis=0)
            head = jnp.where(head_row < s, pltpu.roll(prev, s, axis=0), rolled[:CONV_HALO])
            shifted = jnp.concatenate([head, rolled[CONV_HALO:]], axis=0)
            y = y + shifted * cw_ref[CONV_K - 1 - s:CONV_K - s, cols]
        y = y * jax.nn.sigmoid(y)
        if c < 2:
            y = y * (M_QK_DIM ** -0.5)
        outs["qk"][rows, cols] = y.astype(BF16)

    for name, width in _INPROJ_GROUPS[1:-1]:
        val = group(name, width)
        if name in ("o", "ga", "gb"):
            val = jax.nn.sigmoid(val)
        if name in ("ak", "av"):
            half = A_HEAD_DIM
            swapped = [pltpu.roll(val[:, c:c + 2 * half], half, axis=1) for c in range(0, width, 2 * half)]
            val = jnp.concatenate([val] + swapped, axis=1)
        outs[name][rows, :] = val.astype(BF16)
    outs["g"][:, rows] = group("g", 128).T[:2 * M_HEADS, :]


def _inproj(layer, x2, g, conv_w, conv_b, w_all, seq_len):
    T = x2.shape[0]
    tm = INPROJ_TM
    row = lambda n: pl.BlockSpec((tm, n), lambda i: (i, 0))
    out_widths = [2 * w if n in ("ak", "av") else w for n, w in _INPROJ_GROUPS[:-1]] + [2 * M_HEADS]
    return pl.pallas_call(
        functools.partial(_inproj_kernel, tiles_per_seq=seq_len // tm),
        grid=(T // tm,),
        in_specs=[row(D_MODEL), _const_spec((1, D_MODEL)), _const_spec((CONV_K, 2 * M_QK_WIDTH)),
                  _const_spec((1, 2 * M_QK_WIDTH)), _layer_spec(layer, (D_MODEL, _INPROJ_WIDTH))],
        out_specs=[row(n) for n in out_widths[:-1]] + [pl.BlockSpec((2 * M_HEADS, tm), lambda i: (0, i))],
        out_shape=([jax.ShapeDtypeStruct((T, n), BF16) for n in out_widths[:-1]]
                   + [jax.ShapeDtypeStruct((2 * M_HEADS, T), F32)]),
        scratch_shapes=[pltpu.VMEM((CONV_HALO, 2 * M_QK_WIDTH), F32)],
        compiler_params=pltpu.CompilerParams(
            dimension_semantics=("arbitrary",), vmem_limit_bytes=VMEM_LIMIT_BYTES),
        name="inproj",
    )(x2, g, conv_w, conv_b, w_all)


def _mlstm_init(c_ref, n_ref, m_ref):
    @pl.when(pl.program_id(1) == 0)
    def _():
        c_ref[...] = jnp.zeros_like(c_ref)
        n_ref[...] = jnp.zeros_like(n_ref)
        m_ref[...] = jnp.zeros_like(m_ref)


def _mlstm_gates(blk, grow_ref, gb_ref):
    L = MLSTM_CHUNK
    rows = slice(blk * L, (blk + 1) * L)
    row_i = lax.broadcasted_iota(jnp.int32, (L, L), 0)
    col_j = lax.broadcasted_iota(jnp.int32, (L, L), 1)
    triu_f = (row_i <= col_j).astype(F32)
    g_rows = grow_ref[:, rows] + gb_ref[...]
    logf_rows = jax.nn.log_sigmoid(g_rows)
    b_rows = jnp.dot(logf_rows, triu_f, preferred_element_type=F32,
                     precision=lax.Precision.HIGHEST)
    return g_rows, logf_rows, b_rows


def _mlstm_head(blk, h, gates, qk_ref, v_ref, og_ref, ng_ref, out_ref, c_ref, n_ref, m_ref):
    L = MLSTM_CHUNK
    rows = slice(blk * L, (blk + 1) * L)
    row_i = lax.broadcasted_iota(jnp.int32, (L, L), 0)
    col_j = lax.broadcasted_iota(jnp.int32, (L, L), 1)
    tril = col_j <= row_i
    eye = col_j == row_i
    g_rows, logf_rows, b_rows = gates
    qb = qk_ref[rows, h * M_QK_DIM:(h + 1) * M_QK_DIM]
    kb = qk_ref[rows, M_QK_WIDTH + h * M_QK_DIM:M_QK_WIDTH + (h + 1) * M_QK_DIM]
    v = v_ref[rows, h * M_V_DIM:(h + 1) * M_V_DIM]
    q = qb.astype(F32)
    k = kb.astype(F32)

    m_prev = m_ref[h][:, 0:1]
    n_prev = n_ref[h]
    c_prev = c_ref[h]

    logf_row = logf_rows[M_HEADS + h:M_HEADS + h + 1, :]
    b_row = b_rows[M_HEADS + h:M_HEADS + h + 1, :]
    a_row = g_rows[h:h + 1, :] - b_row
    b_col = jnp.sum(jnp.where(tril, logf_row, 0.0), axis=-1, keepdims=True)
    a_col = jnp.sum(jnp.where(eye, a_row, 0.0), axis=-1, keepdims=True)
    b_last = jnp.sum(logf_row, axis=-1, keepdims=True)

    a_mat = jnp.where(tril, a_row, -jnp.inf)
    m_row = jnp.maximum(m_prev, jnp.max(a_mat, axis=-1, keepdims=True))
    w_intra = jnp.exp(a_mat - m_row)
    w_inter = jnp.exp(m_prev - m_row)

    s = lax.dot_general(qb, kb, (((1,), (1,)), ((), ())),
                        preferred_element_type=F32) * w_intra
    num = (jnp.dot(s.astype(BF16), v, preferred_element_type=F32)
           + w_inter * jnp.dot(qb, c_prev.astype(BF16), preferred_element_type=F32))
    den = (jnp.sum(s, axis=-1, keepdims=True)
           + w_inter * jnp.sum(q * n_prev, axis=-1, keepdims=True))
    hh = num / jnp.maximum(jnp.abs(den), jnp.exp(-(b_col + m_row)))

    m_last = jnp.maximum(m_prev, jnp.max(a_row, axis=-1, keepdims=True))
    decay = jnp.exp(m_prev - m_last)
    kw = k * jnp.exp(a_col - m_last)
    c_ref[h] = decay * c_prev + lax.dot_general(
        kw.astype(BF16), v, (((0,), (0,)), ((), ())), preferred_element_type=F32)
    n_ref[h] = decay * n_prev + jnp.sum(kw, axis=0, keepdims=True)
    m_ref[h] = jnp.broadcast_to(b_last + m_last, (1, 128))

    sl = slice(h * M_V_DIM, (h + 1) * M_V_DIM)
    hn = hh * lax.rsqrt(jnp.mean(hh * hh, axis=-1, keepdims=True) + EPS) * ng_ref[:, sl]
    out_ref[rows, sl] = hn.astype(BF16) * og_ref[rows, sl]


def _swa_init(sink_ref, rel_ref, bucket_ref, bias_ref):
    W = WINDOW

    @pl.when((pl.program_id(0) == 0) & (pl.program_id(1) == 0))
    def _():
        bucket = bucket_ref[...]
        neg = jnp.full((W, 2 * W), NEG_BIG, F32)
        for hq in range(A_Q_HEADS):
            bias_ref[1, hq] = neg

        def fill(bb, carry):
            hit = bucket == bb
            for hq in range(A_Q_HEADS):
                bias_ref[1, hq] = jnp.where(hit, rel_ref[bb, hq] * LOG2E, bias_ref[1, hq])
            return carry
        lax.fori_loop(0, N_BUCKETS, fill, 0)
        col = lax.broadcasted_iota(jnp.int32, (W, 2 * W), 1)
        for hq in range(A_Q_HEADS):
            sink = sink_ref[hq] * LOG2E
            bias_ref[0, hq] = jnp.where(col == 0, sink, jnp.where(col < W, NEG_BIG, bias_ref[1, hq]))
            bias_ref[1, hq] = jnp.where(col == 0, sink, bias_ref[1, hq])


def _swa_kv_head(blk, kv, q_ref, kp_ref, kc_ref, vp_ref, vc_ref, out_ref, bias_ref):
    W = WINDOW
    lane_lo = lax.broadcasted_iota(jnp.int32, (2 * W, 2 * A_HEAD_DIM), 1) < A_HEAD_DIM
    not_sink = lax.broadcasted_iota(jnp.int32, (2 * W, 2 * A_HEAD_DIM), 0) > 0
    keep_lo = lane_lo & not_sink
    keep_hi = jnp.logical_not(lane_lo) & not_sink
    zero = jnp.zeros((), BF16)
    ones_lo = jnp.where(lane_lo, 1.0, 0.0).astype(BF16)
    ones_hi = jnp.where(lane_lo, 0.0, 1.0).astype(BF16)
    nt = (((1,), (1,)), ((), ()))
    rows = slice(blk * W, (blk + 1) * W)
    var = jnp.minimum(pl.program_id(1), 1) if blk == 0 else 1
    def tile(ref, rws, swapped):
        c0 = (A_KV_WIDTH if swapped else 0) + (kv // 2) * 2 * A_HEAD_DIM
        return ref[rws, c0:c0 + 2 * A_HEAD_DIM]

    def prev_cur(p_ref, c_ref, swapped):
        prev = (tile(p_ref, slice(None), swapped) if blk == 0
                else tile(c_ref, slice((blk - 1) * W, blk * W), swapped))
        return jnp.concatenate([prev, tile(c_ref, rows, swapped)], axis=0)

    in_low = kv % 2 == 1
    k_even = jnp.where(keep_lo, prev_cur(kp_ref, kc_ref, in_low), zero)
    k_odd = jnp.where(keep_hi, prev_cur(kp_ref, kc_ref, not in_low), zero)
    v_even = jnp.concatenate([jnp.where(keep_lo, prev_cur(vp_ref, vc_ref, in_low), zero), ones_lo], axis=1)
    v_odd = jnp.concatenate([jnp.where(keep_hi, prev_cur(vp_ref, vc_ref, not in_low), zero), ones_hi], axis=1)
    v_both = jnp.concatenate([v_even, v_odd], axis=0)
    for pair in range(A_GROUP // 2):
        pi = kv * (A_GROUP // 2) + pair
        cols = slice(pi * 2 * A_HEAD_DIM, (pi + 1) * 2 * A_HEAD_DIM)
        q2 = q_ref[rows, cols]
        s_e = lax.dot_general(q2, k_even, nt, preferred_element_type=F32) + bias_ref[var, 2 * pi]
        s_o = lax.dot_general(q2, k_odd, nt, preferred_element_type=F32) + bias_ref[var, 2 * pi + 1]
        p_e = jnp.exp2(s_e - jnp.max(s_e, axis=-1, keepdims=True)).astype(BF16)
        p_o = jnp.exp2(s_o - jnp.max(s_o, axis=-1, keepdims=True)).astype(BF16)
        acc = jnp.dot(jnp.concatenate([p_e, p_o], axis=1), v_both,
                      preferred_element_type=F32)
        out_ref[rows, cols] = (acc[:, :2 * A_HEAD_DIM] / acc[:, 2 * A_HEAD_DIM:]).astype(out_ref.dtype)


def _mixers_kernel(sink_ref, rel_ref, bucket_ref, qk_ref, v_ref, og_ref, grow_ref, gb_ref, ng_ref,
                   q_ref, kp_ref, kc_ref, vp_ref, vc_ref,
                   hm_ref, ha_ref, c_ref, n_ref, m_ref, bias_ref):
    _mlstm_init(c_ref, n_ref, m_ref)
    _swa_init(sink_ref, rel_ref, bucket_ref, bias_ref)
    assert A_KV_HEADS == M_HEADS
    for blk in range(SWA_BLOCKS):
        for u in range(A_KV_HEADS):
            _swa_kv_head(blk, u, q_ref, kp_ref, kc_ref, vp_ref, vc_ref, ha_ref, bias_ref)
        gates = _mlstm_gates(blk, grow_ref, gb_ref)
        for u in range(M_HEADS):
            _mlstm_head(blk, u, gates, qk_ref, v_ref, og_ref, ng_ref, hm_ref, c_ref, n_ref, m_ref)


def _mixers(mqk, mv, og, g_rows, gate_b, norm_g, aq, ak2, av2, sinks, rel_bias, bucket):
    B, S, _ = aq.shape
    W = WINDOW
    tq = SWA_BLOCKS * W
    assert tq == MLSTM_BLOCKS * MLSTM_CHUNK
    kvw = 2 * A_KV_WIDTH
    cur = lambda n: pl.BlockSpec((None, tq, n), lambda b, i: (b, i, 0))
    prev = pl.BlockSpec((None, W, kvw), lambda b, i: (b, jnp.maximum(i * SWA_BLOCKS - 1, 0), 0))
    smem = pl.BlockSpec(memory_space=pltpu.SMEM)
    return pl.pallas_call(
        _mixers_kernel,
        grid=(B, S // tq),
        in_specs=[smem, smem, _const_spec((W, 2 * W)),
                  cur(2 * M_QK_WIDTH), cur(M_V_WIDTH), cur(M_V_WIDTH),
                  pl.BlockSpec((2 * M_HEADS, tq), lambda b, i: (0, b * (S // tq) + i)),
                  _const_spec((2 * M_HEADS, 1)), _const_spec((1, M_V_WIDTH)),
                  cur(A_Q_WIDTH), prev, cur(kvw), prev, cur(kvw)],
        out_specs=[cur(M_V_WIDTH), cur(A_Q_WIDTH)],
        out_shape=[jax.ShapeDtypeStruct((B, S, M_V_WIDTH), BF16),
                   jax.ShapeDtypeStruct((B, S, A_Q_WIDTH), BF16)],
        scratch_shapes=[pltpu.VMEM((M_HEADS, M_QK_DIM, M_V_DIM), F32),
                        pltpu.VMEM((M_HEADS, 1, M_QK_DIM), F32),
                        pltpu.VMEM((M_HEADS, 1, 128), F32),
                        pltpu.VMEM((2, A_Q_HEADS, W, 2 * W), F32)],
        compiler_params=pltpu.CompilerParams(
            dimension_semantics=("arbitrary", "arbitrary"), vmem_limit_bytes=VMEM_LIMIT_BYTES),
        name="mixers",
    )(sinks, rel_bias, bucket, mqk, mv, og, g_rows, gate_b, norm_g, aq, ak2, ak2, av2, av2)


def _post_kernel(x_ref, hm_ref, ha_ref, ga_ref, gb_ref, wm_ref, wa_ref, wo_ref,
                 g_ref, wu_ref, wd_ref, fg_ref, out_ref, *, final):
    sub = PROJ_SUB
    for t in range(x_ref.shape[0] // sub):
        rows = slice(t * sub, (t + 1) * sub)
        y = (ga_ref[rows, :] * jnp.dot(hm_ref[rows, :], wm_ref[...], preferred_element_type=F32)
             + gb_ref[rows, :] * jnp.dot(ha_ref[rows, :], wa_ref[...], preferred_element_type=F32))
        x = x_ref[rows, :] + jnp.dot(y.astype(BF16), wo_ref[...], preferred_element_type=F32)
        h = _rmsnorm(x, g_ref[...]).astype(BF16)
        u = jnp.maximum(jnp.dot(h, wu_ref[...], preferred_element_type=F32), 0.0)
        x = x + jnp.dot((u * u).astype(BF16), wd_ref[...], preferred_element_type=F32)
        if final:
            x = _rmsnorm(x, fg_ref[...])
        out_ref[rows, :] = x


def _post(layer, x2, hm, ha, ga, gb, wm, wa, wo, g, wu, wd, fg, final):
    T = x2.shape[0]
    tm = POST_TM
    row = pl.BlockSpec((tm, D_MODEL), lambda i: (i, 0))
    wspec = _layer_spec(layer, (D_MODEL, D_MODEL))
    vec = _const_spec((1, D_MODEL))
    return pl.pallas_call(
        functools.partial(_post_kernel, final=final),
        grid=(T // tm,),
        in_specs=[row, row, row, row, row, wspec, wspec, wspec,
                  vec, _layer_spec(layer, (D_MODEL, D_FF)), _layer_spec(layer, (D_FF, D_MODEL)), vec],
        out_specs=row,
        out_shape=jax.ShapeDtypeStruct((T, D_MODEL), F32),
        compiler_params=pltpu.CompilerParams(
            dimension_semantics=("parallel",), vmem_limit_bytes=VMEM_LIMIT_BYTES),
        name="post",
    )(x2, hm, ha, ga, gb, wm, wa, wo, g, wu, wd, fg)


def _t5_bucket(n):
    max_exact = N_BUCKETS // 2
    n = np.maximum(n, 0)
    large = max_exact + (np.log(np.maximum(n, 1) / max_exact)
                         / np.log(MAX_DISTANCE / max_exact)
                         * (N_BUCKETS - max_exact)).astype(np.int32)
    large = np.minimum(large, N_BUCKETS - 1)
    return np.where(n < max_exact, n, large).astype(np.int32)


def _bucket_table():
    W = WINDOW
    dist = np.arange(W)[:, None] + W - np.arange(2 * W)[None, :]
    valid = (dist >= 0) & (dist < W)
    return jnp.asarray(np.where(valid, _t5_bucket(dist), -1).astype(np.int32))


def _regroup_kernel(w_ref, out_ref):
    n_gate = 2 * M_HEADS
    gate0 = sum(IN_SIZES[:3])
    tail0 = gate0 + n_gate
    n_tail = sum(IN_SIZES[5:])
    step = REGROUP_PIECE
    for src, dst, n, scale in ((0, 0, gate0, 1.0),
                               (tail0, gate0, A_Q_WIDTH, A_HEAD_DIM ** -0.5 * LOG2E),
                               (tail0 + A_Q_WIDTH, gate0 + A_Q_WIDTH, n_tail - A_Q_WIDTH, 1.0)):
        for o in range(0, n, step):
            piece = w_ref[src + o:src + o + step, :]
            if scale != 1.0:
                piece = piece * scale
            out_ref[:, dst + o:dst + o + step] = piece.T.astype(BF16)
    gate_rows = jnp.concatenate(
        [w_ref[gate0:tail0, :], jnp.zeros((128 - n_gate, w_ref.shape[1]), F32)], axis=0)
    out_ref[:, gate0 + n_tail:] = gate_rows.T.astype(BF16)


def _regroup_w_in(w_in):
    w_t = jnp.swapaxes(w_in, 1, 2)
    depth, n_in, d = w_t.shape
    tc = REGROUP_COLS
    return pl.pallas_call(
        _regroup_kernel,
        grid=(depth, d // tc),
        in_specs=[pl.BlockSpec((None, n_in, tc), lambda l, i: (l, 0, i))],
        out_specs=pl.BlockSpec((None, tc, _INPROJ_WIDTH), lambda l, i: (l, i, 0)),
        out_shape=jax.ShapeDtypeStruct((depth, d, _INPROJ_WIDTH), BF16),
        compiler_params=pltpu.CompilerParams(
            dimension_semantics=("parallel", "parallel"), vmem_limit_bytes=VMEM_LIMIT_BYTES),
        name="regroup",
    )(w_t)


def kernel(x, norm_mix_g, w_in, conv_w, conv_b, b_igate, b_fgate, mlstm_norm_g, attn_sinks,
           rel_bias, w_branch_m, w_branch_a, w_out, norm_mlp_g, w_up, w_down, final_norm_g):
    B, S, D = x.shape
    T = B * S
    bucket = _bucket_table()
    w_all = _regroup_w_in(w_in)
    wm, wa, wo, wu, wd = (w.astype(BF16) for w in (w_branch_m, w_branch_a, w_out, w_up, w_down))
    x2 = x.reshape(T, D)
    for l in range(DEPTH):
        mqk, mv, og, aq, ak, av, ga, gb, g_rows = _inproj(
            l, x2, norm_mix_g[l].reshape(1, D), conv_w[l], conv_b[l].reshape(1, -1), w_all, S)
        gate_b = jnp.concatenate([b_igate[l], b_fgate[l]]).reshape(2 * M_HEADS, 1)
        hm, ha = _mixers(mqk.reshape(B, S, -1), mv.reshape(B, S, -1), og.reshape(B, S, -1), g_rows,
                         gate_b, mlstm_norm_g[l].reshape(1, -1),
                         aq.reshape(B, S, -1), ak.reshape(B, S, -1), av.reshape(B, S, -1),
                         attn_sinks[l], rel_bias, bucket)
        x2 = _post(l, x2, hm.reshape(T, -1), ha.reshape(T, -1), ga, gb, wm, wa, wo,
                   norm_mlp_g[l].reshape(1, D), wu, wd,
                   final_norm_g.reshape(1, D), final=(l == DEPTH - 1))
    return x2.reshape(B, S, D)
```

```python
import functools

import numpy as np
import jax
import jax.numpy as jnp
from jax import lax
from jax.experimental import pallas as pl
from jax.experimental.pallas import tpu as pltpu

D_MODEL = 1024
DEPTH = 2
M_HEADS = 4
M_QK_DIM = 128
M_V_DIM = 256
M_QK_WIDTH = M_HEADS * M_QK_DIM
M_V_WIDTH = M_HEADS * M_V_DIM
CONV_K = 4
A_Q_HEADS = 16
A_KV_HEADS = 4
A_HEAD_DIM = 64
A_GROUP = A_Q_HEADS // A_KV_HEADS
A_Q_WIDTH = A_Q_HEADS * A_HEAD_DIM
A_KV_WIDTH = A_KV_HEADS * A_HEAD_DIM
WINDOW = 128
N_BUCKETS = 32
MAX_DISTANCE = 128
D_FF = 4 * D_MODEL
EPS = 1e-6
IN_SIZES = (2 * M_QK_WIDTH, M_V_WIDTH, M_V_WIDTH, M_HEADS, M_HEADS,
            A_Q_WIDTH, A_KV_WIDTH, A_KV_WIDTH, D_MODEL, D_MODEL)

INPROJ_TM = 512
POST_TM = 512
PROJ_SUB = 256
REGROUP_COLS = 256
REGROUP_PIECE = 512
MLSTM_CHUNK = 128
MLSTM_BLOCKS = 2
CONV_HALO = 8
SWA_BLOCKS = 2
NEG_BIG = -1e30
LOG2E = 1.4426950408889634
VMEM_LIMIT_BYTES = 56 * 1024 * 1024

F32 = jnp.float32
BF16 = jnp.bfloat16


def _const_spec(shape):
    nd = len(shape)
    return pl.BlockSpec(shape, lambda *_: (0,) * nd, pipeline_mode=pl.Buffered(1))


def _layer_spec(layer, shape):
    nd = len(shape)
    return pl.BlockSpec((None,) + tuple(shape), lambda *_: (layer,) + (0,) * nd,
                        pipeline_mode=pl.Buffered(1))


def _rmsnorm(x, g):
    return x * lax.rsqrt(jnp.mean(x * x, axis=-1, keepdims=True) + EPS) * g


_INPROJ_GROUPS = (("qk", 2 * M_QK_WIDTH), ("v", M_V_WIDTH), ("o", M_V_WIDTH),
                  ("aq", A_Q_WIDTH), ("ak", A_KV_WIDTH), ("av", A_KV_WIDTH),
                  ("ga", D_MODEL), ("gb", D_MODEL), ("g", 128))
_INPROJ_OFFS = dict(zip((n for n, _ in _INPROJ_GROUPS),
                        np.cumsum([0] + [w for _, w in _INPROJ_GROUPS])[:-1].tolist()))
_INPROJ_WIDTH = sum(w for _, w in _INPROJ_GROUPS)


def _inproj_kernel(x_ref, g_ref, cw_ref, cb_ref, w_ref, *refs, tiles_per_seq):
    out_refs, halo_ref = refs[:-1], refs[-1]

    @pl.when(pl.program_id(0) % tiles_per_seq == 0)
    def _():
        halo_ref[...] = jnp.zeros_like(halo_ref)

    outs = dict(zip((n for n, _ in _INPROJ_GROUPS), out_refs))
    sub = PROJ_SUB
    for t in range(x_ref.shape[0] // sub):
        _inproj_rows(slice(t * sub, (t + 1) * sub), x_ref, g_ref, cw_ref, cb_ref, w_ref, outs, halo_ref)


def _inproj_rows(rows, x_ref, g_ref, cw_ref, cb_ref, w_ref, outs, halo_ref):
    n_rows = rows.stop - rows.start
    h = _rmsnorm(x_ref[rows, :], g_ref[...]).astype(BF16)
    res = jnp.dot(h, w_ref[...], preferred_element_type=F32)
    group = lambda name, width: res[:, _INPROJ_OFFS[name]:_INPROJ_OFFS[name] + width]

    head_row = lax.broadcasted_iota(jnp.int32, (CONV_HALO, 1), 0)
    gw = 2 * M_QK_WIDTH // 4
    for c in range(4):
        cols = slice(c * gw, (c + 1) * gw)
        r = res[:, cols]
        prev = halo_ref[:, cols]
        halo_ref[:, cols] = r[n_rows - CONV_HALO:, :]
        y = cb_ref[:, cols] + r * cw_ref[CONV_K - 1:CONV_K, cols]
        for s in range(1, CONV_K):
            rolled = pltpu.roll(r, s, axis=0)
            head = jnp.where(head_row < s, pltpu.roll(prev, s, axis=0), rolled[:CONV_HALO])
            shifted = jnp.concatenate([head, rolled[CONV_HALO:]], axis=0)
            y = y + shifted * cw_ref[CONV_K - 1 - s:CONV_K - s, cols]
        y = y * jax.nn.sigmoid(y)
        if c < 2:
            y = y * (M_QK_DIM ** -0.5)
        outs["qk"][rows, cols] = y.astype(BF16)

    for name, width in _INPROJ_GROUPS[1:-1]:
        val = group(name, width)
        if name in ("o", "ga", "gb"):
            val = jax.nn.sigmoid(val)
        if name in ("ak", "av"):
            half = A_HEAD_DIM
            swapped = [pltpu.roll(val[:, c:c + 2 * half], half, axis=1) for c in range(0, width, 2 * half)]
            val = jnp.concatenate([val] + swapped, axis=1)
        outs[name][rows, :] = val.astype(BF16)
    outs["g"][:, rows] = group("g", 128).T[:2 * M_HEADS, :]


def _inproj(layer, x2, g, conv_w, conv_b, w_all, seq_len):
    T = x2.shape[0]
    tm = INPROJ_TM
    row = lambda n: pl.BlockSpec((tm, n), lambda i: (i, 0))
    out_widths = [2 * w if n in ("ak", "av") else w for n, w in _INPROJ_GROUPS[:-1]] + [2 * M_HEADS]
    return pl.pallas_call(
        functools.partial(_inproj_kernel, tiles_per_seq=seq_len // tm),
        grid=(T // tm,),
        in_specs=[row(D_MODEL), _const_spec((1, D_MODEL)), _const_spec((CONV_K, 2 * M_QK_WIDTH)),
                  _const_spec((1, 2 * M_QK_WIDTH)), _layer_spec(layer, (D_MODEL, _INPROJ_WIDTH))],
        out_specs=[row(n) for n in out_widths[:-1]] + [pl.BlockSpec((2 * M_HEADS, tm), lambda i: (0, i))],
        out_shape=([jax.ShapeDtypeStruct((T, n), BF16) for n in out_widths[:-1]]
                   + [jax.ShapeDtypeStruct((2 * M_HEADS, T), F32)]),
        scratch_shapes=[pltpu.VMEM((CONV_HALO, 2 * M_QK_WIDTH), F32)],
        compiler_params=pltpu.CompilerParams(
            dimension_semantics=("arbitrary",), vmem_limit_bytes=VMEM_LIMIT_BYTES),
        name="inproj",
    )(x2, g, conv_w, conv_b, w_all)


def _mlstm_init(c_ref, m_ref):
    @pl.when(pl.program_id(1) == 0)
    def _():
        c_ref[...] = jnp.zeros_like(c_ref)
        m_ref[...] = jnp.zeros_like(m_ref)


def _mlstm_gates(blk, grow_ref, gb_ref, m_ref):
    L = MLSTM_CHUNK
    rows = slice(blk * L, (blk + 1) * L)
    row_i = lax.broadcasted_iota(jnp.int32, (L, L), 0)
    col_j = lax.broadcasted_iota(jnp.int32, (L, L), 1)
    triu_f = (row_i <= col_j).astype(F32)
    g = grow_ref[:, rows] + gb_ref[...]
    logf = pltpu.roll(jax.nn.log_sigmoid(g), M_HEADS, axis=0) * LOG2E
    b = jnp.dot(logf, triu_f, preferred_element_type=F32,
                precision=lax.Precision.HIGHEST)
    a = g * LOG2E - b
    m_prev = m_ref[...]
    m_last = jnp.maximum(m_prev, jnp.max(a, axis=-1, keepdims=True))
    wk = jnp.exp2(a - m_last)
    decay = jnp.exp2(m_prev - m_last)
    m_ref[...] = jnp.sum(logf, axis=-1, keepdims=True) + m_last
    return a, logf, m_prev, wk, decay


def _mlstm_chunk(blk, gates, qk_ref, v_ref, og_ref, ng_ref, out_ref, c_ref, between):
    L = MLSTM_CHUNK
    rows = slice(blk * L, (blk + 1) * L)
    row_i = lax.broadcasted_iota(jnp.int32, (L, L), 0)
    col_j = lax.broadcasted_iota(jnp.int32, (L, L), 1)
    tril = col_j <= row_i
    eye_b = jnp.where(col_j == row_i, 1.0, 0.0).astype(BF16)
    ones_b = jnp.ones((L, 128), BF16)
    mean_b = jnp.full((M_V_DIM, 128), 1.0 / M_V_DIM, BF16)
    a, logf, m_prev_rows, wk, decay_rows = gates
    nt = (((1,), (1,)), ((), ()))
    heads = range(M_HEADS)

    qb, v_ext, s_raw, k_t, m_prev = [], [], [], [], []
    for h in heads:
        qb.append(qk_ref[rows, h * M_QK_DIM:(h + 1) * M_QK_DIM])
        kb = qk_ref[rows, M_QK_WIDTH + h * M_QK_DIM:M_QK_WIDTH + (h + 1) * M_QK_DIM]
        v_ext.append(jnp.concatenate([v_ref[rows, h * M_V_DIM:(h + 1) * M_V_DIM], ones_b], axis=1))
        st = lax.dot_general(jnp.concatenate([qb[h], eye_b], axis=0), kb, nt,
                             preferred_element_type=F32)
        s_raw.append(st[:L])
        k_t.append(st[L:])
        m_prev.append(m_prev_rows[h:h + 1, 0:1])
    between[0]()

    both, floor = [], []
    zeros_b = jnp.zeros((M_QK_DIM, M_QK_DIM), BF16)
    for h in heads:
        a_mat = jnp.where(tril, a[h:h + 1, :], -jnp.inf)
        m_row = jnp.maximum(m_prev[h], jnp.max(a_mat, axis=-1, keepdims=True))
        b_col = jnp.sum(jnp.where(tril, logf[h:h + 1, :], 0.0), axis=-1, keepdims=True)
        floor.append(jnp.exp2(-(b_col + m_row)))
        s = s_raw[h] * jnp.exp2(a_mat - m_row)
        q_inter = (qb[h].astype(F32) * jnp.exp2(m_prev[h] - m_row)).astype(BF16)
        kw_t = (k_t[h] * wk[h:h + 1, :]).astype(BF16)
        c_prev = c_ref[h]
        lhs = jnp.concatenate([jnp.concatenate([s.astype(BF16), q_inter], axis=1),
                               jnp.concatenate([kw_t, zeros_b], axis=1)], axis=0)
        res = jnp.dot(lhs, jnp.concatenate([v_ext[h], c_prev.astype(BF16)], axis=0),
                      preferred_element_type=F32)
        both.append(res[:L])
        c_ref[h] = decay_rows[h:h + 1, 0:1] * c_prev + res[L:]
    between[1]()

    hh = []
    for h in heads:
        inv = 1.0 / jnp.maximum(jnp.abs(both[h][:, M_V_DIM:]), floor[h])
        hh.append(both[h][:, :M_V_DIM] * jnp.concatenate([inv, inv], axis=1))
    msq_all = jnp.dot(jnp.concatenate([(x * x).astype(BF16) for x in hh], axis=0), mean_b,
                      preferred_element_type=F32)
    msq = [msq_all[h * L:(h + 1) * L] for h in heads]
    between[2]()

    for h in heads:
        scale = lax.rsqrt(msq[h] + EPS)
        sl = slice(h * M_V_DIM, (h + 1) * M_V_DIM)
        hn = hh[h] * jnp.concatenate([scale, scale], axis=1) * ng_ref[:, sl]
        out_ref[rows, sl] = hn.astype(BF16) * og_ref[rows, sl]
    between[3]()


def _swa_init(sink_ref, rel_ref, bucket_ref, bias_ref):
    W = WINDOW

    @pl.when((pl.program_id(0) == 0) & (pl.program_id(1) == 0))
    def _():
        bucket = bucket_ref[...]
        neg = jnp.full((W, 2 * W), NEG_BIG, F32)
        for hq in range(A_Q_HEADS):
            bias_ref[1, hq] = neg

        def fill(bb, carry):
            hit = bucket == bb
            for hq in range(A_Q_HEADS):
                bias_ref[1, hq] = jnp.where(hit, rel_ref[bb, hq] * LOG2E, bias_ref[1, hq])
            return carry
        lax.fori_loop(0, N_BUCKETS, fill, 0)
        col = lax.broadcasted_iota(jnp.int32, (W, 2 * W), 1)
        for hq in range(A_Q_HEADS):
            sink = sink_ref[hq] * LOG2E
            bias_ref[0, hq] = jnp.where(col == 0, sink, jnp.where(col < W, NEG_BIG, bias_ref[1, hq]))
            bias_ref[1, hq] = jnp.where(col == 0, sink, bias_ref[1, hq])


def _swa_kv_head(blk, kv, q_ref, kp_ref, kc_ref, vp_ref, vc_ref, out_ref, bias_ref):
    W = WINDOW
    lane_lo = lax.broadcasted_iota(jnp.int32, (2 * W, 2 * A_HEAD_DIM), 1) < A_HEAD_DIM
    not_sink = lax.broadcasted_iota(jnp.int32, (2 * W, 2 * A_HEAD_DIM), 0) > 0
    keep_lo = lane_lo & not_sink
    keep_hi = jnp.logical_not(lane_lo) & not_sink
    zero = jnp.zeros((), BF16)
    ones_lo = jnp.where(lane_lo, 1.0, 0.0).astype(BF16)
    ones_hi = jnp.where(lane_lo, 0.0, 1.0).astype(BF16)
    nt = (((1,), (1,)), ((), ()))
    rows = slice(blk * W, (blk + 1) * W)
    var = jnp.minimum(pl.program_id(1), 1) if blk == 0 else 1
    def tile(ref, rws, swapped):
        c0 = (A_KV_WIDTH if swapped else 0) + (kv // 2) * 2 * A_HEAD_DIM
        return ref[rws, c0:c0 + 2 * A_HEAD_DIM]

    def prev_cur(p_ref, c_ref, swapped):
        prev = (tile(p_ref, slice(None), swapped) if blk == 0
                else tile(c_ref, slice((blk - 1) * W, blk * W), swapped))
        return jnp.concatenate([prev, tile(c_ref, rows, swapped)], axis=0)

    in_low = kv % 2 == 1
    k_even = jnp.where(keep_lo, prev_cur(kp_ref, kc_ref, in_low), zero)
    k_odd = jnp.where(keep_hi, prev_cur(kp_ref, kc_ref, not in_low), zero)
    v_even = jnp.concatenate([jnp.where(keep_lo, prev_cur(vp_ref, vc_ref, in_low), zero), ones_lo], axis=1)
    v_odd = jnp.concatenate([jnp.where(keep_hi, prev_cur(vp_ref, vc_ref, not in_low), zero), ones_hi], axis=1)
    v_both = jnp.concatenate([v_even, v_odd], axis=0)
    pairs = [kv * (A_GROUP // 2) + p for p in range(A_GROUP // 2)]
    cols = [slice(pi * 2 * A_HEAD_DIM, (pi + 1) * 2 * A_HEAD_DIM) for pi in pairs]
    q_st = jnp.concatenate([q_ref[rows, c] for c in cols], axis=0)
    bias_e = jnp.concatenate([bias_ref[var, 2 * pi] for pi in pairs], axis=0)
    bias_o = jnp.concatenate([bias_ref[var, 2 * pi + 1] for pi in pairs], axis=0)
    s_e = lax.dot_general(q_st, k_even, nt, preferred_element_type=F32) + bias_e
    s_o = lax.dot_general(q_st, k_odd, nt, preferred_element_type=F32) + bias_o
    p_e = jnp.exp2(s_e - jnp.max(s_e, axis=-1, keepdims=True)).astype(BF16)
    p_o = jnp.exp2(s_o - jnp.max(s_o, axis=-1, keepdims=True)).astype(BF16)
    acc = jnp.dot(jnp.concatenate([p_e, p_o], axis=1), v_both,
                  preferred_element_type=F32)
    out = (acc[:, :2 * A_HEAD_DIM] / acc[:, 2 * A_HEAD_DIM:]).astype(out_ref.dtype)
    for p, c in enumerate(cols):
        out_ref[rows, c] = out[p * W:(p + 1) * W]


def _mixers_kernel(sink_ref, rel_ref, bucket_ref, qk_ref, v_ref, og_ref, grow_ref, gb_ref, ng_ref,
                   q_ref, kp_ref, kc_ref, vp_ref, vc_ref,
                   hm_ref, ha_ref, c_ref, m_ref, bias_ref):
    _mlstm_init(c_ref, m_ref)
    _swa_init(sink_ref, rel_ref, bucket_ref, bias_ref)
    for blk in range(SWA_BLOCKS):
        gates = _mlstm_gates(blk, grow_ref, gb_ref, m_ref)
        swa = [functools.partial(_swa_kv_head, blk, u, q_ref, kp_ref, kc_ref, vp_ref, vc_ref,
                                 ha_ref, bias_ref) for u in range(A_KV_HEADS)]
        _mlstm_chunk(blk, gates, qk_ref, v_ref, og_ref, ng_ref, hm_ref, c_ref, between=swa)


def _mixers(mqk, mv, og, g_rows, gate_b, norm_g, aq, ak2, av2, sinks, rel_bias, bucket):
    B, S, _ = aq.shape
    W = WINDOW
    tq = SWA_BLOCKS * W
    assert tq == MLSTM_BLOCKS * MLSTM_CHUNK
    kvw = 2 * A_KV_WIDTH
    cur = lambda n: pl.BlockSpec((None, tq, n), lambda b, i: (b, i, 0))
    prev = pl.BlockSpec((None, W, kvw), lambda b, i: (b, jnp.maximum(i * SWA_BLOCKS - 1, 0), 0))
    smem = pl.BlockSpec(memory_space=pltpu.SMEM)
    return pl.pallas_call(
        _mixers_kernel,
        grid=(B, S // tq),
        in_specs=[smem, smem, _const_spec((W, 2 * W)),
                  cur(2 * M_QK_WIDTH), cur(M_V_WIDTH), cur(M_V_WIDTH),
                  pl.BlockSpec((2 * M_HEADS, tq), lambda b, i: (0, b * (S // tq) + i)),
                  _const_spec((2 * M_HEADS, 1)), _const_spec((1, M_V_WIDTH)),
                  cur(A_Q_WIDTH), prev, cur(kvw), prev, cur(kvw)],
        out_specs=[cur(M_V_WIDTH), cur(A_Q_WIDTH)],
        out_shape=[jax.ShapeDtypeStruct((B, S, M_V_WIDTH), BF16),
                   jax.ShapeDtypeStruct((B, S, A_Q_WIDTH), BF16)],
        scratch_shapes=[pltpu.VMEM((M_HEADS, M_QK_DIM, M_V_DIM + 128), F32),
                        pltpu.VMEM((2 * M_HEADS, MLSTM_CHUNK), F32),
                        pltpu.VMEM((2, A_Q_HEADS, W, 2 * W), F32)],
        compiler_params=pltpu.CompilerParams(
            dimension_semantics=("arbitrary", "arbitrary"), vmem_limit_bytes=VMEM_LIMIT_BYTES),
        name="mixers",
    )(sinks, rel_bias, bucket, mqk, mv, og, g_rows, gate_b, norm_g, aq, ak2, ak2, av2, av2)


def _post_kernel(x_ref, hm_ref, ha_ref, ga_ref, gb_ref, wm_ref, wa_ref, wo_ref,
                 g_ref, wu_ref, wd_ref, fg_ref, out_ref, *, final):
    sub = PROJ_SUB
    for t in range(x_ref.shape[0] // sub):
        rows = slice(t * sub, (t + 1) * sub)
        y = (ga_ref[rows, :] * jnp.dot(hm_ref[rows, :], wm_ref[...], preferred_element_type=F32)
             + gb_ref[rows, :] * jnp.dot(ha_ref[rows, :], wa_ref[...], preferred_element_type=F32))
        x = x_ref[rows, :] + jnp.dot(y.astype(BF16), wo_ref[...], preferred_element_type=F32)
        h = _rmsnorm(x, g_ref[...]).astype(BF16)
        u = jnp.maximum(jnp.dot(h, wu_ref[...], preferred_element_type=F32), 0.0)
        x = x + jnp.dot((u * u).astype(BF16), wd_ref[...], preferred_element_type=F32)
        if final:
            x = _rmsnorm(x, fg_ref[...])
        out_ref[rows, :] = x


def _post(layer, x2, hm, ha, ga, gb, wm, wa, wo, g, wu, wd, fg, final):
    T = x2.shape[0]
    tm = POST_TM
    row = pl.BlockSpec((tm, D_MODEL), lambda i: (i, 0))
    wspec = _layer_spec(layer, (D_MODEL, D_MODEL))
    vec = _const_spec((1, D_MODEL))
    return pl.pallas_call(
        functools.partial(_post_kernel, final=final),
        grid=(T // tm,),
        in_specs=[row, row, row, row, row, wspec, wspec, wspec,
                  vec, _layer_spec(layer, (D_MODEL, D_FF)), _layer_spec(layer, (D_FF, D_MODEL)), vec],
        out_specs=row,
        out_shape=jax.ShapeDtypeStruct((T, D_MODEL), F32),
        compiler_params=pltpu.CompilerParams(
            dimension_semantics=("parallel",), vmem_limit_bytes=VMEM_LIMIT_BYTES),
        name="post",
    )(x2, hm, ha, ga, gb, wm, wa, wo, g, wu, wd, fg)


def _t5_bucket(n):
    max_exact = N_BUCKETS // 2
    n = np.maximum(n, 0)
    large = max_exact + (np.log(np.maximum(n, 1) / max_exact)
                         / np.log(MAX_DISTANCE / max_exact)
                         * (N_BUCKETS - max_exact)).astype(np.int32)
    large = np.minimum(large, N_BUCKETS - 1)
    return np.where(n < max_exact, n, large).astype(np.int32)


def _bucket_table():
    W = WINDOW
    dist = np.arange(W)[:, None] + W - np.arange(2 * W)[None, :]
    valid = (dist >= 0) & (dist < W)
    return jnp.asarray(np.where(valid, _t5_bucket(dist), -1).astype(np.int32))


def _regroup_kernel(w_ref, out_ref):
    n_gate = 2 * M_HEADS
    gate0 = sum(IN_SIZES[:3])
    tail0 = gate0 + n_gate
    n_tail = sum(IN_SIZES[5:])
    step = REGROUP_PIECE
    for src, dst, n, scale in ((0, 0, gate0, 1.0),
                               (tail0, gate0, A_Q_WIDTH, A_HEAD_DIM ** -0.5 * LOG2E),
                               (tail0 + A_Q_WIDTH, gate0 + A_Q_WIDTH, n_tail - A_Q_WIDTH, 1.0)):
        for o in range(0, n, step):
            piece = w_ref[src + o:src + o + step, :]
            if scale != 1.0:
                piece = piece * scale
            out_ref[:, dst + o:dst + o + step] = piece.T.astype(BF16)
    gate_rows = jnp.concatenate(
        [w_ref[gate0:tail0, :], jnp.zeros((128 - n_gate, w_ref.shape[1]), F32)], axis=0)
    out_ref[:, gate0 + n_tail:] = gate_rows.T.astype(BF16)


def _regroup_w_in(w_in):
    w_t = jnp.swapaxes(w_in, 1, 2)
    depth, n_in, d = w_t.shape
    tc = REGROUP_COLS
    return pl.pallas_call(
        _regroup_kernel,
        grid=(depth, d // tc),
        in_specs=[pl.BlockSpec((None, n_in, tc), lambda l, i: (l, 0, i))],
        out_specs=pl.BlockSpec((None, tc, _INPROJ_WIDTH), lambda l, i: (l, i, 0)),
        out_shape=jax.ShapeDtypeStruct((depth, d, _INPROJ_WIDTH), BF16),
        compiler_params=pltpu.CompilerParams(
            dimension_semantics=("parallel", "parallel"), vmem_limit_bytes=VMEM_LIMIT_BYTES),
        name="regroup",
    )(w_t)


def kernel(x, norm_mix_g, w_in, conv_w, conv_b, b_igate, b_fgate, mlstm_norm_g, attn_sinks,
           rel_bias, w_branch_m, w_branch_a, w_out, norm_mlp_g, w_up, w_down, final_norm_g):
    B, S, D = x.shape
    T = B * S
    bucket = _bucket_table()
    w_all = _regroup_w_in(w_in)
    wm, wa, wo, wu, wd = (w.astype(BF16) for w in (w_branch_m, w_branch_a, w_out, w_up, w_down))
    x2 = x.reshape(T, D)
    for l in range(DEPTH):
        mqk, mv, og, aq, ak, av, ga, gb, g_rows = _inproj(
            l, x2, norm_mix_g[l].reshape(1, D), conv_w[l], conv_b[l].reshape(1, -1), w_all, S)
        gate_b = jnp.concatenate([b_igate[l], b_fgate[l]]).reshape(2 * M_HEADS, 1)
        hm, ha = _mixers(mqk.reshape(B, S, -1), mv.reshape(B, S, -1), og.reshape(B, S, -1), g_rows,
                         gate_b, mlstm_norm_g[l].reshape(1, -1),
                         aq.reshape(B, S, -1), ak.reshape(B, S, -1), av.reshape(B, S, -1),
                         attn_sinks[l], rel_bias, bucket)
        x2 = _post(l, x2, hm.reshape(T, -1), ha.reshape(T, -1), ga, gb, wm, wa, wo,
                   norm_mlp_g[l].reshape(1, D), wu, wd,
                   final_norm_g.reshape(1, D), final=(l == DEPTH - 1))
    return x2.reshape(B, S, D)
```

```python
import functools

import numpy as np
import jax
import jax.numpy as jnp
from jax import lax
from jax.experimental import pallas as pl
from jax.experimental.pallas import tpu as pltpu

D_MODEL = 1024
DEPTH = 2
M_HEADS = 4
M_QK_DIM = 128
M_V_DIM = 256
M_QK_WIDTH = M_HEADS * M_QK_DIM
M_V_WIDTH = M_HEADS * M_V_DIM
CONV_K = 4
A_Q_HEADS = 16
A_KV_HEADS = 4
A_HEAD_DIM = 64
A_GROUP = A_Q_HEADS // A_KV_HEADS
A_Q_WIDTH = A_Q_HEADS * A_HEAD_DIM
A_KV_WIDTH = A_KV_HEADS * A_HEAD_DIM
WINDOW = 128
N_BUCKETS = 32
MAX_DISTANCE = 128
D_FF = 4 * D_MODEL
EPS = 1e-6
IN_SIZES = (2 * M_QK_WIDTH, M_V_WIDTH, M_V_WIDTH, M_HEADS, M_HEADS,
            A_Q_WIDTH, A_KV_WIDTH, A_KV_WIDTH, D_MODEL, D_MODEL)

INPROJ_TM = 512
POST_TM = 512
PROJ_SUB = 256
REGROUP_COLS = 256
REGROUP_PIECE = 512
MLSTM_CHUNK = 128
MLSTM_BLOCKS = 2
CONV_HALO = 8
SWA_BLOCKS = 2
NEG_BIG = -1e30
LOG2E = 1.4426950408889634
VMEM_LIMIT_BYTES = 56 * 1024 * 1024

F32 = jnp.float32
BF16 = jnp.bfloat16


def _const_spec(shape):
    nd = len(shape)
    return pl.BlockSpec(shape, lambda *_: (0,) * nd, pipeline_mode=pl.Buffered(1))


def _layer_spec(layer, shape):
    nd = len(shape)
    return pl.BlockSpec((None,) + tuple(shape), lambda *_: (layer,) + (0,) * nd,
                        pipeline_mode=pl.Buffered(1))


def _rmsnorm(x, g):
    return x * lax.rsqrt(jnp.mean(x * x, axis=-1, keepdims=True) + EPS) * g


_INPROJ_GROUPS = (("qk", 2 * M_QK_WIDTH), ("v", M_V_WIDTH), ("o", M_V_WIDTH),
                  ("aq", A_Q_WIDTH), ("ak", A_KV_WIDTH), ("av", A_KV_WIDTH),
                  ("ga", D_MODEL), ("gb", D_MODEL), ("g", 128))
_INPROJ_OFFS = dict(zip((n for n, _ in _INPROJ_GROUPS),
                        np.cumsum([0] + [w for _, w in _INPROJ_GROUPS])[:-1].tolist()))
_INPROJ_WIDTH = sum(w for _, w in _INPROJ_GROUPS)


def _inproj_kernel(x_ref, g_ref, cw_ref, cb_ref, w_ref, *refs, tiles_per_seq):
    out_refs, halo_ref = refs[:-1], refs[-1]

    @pl.when(pl.program_id(0) % tiles_per_seq == 0)
    def _():
        halo_ref[...] = jnp.zeros_like(halo_ref)

    outs = dict(zip((n for n, _ in _INPROJ_GROUPS), out_refs))
    sub = PROJ_SUB
    for t in range(x_ref.shape[0] // sub):
        _inproj_rows(slice(t * sub, (t + 1) * sub), x_ref, g_ref, cw_ref, cb_ref, w_ref, outs, halo_ref)


def _inproj_rows(rows, x_ref, g_ref, cw_ref, cb_ref, w_ref, outs, halo_ref):
    n_rows = rows.stop - rows.start
    h = _rmsnorm(x_ref[rows, :], g_ref[...]).astype(BF16)
    res = jnp.dot(h, w_ref[...], preferred_element_type=F32)
    group = lambda name, width: res[:, _INPROJ_OFFS[name]:_INPROJ_OFFS[name] + width]

    head_row = lax.broadcasted_iota(jnp.int32, (CONV_HALO, 1), 0)
    gw = 2 * M_QK_WIDTH // 4
    for c in range(4):
        cols = slice(c * gw, (c + 1) * gw)
        r = res[:, cols]
        prev = halo_ref[:, cols]
        halo_ref[:, cols] = r[n_rows - CONV_HALO:, :]
        y = cb_ref[:, cols] + r * cw_ref[CONV_K - 1:CONV_K, cols]
        for s in range(1, CONV_K):
            rolled = pltpu.roll(r, s, axis=0)
            head = jnp.where(head_row < s, pltpu.roll(prev, s, axis=0), rolled[:CONV_HALO])
            shifted = jnp.concatenate([head, rolled[CONV_HALO:]], axis=0)
            y = y + shifted * cw_ref[CONV_K - 1 - s:CONV_K - s, cols]
        y = y * jax.nn.sigmoid(y)
        if c < 2:
            y = y * (M_QK_DIM ** -0.5)
        outs["qk"][rows, cols] = y.astype(BF16)

    for name, width in _INPROJ_GROUPS[1:-1]:
        val = group(name, width)
        if name in ("o", "ga", "gb"):
            val = jax.nn.sigmoid(val)
        if name in ("ak", "av"):
            half = A_HEAD_DIM
            swapped = [pltpu.roll(val[:, c:c + 2 * half], half, axis=1) for c in range(0, width, 2 * half)]
            val = jnp.concatenate([val] + swapped, axis=1)
        outs[name][rows, :] = val.astype(BF16)
    outs["g"][:, rows] = group("g", 128).T[:2 * M_HEADS, :]


def _inproj(layer, x2, g, conv_w, conv_b, w_all, seq_len):
    T = x2.shape[0]
    tm = INPROJ_TM
    row = lambda n: pl.BlockSpec((tm, n), lambda i: (i, 0))
    out_widths = [2 * w if n in ("ak", "av") else w for n, w in _INPROJ_GROUPS[:-1]] + [2 * M_HEADS]
    return pl.pallas_call(
        functools.partial(_inproj_kernel, tiles_per_seq=seq_len // tm),
        grid=(T // tm,),
        in_specs=[row(D_MODEL), _const_spec((1, D_MODEL)), _const_spec((CONV_K, 2 * M_QK_WIDTH)),
                  _const_spec((1, 2 * M_QK_WIDTH)), _layer_spec(layer, (D_MODEL, _INPROJ_WIDTH))],
        out_specs=[row(n) for n in out_widths[:-1]] + [pl.BlockSpec((2 * M_HEADS, tm), lambda i: (0, i))],
        out_shape=([jax.ShapeDtypeStruct((T, n), BF16) for n in out_widths[:-1]]
                   + [jax.ShapeDtypeStruct((2 * M_HEADS, T), F32)]),
        scratch_shapes=[pltpu.VMEM((CONV_HALO, 2 * M_QK_WIDTH), F32)],
        compiler_params=pltpu.CompilerParams(
            dimension_semantics=("arbitrary",), vmem_limit_bytes=VMEM_LIMIT_BYTES),
        name="inproj",
    )(x2, g, conv_w, conv_b, w_all)


def _mlstm_init(c_ref, m_ref):
    @pl.when(pl.program_id(0) == 0)
    def _():
        c_ref[...] = jnp.zeros_like(c_ref)
        m_ref[...] = jnp.zeros_like(m_ref)


def _mlstm_gates(blk, grow_ref, gb_ref, m_ref):
    L = MLSTM_CHUNK
    rows = slice(blk * L, (blk + 1) * L)
    row_i = lax.broadcasted_iota(jnp.int32, (L, L), 0)
    col_j = lax.broadcasted_iota(jnp.int32, (L, L), 1)
    triu_f = (row_i <= col_j).astype(F32)
    g = grow_ref[:, rows] + gb_ref[...]
    logf = pltpu.roll(jax.nn.log_sigmoid(g), M_HEADS, axis=0) * LOG2E
    b = jnp.dot(logf, triu_f, preferred_element_type=F32,
                precision=lax.Precision.HIGHEST)
    a = g * LOG2E - b
    m_prev = m_ref[...]
    m_last = jnp.maximum(m_prev, jnp.max(a, axis=-1, keepdims=True))
    wk = jnp.exp2(a - m_last)
    decay = jnp.exp2(m_prev - m_last)
    m_ref[...] = jnp.sum(logf, axis=-1, keepdims=True) + m_last
    return a, logf, m_prev, wk, decay


def _mlstm_chunk(blk, seqs, ng_ref, between):
    L = MLSTM_CHUNK
    rows = slice(blk * L, (blk + 1) * L)
    row_i = lax.broadcasted_iota(jnp.int32, (L, L), 0)
    col_j = lax.broadcasted_iota(jnp.int32, (L, L), 1)
    tril = col_j <= row_i
    eye_b = jnp.where(col_j == row_i, 1.0, 0.0).astype(BF16)
    ones_b = jnp.ones((L, 128), BF16)
    mean_b = jnp.full((M_V_DIM, 128), 1.0 / M_V_DIM, BF16)
    nt = (((1,), (1,)), ((), ()))
    units = [(seq, h) for seq in seqs for h in range(M_HEADS)]
    n_units = range(len(units))

    qb, v_ext, s_raw, k_t, m_prev = [], [], [], [], []
    for u, ((gates, qk_ref, v_ref, _, _, _), h) in enumerate(units):
        qb.append(qk_ref[rows, h * M_QK_DIM:(h + 1) * M_QK_DIM])
        kb = qk_ref[rows, M_QK_WIDTH + h * M_QK_DIM:M_QK_WIDTH + (h + 1) * M_QK_DIM]
        v_ext.append(jnp.concatenate([v_ref[rows, h * M_V_DIM:(h + 1) * M_V_DIM], ones_b], axis=1))
        st = lax.dot_general(jnp.concatenate([qb[u], eye_b], axis=0), kb, nt,
                             preferred_element_type=F32)
        s_raw.append(st[:L])
        k_t.append(st[L:])
        m_prev.append(gates[2][h:h + 1, 0:1])
    between[0]()

    both, floor = [], []
    zeros_b = jnp.zeros((M_QK_DIM, M_QK_DIM), BF16)
    for u, ((gates, _, _, _, _, c_ref), h) in enumerate(units):
        a, logf, _, wk, decay_rows = gates
        a_mat = jnp.where(tril, a[h:h + 1, :], -jnp.inf)
        m_row = jnp.maximum(m_prev[u], jnp.max(a_mat, axis=-1, keepdims=True))
        b_col = jnp.sum(jnp.where(tril, logf[h:h + 1, :], 0.0), axis=-1, keepdims=True)
        floor.append(jnp.exp2(-(b_col + m_row)))
        s = s_raw[u] * jnp.exp2(a_mat - m_row)
        q_inter = (qb[u].astype(F32) * jnp.exp2(m_prev[u] - m_row)).astype(BF16)
        kw_t = (k_t[u] * wk[h:h + 1, :]).astype(BF16)
        c_prev = c_ref[h]
        lhs = jnp.concatenate([jnp.concatenate([s.astype(BF16), q_inter], axis=1),
                               jnp.concatenate([kw_t, zeros_b], axis=1)], axis=0)
        res = jnp.dot(lhs, jnp.concatenate([v_ext[u], c_prev.astype(BF16)], axis=0),
                      preferred_element_type=F32)
        both.append(res[:L])
        c_ref[h] = decay_rows[h:h + 1, 0:1] * c_prev + res[L:]
    between[1]()

    hh = []
    for u in n_units:
        inv = 1.0 / jnp.maximum(jnp.abs(both[u][:, M_V_DIM:]), floor[u])
        hh.append(both[u][:, :M_V_DIM] * jnp.concatenate([inv, inv], axis=1))
    msq_all = jnp.dot(jnp.concatenate([(x * x).astype(BF16) for x in hh], axis=0), mean_b,
                      preferred_element_type=F32)
    between[2]()

    for u, ((_, _, _, og_ref, out_ref, _), h) in enumerate(units):
        scale = lax.rsqrt(msq_all[u * L:(u + 1) * L] + EPS)
        sl = slice(h * M_V_DIM, (h + 1) * M_V_DIM)
        hn = hh[u] * jnp.concatenate([scale, scale], axis=1) * ng_ref[:, sl]
        out_ref[rows, sl] = hn.astype(BF16) * og_ref[rows, sl]
    between[3]()


def _swa_init(sink_ref, rel_ref, bucket_ref, bias_ref):
    W = WINDOW

    @pl.when(pl.program_id(0) == 0)
    def _():
        bucket = bucket_ref[...]
        neg = jnp.full((W, 2 * W), NEG_BIG, F32)
        for hq in range(A_Q_HEADS):
            bias_ref[1, hq] = neg

        def fill(bb, carry):
            hit = bucket == bb
            for hq in range(A_Q_HEADS):
                bias_ref[1, hq] = jnp.where(hit, rel_ref[bb, hq] * LOG2E, bias_ref[1, hq])
            return carry
        lax.fori_loop(0, N_BUCKETS, fill, 0)
        col = lax.broadcasted_iota(jnp.int32, (W, 2 * W), 1)
        for hq in range(A_Q_HEADS):
            sink = sink_ref[hq] * LOG2E
            bias_ref[0, hq] = jnp.where(col == 0, sink, jnp.where(col < W, NEG_BIG, bias_ref[1, hq]))
            bias_ref[1, hq] = jnp.where(col == 0, sink, bias_ref[1, hq])


def _swa_kv_head(blk, kv, q_ref, kp_ref, kc_ref, vp_ref, vc_ref, out_ref, bias_ref):
    W = WINDOW
    lane_lo = lax.broadcasted_iota(jnp.int32, (2 * W, 2 * A_HEAD_DIM), 1) < A_HEAD_DIM
    not_sink = lax.broadcasted_iota(jnp.int32, (2 * W, 2 * A_HEAD_DIM), 0) > 0
    keep_lo = lane_lo & not_sink
    keep_hi = jnp.logical_not(lane_lo) & not_sink
    zero = jnp.zeros((), BF16)
    ones_lo = jnp.where(lane_lo, 1.0, 0.0).astype(BF16)
    ones_hi = jnp.where(lane_lo, 0.0, 1.0).astype(BF16)
    nt = (((1,), (1,)), ((), ()))
    rows = slice(blk * W, (blk + 1) * W)
    var = jnp.minimum(pl.program_id(0), 1) if blk == 0 else 1
    def tile(ref, rws, swapped):
        c0 = (A_KV_WIDTH if swapped else 0) + (kv // 2) * 2 * A_HEAD_DIM
        return ref[rws, c0:c0 + 2 * A_HEAD_DIM]

    def prev_cur(p_ref, c_ref, swapped):
        prev = (tile(p_ref, slice(None), swapped) if blk == 0
                else tile(c_ref, slice((blk - 1) * W, blk * W), swapped))
        return jnp.concatenate([prev, tile(c_ref, rows, swapped)], axis=0)

    in_low = kv % 2 == 1
    k_even = jnp.where(keep_lo, prev_cur(kp_ref, kc_ref, in_low), zero)
    k_odd = jnp.where(keep_hi, prev_cur(kp_ref, kc_ref, not in_low), zero)
    v_even = jnp.concatenate([jnp.where(keep_lo, prev_cur(vp_ref, vc_ref, in_low), zero), ones_lo], axis=1)
    v_odd = jnp.concatenate([jnp.where(keep_hi, prev_cur(vp_ref, vc_ref, not in_low), zero), ones_hi], axis=1)
    v_both = jnp.concatenate([v_even, v_odd], axis=0)
    pairs = [kv * (A_GROUP // 2) + p for p in range(A_GROUP // 2)]
    cols = [slice(pi * 2 * A_HEAD_DIM, (pi + 1) * 2 * A_HEAD_DIM) for pi in pairs]
    q_st = jnp.concatenate([q_ref[rows, c] for c in cols], axis=0)
    bias_e = jnp.concatenate([bias_ref[var, 2 * pi] for pi in pairs], axis=0)
    bias_o = jnp.concatenate([bias_ref[var, 2 * pi + 1] for pi in pairs], axis=0)
    s_e = lax.dot_general(q_st, k_even, nt, preferred_element_type=F32) + bias_e
    s_o = lax.dot_general(q_st, k_odd, nt, preferred_element_type=F32) + bias_o
    p_e = jnp.exp2(s_e - jnp.max(s_e, axis=-1, keepdims=True)).astype(BF16)
    p_o = jnp.exp2(s_o - jnp.max(s_o, axis=-1, keepdims=True)).astype(BF16)
    acc = jnp.dot(jnp.concatenate([p_e, p_o], axis=1), v_both,
                  preferred_element_type=F32)
    out = (acc[:, :2 * A_HEAD_DIM] / acc[:, 2 * A_HEAD_DIM:]).astype(out_ref.dtype)
    for p, c in enumerate(cols):
        out_ref[rows, c] = out[p * W:(p + 1) * W]


def _mixers_kernel(sink_ref, rel_ref, bucket_ref, qk_ref, v_ref, og_ref, gb_ref, ng_ref,
                   q_ref, kp_ref, kc_ref, vp_ref, vc_ref, *refs):
    n_seq = qk_ref.shape[0]
    grow_refs = refs[:n_seq]
    hm_ref, ha_ref, c_ref, m_ref, bias_ref = refs[n_seq:]
    _mlstm_init(c_ref, m_ref)
    _swa_init(sink_ref, rel_ref, bucket_ref, bias_ref)
    for blk in range(SWA_BLOCKS):
        seqs = [(_mlstm_gates(blk, grow_refs[b], gb_ref, m_ref.at[b]),
                 qk_ref.at[b], v_ref.at[b], og_ref.at[b], hm_ref.at[b], c_ref.at[b])
                for b in range(n_seq)]
        swa = [functools.partial(_swa_kv_head, blk, u, q_ref.at[b], kp_ref.at[b], kc_ref.at[b],
                                 vp_ref.at[b], vc_ref.at[b], ha_ref.at[b], bias_ref)
               for b in range(n_seq) for u in range(A_KV_HEADS)]
        per_gap = len(swa) // 4
        _mlstm_chunk(blk, seqs, ng_ref,
                     between=[lambda g=g: [f() for f in swa[g * per_gap:(g + 1) * per_gap]]
                              for g in range(4)])


def _mixers(mqk, mv, og, g_rows, gate_b, norm_g, aq, ak2, av2, sinks, rel_bias, bucket):
    B, S, _ = aq.shape
    W = WINDOW
    tq = SWA_BLOCKS * W
    assert tq == MLSTM_BLOCKS * MLSTM_CHUNK
    kvw = 2 * A_KV_WIDTH
    cur = lambda n: pl.BlockSpec((B, tq, n), lambda i: (0, i, 0))
    prev = pl.BlockSpec((B, W, kvw), lambda i: (0, jnp.maximum(i * SWA_BLOCKS - 1, 0), 0))
    gate_rows = [pl.BlockSpec((2 * M_HEADS, tq), lambda i, b=b: (0, b * (S // tq) + i)) for b in range(B)]
    smem = pl.BlockSpec(memory_space=pltpu.SMEM)
    return pl.pallas_call(
        _mixers_kernel,
        grid=(S // tq,),
        in_specs=[smem, smem, _const_spec((W, 2 * W)),
                  cur(2 * M_QK_WIDTH), cur(M_V_WIDTH), cur(M_V_WIDTH),
                  _const_spec((2 * M_HEADS, 1)), _const_spec((1, M_V_WIDTH)),
                  cur(A_Q_WIDTH), prev, cur(kvw), prev, cur(kvw)] + gate_rows,
        out_specs=[cur(M_V_WIDTH), cur(A_Q_WIDTH)],
        out_shape=[jax.ShapeDtypeStruct((B, S, M_V_WIDTH), BF16),
                   jax.ShapeDtypeStruct((B, S, A_Q_WIDTH), BF16)],
        scratch_shapes=[pltpu.VMEM((B, M_HEADS, M_QK_DIM, M_V_DIM + 128), F32),
                        pltpu.VMEM((B, 2 * M_HEADS, MLSTM_CHUNK), F32),
                        pltpu.VMEM((2, A_Q_HEADS, W, 2 * W), F32)],
        compiler_params=pltpu.CompilerParams(
            dimension_semantics=("arbitrary",), vmem_limit_bytes=VMEM_LIMIT_BYTES),
        name="mixers",
    )(sinks, rel_bias, bucket, mqk, mv, og, gate_b, norm_g, aq, ak2, ak2, av2, av2,
      *([g_rows] * B))


def _post_kernel(x_ref, hm_ref, ha_ref, ga_ref, gb_ref, wm_ref, wa_ref, wo_ref,
                 g_ref, wu_ref, wd_ref, fg_ref, out_ref, *, final):
    sub = PROJ_SUB
    for t in range(x_ref.shape[0] // sub):
        rows = slice(t * sub, (t + 1) * sub)
        y = (ga_ref[rows, :] * jnp.dot(hm_ref[rows, :], wm_ref[...], preferred_element_type=F32)
             + gb_ref[rows, :] * jnp.dot(ha_ref[rows, :], wa_ref[...], preferred_element_type=F32))
        x = x_ref[rows, :] + jnp.dot(y.astype(BF16), wo_ref[...], preferred_element_type=F32)
        h = _rmsnorm(x, g_ref[...]).astype(BF16)
        u = jnp.maximum(jnp.dot(h, wu_ref[...], preferred_element_type=F32), 0.0)
        x = x + jnp.dot((u * u).astype(BF16), wd_ref[...], preferred_element_type=F32)
        if final:
            x = _rmsnorm(x, fg_ref[...])
        out_ref[rows, :] = x


def _post(layer, x2, hm, ha, ga, gb, wm, wa, wo, g, wu, wd, fg, final):
    T = x2.shape[0]
    tm = POST_TM
    row = pl.BlockSpec((tm, D_MODEL), lambda i: (i, 0))
    wspec = _layer_spec(layer, (D_MODEL, D_MODEL))
    vec = _const_spec((1, D_MODEL))
    return pl.pallas_call(
        functools.partial(_post_kernel, final=final),
        grid=(T // tm,),
        in_specs=[row, row, row, row, row, wspec, wspec, wspec,
                  vec, _layer_spec(layer, (D_MODEL, D_FF)), _layer_spec(layer, (D_FF, D_MODEL)), vec],
        out_specs=row,
        out_shape=jax.ShapeDtypeStruct((T, D_MODEL), F32),
        compiler_params=pltpu.CompilerParams(
            dimension_semantics=("parallel",), vmem_limit_bytes=VMEM_LIMIT_BYTES),
        name="post",
    )(x2, hm, ha, ga, gb, wm, wa, wo, g, wu, wd, fg)


def _t5_bucket(n):
    max_exact = N_BUCKETS // 2
    n = np.maximum(n, 0)
    large = max_exact + (np.log(np.maximum(n, 1) / max_exact)
                         / np.log(MAX_DISTANCE / max_exact)
                         * (N_BUCKETS - max_exact)).astype(np.int32)
    large = np.minimum(large, N_BUCKETS - 1)
    return np.where(n < max_exact, n, large).astype(np.int32)


def _bucket_table():
    W = WINDOW
    dist = np.arange(W)[:, None] + W - np.arange(2 * W)[None, :]
    valid = (dist >= 0) & (dist < W)
    return jnp.asarray(np.where(valid, _t5_bucket(dist), -1).astype(np.int32))


def _regroup_kernel(w_ref, out_ref):
    n_gate = 2 * M_HEADS
    gate0 = sum(IN_SIZES[:3])
    tail0 = gate0 + n_gate
    n_tail = sum(IN_SIZES[5:])
    step = REGROUP_PIECE
    for src, dst, n, scale in ((0, 0, gate0, 1.0),
                               (tail0, gate0, A_Q_WIDTH, A_HEAD_DIM ** -0.5 * LOG2E),
                               (tail0 + A_Q_WIDTH, gate0 + A_Q_WIDTH, n_tail - A_Q_WIDTH, 1.0)):
        for o in range(0, n, step):
            piece = w_ref[src + o:src + o + step, :]
            if scale != 1.0:
                piece = piece * scale
            out_ref[:, dst + o:dst + o + step] = piece.T.astype(BF16)
    gate_rows = jnp.concatenate(
        [w_ref[gate0:tail0, :], jnp.zeros((128 - n_gate, w_ref.shape[1]), F32)], axis=0)
    out_ref[:, gate0 + n_tail:] = gate_rows.T.astype(BF16)


def _regroup_w_in(w_in):
    w_t = jnp.swapaxes(w_in, 1, 2)
    depth, n_in, d = w_t.shape
    tc = REGROUP_COLS
    return pl.pallas_call(
        _regroup_kernel,
        grid=(depth, d // tc),
        in_specs=[pl.BlockSpec((None, n_in, tc), lambda l, i: (l, 0, i))],
        out_specs=pl.BlockSpec((None, tc, _INPROJ_WIDTH), lambda l, i: (l, i, 0)),
        out_shape=jax.ShapeDtypeStruct((depth, d, _INPROJ_WIDTH), BF16),
        compiler_params=pltpu.CompilerParams(
            dimension_semantics=("parallel", "parallel"), vmem_limit_bytes=VMEM_LIMIT_BYTES),
        name="regroup",
    )(w_t)


def kernel(x, norm_mix_g, w_in, conv_w, conv_b, b_igate, b_fgate, mlstm_norm_g, attn_sinks,
           rel_bias, w_branch_m, w_branch_a, w_out, norm_mlp_g, w_up, w_down, final_norm_g):
    B, S, D = x.shape
    T = B * S
    bucket = _bucket_table()
    w_all = _regroup_w_in(w_in)
    wm, wa, wo, wu, wd = (w.astype(BF16) for w in (w_branch_m, w_branch_a, w_out, w_up, w_down))
    x2 = x.reshape(T, D)
    for l in range(DEPTH):
        mqk, mv, og, aq, ak, av, ga, gb, g_rows = _inproj(
            l, x2, norm_mix_g[l].reshape(1, D), conv_w[l], conv_b[l].reshape(1, -1), w_all, S)
        gate_b = jnp.concatenate([b_igate[l], b_fgate[l]]).reshape(2 * M_HEADS, 1)
        hm, ha = _mixers(mqk.reshape(B, S, -1), mv.reshape(B, S, -1), og.reshape(B, S, -1), g_rows,
                         gate_b, mlstm_norm_g[l].reshape(1, -1),
                         aq.reshape(B, S, -1), ak.reshape(B, S, -1), av.reshape(B, S, -1),
                         attn_sinks[l], rel_bias, bucket)
        x2 = _post(l, x2, hm.reshape(T, -1), ha.reshape(T, -1), ga, gb, wm, wa, wo,
                   norm_mlp_g[l].reshape(1, D), wu, wd,
                   final_norm_g.reshape(1, D), final=(l == DEPTH - 1))
    return x2.reshape(B, S, D)
```

```python
import functools

import numpy as np
import jax
import jax.numpy as jnp
from jax import lax
from jax.experimental import pallas as pl
from jax.experimental.pallas import tpu as pltpu

D_MODEL = 1024
DEPTH = 2
M_HEADS = 4
M_QK_DIM = 128
M_V_DIM = 256
M_QK_WIDTH = M_HEADS * M_QK_DIM
M_V_WIDTH = M_HEADS * M_V_DIM
CONV_K = 4
A_Q_HEADS = 16
A_KV_HEADS = 4
A_HEAD_DIM = 64
A_GROUP = A_Q_HEADS // A_KV_HEADS
A_Q_WIDTH = A_Q_HEADS * A_HEAD_DIM
A_KV_WIDTH = A_KV_HEADS * A_HEAD_DIM
WINDOW = 128
N_BUCKETS = 32
MAX_DISTANCE = 128
D_FF = 4 * D_MODEL
EPS = 1e-6
IN_SIZES = (2 * M_QK_WIDTH, M_V_WIDTH, M_V_WIDTH, M_HEADS, M_HEADS,
            A_Q_WIDTH, A_KV_WIDTH, A_KV_WIDTH, D_MODEL, D_MODEL)

INPROJ_TM = 512
POST_TM = 512
PROJ_SUB = 256
REGROUP_COLS = 256
REGROUP_PIECE = 512
MLSTM_CHUNK = 128
MLSTM_BLOCKS = 2
CONV_HALO = 8
SWA_BLOCKS = 2
NEG_BIG = -1e30
LOG2E = 1.4426950408889634
VMEM_LIMIT_BYTES = 56 * 1024 * 1024

F32 = jnp.float32
BF16 = jnp.bfloat16


def _const_spec(shape):
    nd = len(shape)
    return pl.BlockSpec(shape, lambda *_: (0,) * nd, pipeline_mode=pl.Buffered(1))


def _layer_spec(layer, shape):
    nd = len(shape)
    return pl.BlockSpec((None,) + tuple(shape), lambda *_: (layer,) + (0,) * nd,
                        pipeline_mode=pl.Buffered(1))


def _rmsnorm(x, g):
    return x * lax.rsqrt(jnp.mean(x * x, axis=-1, keepdims=True) + EPS) * g


_INPROJ_GROUPS = (("qk", 2 * M_QK_WIDTH), ("v", M_V_WIDTH), ("o", M_V_WIDTH),
                  ("aq", A_Q_WIDTH), ("ak", A_KV_WIDTH), ("av", A_KV_WIDTH),
                  ("ga", D_MODEL), ("gb", D_MODEL), ("g", 128))
_INPROJ_OFFS = dict(zip((n for n, _ in _INPROJ_GROUPS),
                        np.cumsum([0] + [w for _, w in _INPROJ_GROUPS])[:-1].tolist()))
_INPROJ_WIDTH = sum(w for _, w in _INPROJ_GROUPS)


def _inproj_kernel(x_ref, g_ref, cw_ref, cb_ref, w_ref, *refs, tiles_per_seq):
    out_refs, halo_ref = refs[:-1], refs[-1]

    @pl.when(pl.program_id(0) % tiles_per_seq == 0)
    def _():
        halo_ref[...] = jnp.zeros_like(halo_ref)

    outs = dict(zip((n for n, _ in _INPROJ_GROUPS), out_refs))
    sub = PROJ_SUB
    for t in range(x_ref.shape[0] // sub):
        _inproj_rows(slice(t * sub, (t + 1) * sub), x_ref, g_ref, cw_ref, cb_ref, w_ref, outs, halo_ref)


def _inproj_rows(rows, x_ref, g_ref, cw_ref, cb_ref, w_ref, outs, halo_ref):
    n_rows = rows.stop - rows.start
    h = _rmsnorm(x_ref[rows, :], g_ref[...]).astype(BF16)
    res = jnp.dot(h, w_ref[...], preferred_element_type=F32)
    group = lambda name, width: res[:, _INPROJ_OFFS[name]:_INPROJ_OFFS[name] + width]

    head_row = lax.broadcasted_iota(jnp.int32, (CONV_HALO, 1), 0)
    gw = 2 * M_QK_WIDTH // 4
    for c in range(4):
        cols = slice(c * gw, (c + 1) * gw)
        r = res[:, cols]
        prev = halo_ref[:, cols]
        halo_ref[:, cols] = r[n_rows - CONV_HALO:, :]
        y = cb_ref[:, cols] + r * cw_ref[CONV_K - 1:CONV_K, cols]
        for s in range(1, CONV_K):
            rolled = pltpu.roll(r, s, axis=0)
            head = jnp.where(head_row < s, pltpu.roll(prev, s, axis=0), rolled[:CONV_HALO])
            shifted = jnp.concatenate([head, rolled[CONV_HALO:]], axis=0)
            y = y + shifted * cw_ref[CONV_K - 1 - s:CONV_K - s, cols]
        y = y * jax.nn.sigmoid(y)
        if c < 2:
            y = y * (M_QK_DIM ** -0.5)
        outs["qk"][rows, cols] = y.astype(BF16)

    for name, width in _INPROJ_GROUPS[1:-1]:
        val = group(name, width)
        if name in ("o", "ga", "gb"):
            val = jax.nn.sigmoid(val)
        if name in ("ak", "av"):
            half = A_HEAD_DIM
            swapped = [pltpu.roll(val[:, c:c + 2 * half], half, axis=1) for c in range(0, width, 2 * half)]
            val = jnp.concatenate([val] + swapped, axis=1)
        outs[name][rows, :] = val.astype(BF16)
    outs["g"][:, rows] = group("g", 128).T[:2 * M_HEADS, :]


def _inproj(layer, x2, g, conv_w, conv_b, w_all, seq_len):
    T = x2.shape[0]
    tm = INPROJ_TM
    row = lambda n: pl.BlockSpec((tm, n), lambda i: (i, 0))
    out_widths = [2 * w if n in ("ak", "av") else w for n, w in _INPROJ_GROUPS[:-1]] + [2 * M_HEADS]
    return pl.pallas_call(
        functools.partial(_inproj_kernel, tiles_per_seq=seq_len // tm),
        grid=(T // tm,),
        in_specs=[row(D_MODEL), _const_spec((1, D_MODEL)), _const_spec((CONV_K, 2 * M_QK_WIDTH)),
                  _const_spec((1, 2 * M_QK_WIDTH)), _layer_spec(layer, (D_MODEL, _INPROJ_WIDTH))],
        out_specs=[row(n) for n in out_widths[:-1]] + [pl.BlockSpec((2 * M_HEADS, tm), lambda i: (0, i))],
        out_shape=([jax.ShapeDtypeStruct((T, n), BF16) for n in out_widths[:-1]]
                   + [jax.ShapeDtypeStruct((2 * M_HEADS, T), F32)]),
        scratch_shapes=[pltpu.VMEM((CONV_HALO, 2 * M_QK_WIDTH), F32)],
        compiler_params=pltpu.CompilerParams(
            dimension_semantics=("arbitrary",), vmem_limit_bytes=VMEM_LIMIT_BYTES),
        name="inproj",
    )(x2, g, conv_w, conv_b, w_all)


def _mlstm_init(c_ref, m_ref):
    @pl.when(pl.program_id(0) == 0)
    def _():
        c_ref[...] = jnp.zeros_like(c_ref)
        m_ref[...] = jnp.zeros_like(m_ref)


def _mlstm_gates(blk, grow_ref, gb_ref, m_ref):
    L = MLSTM_CHUNK
    rows = slice(blk * L, (blk + 1) * L)
    row_i = lax.broadcasted_iota(jnp.int32, (L, L), 0)
    col_j = lax.broadcasted_iota(jnp.int32, (L, L), 1)
    triu_f = (row_i <= col_j).astype(F32)
    g = grow_ref[:, rows] + gb_ref[...]
    logf = pltpu.roll(jax.nn.log_sigmoid(g), M_HEADS, axis=0) * LOG2E
    b = jnp.dot(logf, triu_f, preferred_element_type=F32,
                precision=lax.Precision.HIGHEST)
    a = g * LOG2E - b
    m_prev = m_ref[...]
    m_last = jnp.maximum(m_prev, jnp.max(a, axis=-1, keepdims=True))
    wk = jnp.exp2(a - m_last)
    decay = jnp.exp2(m_prev - m_last)
    m_ref[...] = jnp.sum(logf, axis=-1, keepdims=True) + m_last
    return a, logf, m_prev, wk, decay


def _mlstm_chunk(blk, seqs, ng_ref, between):
    L = MLSTM_CHUNK
    rows = slice(blk * L, (blk + 1) * L)
    row_i = lax.broadcasted_iota(jnp.int32, (L, L), 0)
    col_j = lax.broadcasted_iota(jnp.int32, (L, L), 1)
    tril = col_j <= row_i
    eye_b = jnp.where(col_j == row_i, 1.0, 0.0).astype(BF16)
    ones_b = jnp.ones((L, 128), BF16)
    mean_b = jnp.full((M_V_DIM, 128), 1.0 / M_V_DIM, BF16)
    nt = (((1,), (1,)), ((), ()))
    units = [(seq, h) for seq in seqs for h in range(M_HEADS)]
    n_units = range(len(units))

    qb, v_ext, s_raw, k_t, m_prev = [], [], [], [], []
    for u, ((gates, qk_ref, v_ref, _, _, _), h) in enumerate(units):
        qb.append(qk_ref[rows, h * M_QK_DIM:(h + 1) * M_QK_DIM])
        kb = qk_ref[rows, M_QK_WIDTH + h * M_QK_DIM:M_QK_WIDTH + (h + 1) * M_QK_DIM]
        v_ext.append(jnp.concatenate([v_ref[rows, h * M_V_DIM:(h + 1) * M_V_DIM], ones_b], axis=1))
        st = lax.dot_general(jnp.concatenate([qb[u], eye_b], axis=0), kb, nt,
                             preferred_element_type=F32)
        s_raw.append(st[:L])
        k_t.append(st[L:])
        m_prev.append(gates[2][h:h + 1, 0:1])
    between[0]()

    both, floor = [], []
    zeros_b = jnp.zeros((M_QK_DIM, M_QK_DIM), BF16)
    for u, ((gates, _, _, _, _, c_ref), h) in enumerate(units):
        a, logf, _, wk, decay_rows = gates
        a_mat = jnp.where(tril, a[h:h + 1, :], -jnp.inf)
        m_row = jnp.maximum(m_prev[u], jnp.max(a_mat, axis=-1, keepdims=True))
        b_col = jnp.sum(jnp.where(tril, logf[h:h + 1, :], 0.0), axis=-1, keepdims=True)
        floor.append(jnp.exp2(-(b_col + m_row)))
        s = s_raw[u] * jnp.exp2(a_mat - m_row)
        q_inter = (qb[u].astype(F32) * jnp.exp2(m_prev[u] - m_row)).astype(BF16)
        kw_t = (k_t[u] * wk[h:h + 1, :]).astype(BF16)
        c_prev = c_ref[h]
        lhs = jnp.concatenate([jnp.concatenate([s.astype(BF16), q_inter], axis=1),
                               jnp.concatenate([kw_t, zeros_b], axis=1)], axis=0)
        res = jnp.dot(lhs, jnp.concatenate([v_ext[u], c_prev.astype(BF16)], axis=0),
                      preferred_element_type=F32)
        both.append(res[:L])
        c_ref[h] = decay_rows[h:h + 1, 0:1] * c_prev + res[L:]
    between[1]()

    hh = []
    for u in n_units:
        inv = 1.0 / jnp.maximum(jnp.abs(both[u][:, M_V_DIM:]), floor[u])
        hh.append(both[u][:, :M_V_DIM] * jnp.concatenate([inv, inv], axis=1))
    msq = [jnp.mean(x * x, axis=-1, keepdims=True) for x in hh]
    between[2]()

    for u, ((_, _, _, og_ref, out_ref, _), h) in enumerate(units):
        sl = slice(h * M_V_DIM, (h + 1) * M_V_DIM)
        hn = hh[u] * lax.rsqrt(msq[u] + EPS) * ng_ref[:, sl]
        out_ref[rows, sl] = hn.astype(BF16) * og_ref[rows, sl]
    between[3]()


def _swa_init(sink_ref, rel_ref, bucket_ref, bias_ref):
    W = WINDOW

    @pl.when(pl.program_id(0) == 0)
    def _():
        bucket = bucket_ref[...]
        neg = jnp.full((W, 2 * W), NEG_BIG, F32)
        for hq in range(A_Q_HEADS):
            bias_ref[1, hq] = neg

        def fill(bb, carry):
            hit = bucket == bb
            for hq in range(A_Q_HEADS):
                bias_ref[1, hq] = jnp.where(hit, rel_ref[bb, hq] * LOG2E, bias_ref[1, hq])
            return carry
        lax.fori_loop(0, N_BUCKETS, fill, 0)
        col = lax.broadcasted_iota(jnp.int32, (W, 2 * W), 1)
        for hq in range(A_Q_HEADS):
            sink = sink_ref[hq] * LOG2E
            bias_ref[0, hq] = jnp.where(col == 0, sink, jnp.where(col < W, NEG_BIG, bias_ref[1, hq]))
            bias_ref[1, hq] = jnp.where(col == 0, sink, bias_ref[1, hq])


def _swa_kv_head(blk, kv, q_ref, kp_ref, kc_ref, vp_ref, vc_ref, out_ref, bias_ref):
    W = WINDOW
    lane_lo = lax.broadcasted_iota(jnp.int32, (2 * W, 2 * A_HEAD_DIM), 1) < A_HEAD_DIM
    not_sink = lax.broadcasted_iota(jnp.int32, (2 * W, 2 * A_HEAD_DIM), 0) > 0
    keep_lo = lane_lo & not_sink
    keep_hi = jnp.logical_not(lane_lo) & not_sink
    zero = jnp.zeros((), BF16)
    ones_lo = jnp.where(lane_lo, 1.0, 0.0).astype(BF16)
    ones_hi = jnp.where(lane_lo, 0.0, 1.0).astype(BF16)
    nt = (((1,), (1,)), ((), ()))
    rows = slice(blk * W, (blk + 1) * W)
    var = jnp.minimum(pl.program_id(0), 1) if blk == 0 else 1
    def tile(ref, rws, swapped):
        c0 = (A_KV_WIDTH if swapped else 0) + (kv // 2) * 2 * A_HEAD_DIM
        return ref[rws, c0:c0 + 2 * A_HEAD_DIM]

    def prev_cur(p_ref, c_ref, swapped):
        prev = (tile(p_ref, slice(None), swapped) if blk == 0
                else tile(c_ref, slice((blk - 1) * W, blk * W), swapped))
        return jnp.concatenate([prev, tile(c_ref, rows, swapped)], axis=0)

    in_low = kv % 2 == 1
    k_even = jnp.where(keep_lo, prev_cur(kp_ref, kc_ref, in_low), zero)
    k_odd = jnp.where(keep_hi, prev_cur(kp_ref, kc_ref, not in_low), zero)
    v_even = jnp.concatenate([jnp.where(keep_lo, prev_cur(vp_ref, vc_ref, in_low), zero), ones_lo], axis=1)
    v_odd = jnp.concatenate([jnp.where(keep_hi, prev_cur(vp_ref, vc_ref, not in_low), zero), ones_hi], axis=1)
    v_both = jnp.concatenate([v_even, v_odd], axis=0)
    pairs = [kv * (A_GROUP // 2) + p for p in range(A_GROUP // 2)]
    cols = [slice(pi * 2 * A_HEAD_DIM, (pi + 1) * 2 * A_HEAD_DIM) for pi in pairs]
    q_st = jnp.concatenate([q_ref[rows, c] for c in cols], axis=0)
    bias_e = jnp.concatenate([bias_ref[var, 2 * pi] for pi in pairs], axis=0)
    bias_o = jnp.concatenate([bias_ref[var, 2 * pi + 1] for pi in pairs], axis=0)
    s_e = lax.dot_general(q_st, k_even, nt, preferred_element_type=F32) + bias_e
    s_o = lax.dot_general(q_st, k_odd, nt, preferred_element_type=F32) + bias_o
    p_e = jnp.exp2(s_e - jnp.max(s_e, axis=-1, keepdims=True)).astype(BF16)
    p_o = jnp.exp2(s_o - jnp.max(s_o, axis=-1, keepdims=True)).astype(BF16)
    acc = jnp.dot(jnp.concatenate([p_e, p_o], axis=1), v_both,
                  preferred_element_type=F32)
    out = (acc[:, :2 * A_HEAD_DIM] / acc[:, 2 * A_HEAD_DIM:]).astype(out_ref.dtype)
    for p, c in enumerate(cols):
        out_ref[rows, c] = out[p * W:(p + 1) * W]


def _mixers_kernel(sink_ref, rel_ref, bucket_ref, qk_ref, v_ref, og_ref, gb_ref, ng_ref,
                   q_ref, kp_ref, kc_ref, vp_ref, vc_ref, *refs):
    n_seq = qk_ref.shape[0]
    grow_refs = refs[:n_seq]
    hm_ref, ha_ref, c_ref, m_ref, bias_ref = refs[n_seq:]
    _mlstm_init(c_ref, m_ref)
    _swa_init(sink_ref, rel_ref, bucket_ref, bias_ref)
    for blk in range(SWA_BLOCKS):
        seqs = [(_mlstm_gates(blk, grow_refs[b], gb_ref, m_ref.at[b]),
                 qk_ref.at[b], v_ref.at[b], og_ref.at[b], hm_ref.at[b], c_ref.at[b])
                for b in range(n_seq)]
        swa = [functools.partial(_swa_kv_head, blk, u, q_ref.at[b], kp_ref.at[b], kc_ref.at[b],
                                 vp_ref.at[b], vc_ref.at[b], ha_ref.at[b], bias_ref)
               for b in range(n_seq) for u in range(A_KV_HEADS)]
        per_gap = len(swa) // 4
        _mlstm_chunk(blk, seqs, ng_ref,
                     between=[lambda g=g: [f() for f in swa[g * per_gap:(g + 1) * per_gap]]
                              for g in range(4)])


def _mixers(mqk, mv, og, g_rows, gate_b, norm_g, aq, ak2, av2, sinks, rel_bias, bucket):
    B, S, _ = aq.shape
    W = WINDOW
    tq = SWA_BLOCKS * W
    assert tq == MLSTM_BLOCKS * MLSTM_CHUNK
    kvw = 2 * A_KV_WIDTH
    cur = lambda n: pl.BlockSpec((B, tq, n), lambda i: (0, i, 0))
    prev = pl.BlockSpec((B, W, kvw), lambda i: (0, jnp.maximum(i * SWA_BLOCKS - 1, 0), 0))
    gate_rows = [pl.BlockSpec((2 * M_HEADS, tq), lambda i, b=b: (0, b * (S // tq) + i)) for b in range(B)]
    smem = pl.BlockSpec(memory_space=pltpu.SMEM)
    return pl.pallas_call(
        _mixers_kernel,
        grid=(S // tq,),
        in_specs=[smem, smem, _const_spec((W, 2 * W)),
                  cur(2 * M_QK_WIDTH), cur(M_V_WIDTH), cur(M_V_WIDTH),
                  _const_spec((2 * M_HEADS, 1)), _const_spec((1, M_V_WIDTH)),
                  cur(A_Q_WIDTH), prev, cur(kvw), prev, cur(kvw)] + gate_rows,
        out_specs=[cur(M_V_WIDTH), cur(A_Q_WIDTH)],
        out_shape=[jax.ShapeDtypeStruct((B, S, M_V_WIDTH), BF16),
                   jax.ShapeDtypeStruct((B, S, A_Q_WIDTH), BF16)],
        scratch_shapes=[pltpu.VMEM((B, M_HEADS, M_QK_DIM, M_V_DIM + 128), F32),
                        pltpu.VMEM((B, 2 * M_HEADS, MLSTM_CHUNK), F32),
                        pltpu.VMEM((2, A_Q_HEADS, W, 2 * W), F32)],
        compiler_params=pltpu.CompilerParams(
            dimension_semantics=("arbitrary",), vmem_limit_bytes=VMEM_LIMIT_BYTES),
        name="mixers",
    )(sinks, rel_bias, bucket, mqk, mv, og, gate_b, norm_g, aq, ak2, ak2, av2, av2,
      *([g_rows] * B))


def _post_kernel(x_ref, hm_ref, ha_ref, ga_ref, gb_ref, wm_ref, wa_ref, wo_ref,
                 g_ref, wu_ref, wd_ref, fg_ref, out_ref, *, final):
    sub = PROJ_SUB
    for t in range(x_ref.shape[0] // sub):
        rows = slice(t * sub, (t + 1) * sub)
        y = (ga_ref[rows, :] * jnp.dot(hm_ref[rows, :], wm_ref[...], preferred_element_type=F32)
             + gb_ref[rows, :] * jnp.dot(ha_ref[rows, :], wa_ref[...], preferred_element_type=F32))
        x = x_ref[rows, :] + jnp.dot(y.astype(BF16), wo_ref[...], preferred_element_type=F32)
        h = _rmsnorm(x, g_ref[...]).astype(BF16)
        u = jnp.maximum(jnp.dot(h, wu_ref[...], preferred_element_type=F32), 0.0)
        x = x + jnp.dot((u * u).astype(BF16), wd_ref[...], preferred_element_type=F32)
        if final:
            x = _rmsnorm(x, fg_ref[...])
        out_ref[rows, :] = x


def _post(layer, x2, hm, ha, ga, gb, wm, wa, wo, g, wu, wd, fg, final):
    T = x2.shape[0]
    tm = POST_TM
    row = pl.BlockSpec((tm, D_MODEL), lambda i: (i, 0))
    wspec = _layer_spec(layer, (D_MODEL, D_MODEL))
    vec = _const_spec((1, D_MODEL))
    return pl.pallas_call(
        functools.partial(_post_kernel, final=final),
        grid=(T // tm,),
        in_specs=[row, row, row, row, row, wspec, wspec, wspec,
                  vec, _layer_spec(layer, (D_MODEL, D_FF)), _layer_spec(layer, (D_FF, D_MODEL)), vec],
        out_specs=row,
        out_shape=jax.ShapeDtypeStruct((T, D_MODEL), F32),
        compiler_params=pltpu.CompilerParams(
            dimension_semantics=("parallel",), vmem_limit_bytes=VMEM_LIMIT_BYTES),
        name="post",
    )(x2, hm, ha, ga, gb, wm, wa, wo, g, wu, wd, fg)


def _t5_bucket(n):
    max_exact = N_BUCKETS // 2
    n = np.maximum(n, 0)
    large = max_exact + (np.log(np.maximum(n, 1) / max_exact)
                         / np.log(MAX_DISTANCE / max_exact)
                         * (N_BUCKETS - max_exact)).astype(np.int32)
    large = np.minimum(large, N_BUCKETS - 1)
    return np.where(n < max_exact, n, large).astype(np.int32)


def _bucket_table():
    W = WINDOW
    dist = np.arange(W)[:, None] + W - np.arange(2 * W)[None, :]
    valid = (dist >= 0) & (dist < W)
    return jnp.asarray(np.where(valid, _t5_bucket(dist), -1).astype(np.int32))


def _regroup_kernel(w_ref, out_ref):
    n_gate = 2 * M_HEADS
    gate0 = sum(IN_SIZES[:3])
    tail0 = gate0 + n_gate
    n_tail = sum(IN_SIZES[5:])
    step = REGROUP_PIECE
    for src, dst, n, scale in ((0, 0, gate0, 1.0),
                               (tail0, gate0, A_Q_WIDTH, A_HEAD_DIM ** -0.5 * LOG2E),
                               (tail0 + A_Q_WIDTH, gate0 + A_Q_WIDTH, n_tail - A_Q_WIDTH, 1.0)):
        for o in range(0, n, step):
            piece = w_ref[src + o:src + o + step, :]
            if scale != 1.0:
                piece = piece * scale
            out_ref[:, dst + o:dst + o + step] = piece.T.astype(BF16)
    gate_rows = jnp.concatenate(
        [w_ref[gate0:tail0, :], jnp.zeros((128 - n_gate, w_ref.shape[1]), F32)], axis=0)
    out_ref[:, gate0 + n_tail:] = gate_rows.T.astype(BF16)


def _regroup_w_in(w_in):
    w_t = jnp.swapaxes(w_in, 1, 2)
    depth, n_in, d = w_t.shape
    tc = REGROUP_COLS
    return pl.pallas_call(
        _regroup_kernel,
        grid=(depth, d // tc),
        in_specs=[pl.BlockSpec((None, n_in, tc), lambda l, i: (l, 0, i))],
        out_specs=pl.BlockSpec((None, tc, _INPROJ_WIDTH), lambda l, i: (l, i, 0)),
        out_shape=jax.ShapeDtypeStruct((depth, d, _INPROJ_WIDTH), BF16),
        compiler_params=pltpu.CompilerParams(
            dimension_semantics=("parallel", "parallel"), vmem_limit_bytes=VMEM_LIMIT_BYTES),
        name="regroup",
    )(w_t)


def kernel(x, norm_mix_g, w_in, conv_w, conv_b, b_igate, b_fgate, mlstm_norm_g, attn_sinks,
           rel_bias, w_branch_m, w_branch_a, w_out, norm_mlp_g, w_up, w_down, final_norm_g):
    B, S, D = x.shape
    T = B * S
    bucket = _bucket_table()
    w_all = _regroup_w_in(w_in)
    wm, wa, wo, wu, wd = (w.astype(BF16) for w in (w_branch_m, w_branch_a, w_out, w_up, w_down))
    x2 = x.reshape(T, D)
    for l in range(DEPTH):
        mqk, mv, og, aq, ak, av, ga, gb, g_rows = _inproj(
            l, x2, norm_mix_g[l].reshape(1, D), conv_w[l], conv_b[l].reshape(1, -1), w_all, S)
        gate_b = jnp.concatenate([b_igate[l], b_fgate[l]]).reshape(2 * M_HEADS, 1)
        hm, ha = _mixers(mqk.reshape(B, S, -1), mv.reshape(B, S, -1), og.reshape(B, S, -1), g_rows,
                         gate_b, mlstm_norm_g[l].reshape(1, -1),
                         aq.reshape(B, S, -1), ak.reshape(B, S, -1), av.reshape(B, S, -1),
                         attn_sinks[l], rel_bias, bucket)
        x2 = _post(l, x2, hm.reshape(T, -1), ha.reshape(T, -1), ga, gb, wm, wa, wo,
                   norm_mlp_g[l].reshape(1, D), wu, wd,
                   final_norm_g.reshape(1, D), final=(l == DEPTH - 1))
    return x2.reshape(B, S, D)
```

```python
import functools

import numpy as np
import jax
import jax.numpy as jnp
from jax import lax
from jax.experimental import pallas as pl
from jax.experimental.pallas import tpu as pltpu

D_MODEL = 1024
DEPTH = 2
M_HEADS = 4
M_QK_DIM = 128
M_V_DIM = 256
M_QK_WIDTH = M_HEADS * M_QK_DIM
M_V_WIDTH = M_HEADS * M_V_DIM
CONV_K = 4
A_Q_HEADS = 16
A_KV_HEADS = 4
A_HEAD_DIM = 64
A_GROUP = A_Q_HEADS // A_KV_HEADS
A_Q_WIDTH = A_Q_HEADS * A_HEAD_DIM
A_KV_WIDTH = A_KV_HEADS * A_HEAD_DIM
WINDOW = 128
N_BUCKETS = 32
MAX_DISTANCE = 128
D_FF = 4 * D_MODEL
EPS = 1e-6
IN_SIZES = (2 * M_QK_WIDTH, M_V_WIDTH, M_V_WIDTH, M_HEADS, M_HEADS,
            A_Q_WIDTH, A_KV_WIDTH, A_KV_WIDTH, D_MODEL, D_MODEL)

INPROJ_TM = 512
POST_TM = 512
PROJ_SUB = 256
REGROUP_COLS = 256
REGROUP_PIECE = 512
MLSTM_CHUNK = 128
MLSTM_BLOCKS = 2
CONV_HALO = 8
SWA_BLOCKS = 2
NEG_BIG = -1e30
LOG2E = 1.4426950408889634
VMEM_LIMIT_BYTES = 56 * 1024 * 1024

F32 = jnp.float32
BF16 = jnp.bfloat16


def _const_spec(shape):
    nd = len(shape)
    return pl.BlockSpec(shape, lambda *_: (0,) * nd, pipeline_mode=pl.Buffered(1))


def _layer_spec(layer, shape):
    nd = len(shape)
    return pl.BlockSpec((None,) + tuple(shape), lambda *_: (layer,) + (0,) * nd,
                        pipeline_mode=pl.Buffered(1))


def _rmsnorm(x, g):
    return x * lax.rsqrt(jnp.mean(x * x, axis=-1, keepdims=True) + EPS) * g


_INPROJ_GROUPS = (("qk", 2 * M_QK_WIDTH), ("v", M_V_WIDTH), ("o", M_V_WIDTH),
                  ("aq", A_Q_WIDTH), ("ak", A_KV_WIDTH), ("av", A_KV_WIDTH),
                  ("ga", D_MODEL), ("gb", D_MODEL), ("g", 128))
_INPROJ_OFFS = dict(zip((n for n, _ in _INPROJ_GROUPS),
                        np.cumsum([0] + [w for _, w in _INPROJ_GROUPS])[:-1].tolist()))
_INPROJ_WIDTH = sum(w for _, w in _INPROJ_GROUPS)
_PROJ_OUT = tuple((n, 2 * w if n in ("ak", "av") else w) for n, w in _INPROJ_GROUPS[:-1])
_PROJ_OUT_OFFS = dict(zip((n for n, _ in _PROJ_OUT),
                          np.cumsum([0] + [w for _, w in _PROJ_OUT])[:-1].tolist()))
_PROJ_OUT_WIDTH = sum(w for _, w in _PROJ_OUT)


def _inproj_kernel(x_ref, g_ref, cw_ref, cb_ref, w_ref, *refs, tiles_per_seq):
    out_ref, gates_ref, halo_ref = refs

    @pl.when(pl.program_id(0) % tiles_per_seq == 0)
    def _():
        halo_ref[...] = jnp.zeros_like(halo_ref)

    sub = PROJ_SUB
    for t in range(x_ref.shape[0] // sub):
        _inproj_rows(slice(t * sub, (t + 1) * sub), x_ref, g_ref, cw_ref, cb_ref, w_ref,
                     out_ref, gates_ref, halo_ref)


def _inproj_rows(rows, x_ref, g_ref, cw_ref, cb_ref, w_ref, out_ref, gates_ref, halo_ref):
    n_rows = rows.stop - rows.start
    h = _rmsnorm(x_ref[rows, :], g_ref[...]).astype(BF16)
    res = jnp.dot(h, w_ref[...], preferred_element_type=F32)
    group = lambda name, width: res[:, _INPROJ_OFFS[name]:_INPROJ_OFFS[name] + width]

    head_row = lax.broadcasted_iota(jnp.int32, (CONV_HALO, 1), 0)
    gw = 2 * M_QK_WIDTH // 4
    for c in range(4):
        cols = slice(c * gw, (c + 1) * gw)
        r = res[:, cols]
        prev = halo_ref[:, cols]
        halo_ref[:, cols] = r[n_rows - CONV_HALO:, :]
        y = cb_ref[:, cols] + r * cw_ref[CONV_K - 1:CONV_K, cols]
        for s in range(1, CONV_K):
            rolled = pltpu.roll(r, s, axis=0)
            head = jnp.where(head_row < s, pltpu.roll(prev, s, axis=0), rolled[:CONV_HALO])
            shifted = jnp.concatenate([head, rolled[CONV_HALO:]], axis=0)
            y = y + shifted * cw_ref[CONV_K - 1 - s:CONV_K - s, cols]
        y = y * jax.nn.sigmoid(y)
        if c < 2:
            y = y * (M_QK_DIM ** -0.5)
        out_ref[rows, cols] = y.astype(BF16)

    for name, width in _INPROJ_GROUPS[1:-1]:
        val = group(name, width)
        if name in ("o", "ga", "gb"):
            val = jax.nn.sigmoid(val)
        if name in ("ak", "av"):
            half = A_HEAD_DIM
            swapped = [pltpu.roll(val[:, c:c + 2 * half], half, axis=1) for c in range(0, width, 2 * half)]
            val = jnp.concatenate([val] + swapped, axis=1)
        off = _PROJ_OUT_OFFS[name]
        out_ref[rows, off:off + val.shape[1]] = val.astype(BF16)
    gates_ref[:, rows] = group("g", 128).T[:2 * M_HEADS, :]


def _inproj(layer, x2, g, conv_w, conv_b, w_all, seq_len):
    T = x2.shape[0]
    tm = INPROJ_TM
    row = lambda n: pl.BlockSpec((tm, n), lambda i: (i, 0))
    return pl.pallas_call(
        functools.partial(_inproj_kernel, tiles_per_seq=seq_len // tm),
        grid=(T // tm,),
        in_specs=[row(D_MODEL), _const_spec((1, D_MODEL)), _const_spec((CONV_K, 2 * M_QK_WIDTH)),
                  _const_spec((1, 2 * M_QK_WIDTH)), _layer_spec(layer, (D_MODEL, _INPROJ_WIDTH))],
        out_specs=[row(_PROJ_OUT_WIDTH), pl.BlockSpec((2 * M_HEADS, tm), lambda i: (0, i))],
        out_shape=[jax.ShapeDtypeStruct((T, _PROJ_OUT_WIDTH), BF16),
                   jax.ShapeDtypeStruct((2 * M_HEADS, T), F32)],
        scratch_shapes=[pltpu.VMEM((CONV_HALO, 2 * M_QK_WIDTH), F32)],
        compiler_params=pltpu.CompilerParams(
            dimension_semantics=("arbitrary",), vmem_limit_bytes=VMEM_LIMIT_BYTES),
        name="inproj",
    )(x2, g, conv_w, conv_b, w_all)


def _mlstm_init(c_ref, m_ref):
    @pl.when(pl.program_id(0) == 0)
    def _():
        c_ref[...] = jnp.zeros_like(c_ref)
        m_ref[...] = jnp.zeros_like(m_ref)


def _mlstm_gates(blk, grow_ref, gb_ref, m_ref):
    L = MLSTM_CHUNK
    rows = slice(blk * L, (blk + 1) * L)
    row_i = lax.broadcasted_iota(jnp.int32, (L, L), 0)
    col_j = lax.broadcasted_iota(jnp.int32, (L, L), 1)
    triu_f = (row_i <= col_j).astype(F32)
    g = grow_ref[:, rows] + gb_ref[...]
    logf = pltpu.roll(jax.nn.log_sigmoid(g), M_HEADS, axis=0) * LOG2E
    b = jnp.dot(logf, triu_f, preferred_element_type=F32,
                precision=lax.Precision.HIGHEST)
    a = g * LOG2E - b
    m_prev = m_ref[...]
    m_last = jnp.maximum(m_prev, jnp.max(a, axis=-1, keepdims=True))
    wk = jnp.exp2(a - m_last)
    decay = jnp.exp2(m_prev - m_last)
    m_ref[...] = jnp.sum(logf, axis=-1, keepdims=True) + m_last
    return a, logf, m_prev, wk, decay


def _mlstm_chunk(blk, seqs, ng_ref, between):
    L = MLSTM_CHUNK
    rows = slice(blk * L, (blk + 1) * L)
    row_i = lax.broadcasted_iota(jnp.int32, (L, L), 0)
    col_j = lax.broadcasted_iota(jnp.int32, (L, L), 1)
    tril = col_j <= row_i
    eye_b = jnp.where(col_j == row_i, 1.0, 0.0).astype(BF16)
    ones_b = jnp.ones((L, 128), BF16)
    mean_b = jnp.full((M_V_DIM, 128), 1.0 / M_V_DIM, BF16)
    nt = (((1,), (1,)), ((), ()))
    units = [(seq, h) for seq in seqs for h in range(M_HEADS)]
    n_units = range(len(units))

    qb, v_ext, s_raw, k_t, m_prev = [], [], [], [], []
    for u, ((gates, qk_ref, v_ref, _, _, _), h) in enumerate(units):
        qb.append(qk_ref[rows, h * M_QK_DIM:(h + 1) * M_QK_DIM])
        kb = qk_ref[rows, M_QK_WIDTH + h * M_QK_DIM:M_QK_WIDTH + (h + 1) * M_QK_DIM]
        v_ext.append(jnp.concatenate([v_ref[rows, h * M_V_DIM:(h + 1) * M_V_DIM], ones_b], axis=1))
        st = lax.dot_general(jnp.concatenate([qb[u], eye_b], axis=0), kb, nt,
                             preferred_element_type=F32)
        s_raw.append(st[:L])
        k_t.append(st[L:])
        m_prev.append(gates[2][h:h + 1, 0:1])
    between[0]()

    both, floor = [], []
    zeros_b = jnp.zeros((M_QK_DIM, M_QK_DIM), BF16)
    for u, ((gates, _, _, _, _, c_ref), h) in enumerate(units):
        a, logf, _, wk, decay_rows = gates
        a_mat = jnp.where(tril, a[h:h + 1, :], -jnp.inf)
        m_row = jnp.maximum(m_prev[u], jnp.max(a_mat, axis=-1, keepdims=True))
        b_col = jnp.sum(jnp.where(tril, logf[h:h + 1, :], 0.0), axis=-1, keepdims=True)
        floor.append(jnp.exp2(-(b_col + m_row)))
        s = s_raw[u] * jnp.exp2(a_mat - m_row)
        q_inter = (qb[u].astype(F32) * jnp.exp2(m_prev[u] - m_row)).astype(BF16)
        kw_t = (k_t[u] * wk[h:h + 1, :]).astype(BF16)
        c_prev = c_ref[h]
        lhs = jnp.concatenate([jnp.concatenate([s.astype(BF16), q_inter], axis=1),
                               jnp.concatenate([kw_t, zeros_b], axis=1)], axis=0)
        res = jnp.dot(lhs, jnp.concatenate([v_ext[u], c_prev.astype(BF16)], axis=0),
                      preferred_element_type=F32)
        both.append(res[:L])
        c_ref[h] = decay_rows[h:h + 1, 0:1] * c_prev + res[L:]
    between[1]()

    hh = []
    for u in n_units:
        inv = 1.0 / jnp.maximum(jnp.abs(both[u][:, M_V_DIM:]), floor[u])
        hh.append(both[u][:, :M_V_DIM] * jnp.concatenate([inv, inv], axis=1))
    msq = [jnp.mean(x * x, axis=-1, keepdims=True) for x in hh]
    between[2]()

    for u, ((_, _, _, og_ref, out_ref, _), h) in enumerate(units):
        sl = slice(h * M_V_DIM, (h + 1) * M_V_DIM)
        hn = hh[u] * lax.rsqrt(msq[u] + EPS) * ng_ref[:, sl]
        out_ref[rows, sl] = hn.astype(BF16) * og_ref[rows, sl]
    between[3]()


def _swa_init(sink_ref, rel_ref, bucket_ref, bias_ref):
    W = WINDOW

    @pl.when(pl.program_id(0) == 0)
    def _():
        bucket = bucket_ref[...]
        neg = jnp.full((W, 2 * W), NEG_BIG, F32)
        for hq in range(A_Q_HEADS):
            bias_ref[1, hq] = neg

        def fill(bb, carry):
            hit = bucket == bb
            for hq in range(A_Q_HEADS):
                bias_ref[1, hq] = jnp.where(hit, rel_ref[bb, hq] * LOG2E, bias_ref[1, hq])
            return carry
        lax.fori_loop(0, N_BUCKETS, fill, 0)
        col = lax.broadcasted_iota(jnp.int32, (W, 2 * W), 1)
        for hq in range(A_Q_HEADS):
            sink = sink_ref[hq] * LOG2E
            bias_ref[0, hq] = jnp.where(col == 0, sink, jnp.where(col < W, NEG_BIG, bias_ref[1, hq]))
            bias_ref[1, hq] = jnp.where(col == 0, sink, bias_ref[1, hq])


def _swa_kv_head(blk, kv, q_ref, kp_ref, kc_ref, vp_ref, vc_ref, out_ref, bias_ref):
    W = WINDOW
    lane_lo = lax.broadcasted_iota(jnp.int32, (2 * W, 2 * A_HEAD_DIM), 1) < A_HEAD_DIM
    not_sink = lax.broadcasted_iota(jnp.int32, (2 * W, 2 * A_HEAD_DIM), 0) > 0
    keep_lo = lane_lo & not_sink
    keep_hi = jnp.logical_not(lane_lo) & not_sink
    zero = jnp.zeros((), BF16)
    ones_lo = jnp.where(lane_lo, 1.0, 0.0).astype(BF16)
    ones_hi = jnp.where(lane_lo, 0.0, 1.0).astype(BF16)
    nt = (((1,), (1,)), ((), ()))
    rows = slice(blk * W, (blk + 1) * W)
    var = jnp.minimum(pl.program_id(0), 1) if blk == 0 else 1
    def tile(ref, rws, swapped):
        c0 = (A_KV_WIDTH if swapped else 0) + (kv // 2) * 2 * A_HEAD_DIM
        return ref[rws, c0:c0 + 2 * A_HEAD_DIM]

    def prev_cur(p_ref, c_ref, swapped):
        prev = (tile(p_ref, slice(None), swapped) if blk == 0
                else tile(c_ref, slice((blk - 1) * W, blk * W), swapped))
        return jnp.concatenate([prev, tile(c_ref, rows, swapped)], axis=0)

    in_low = kv % 2 == 1
    k_even = jnp.where(keep_lo, prev_cur(kp_ref, kc_ref, in_low), zero)
    k_odd = jnp.where(keep_hi, prev_cur(kp_ref, kc_ref, not in_low), zero)
    v_even = jnp.concatenate([jnp.where(keep_lo, prev_cur(vp_ref, vc_ref, in_low), zero), ones_lo], axis=1)
    v_odd = jnp.concatenate([jnp.where(keep_hi, prev_cur(vp_ref, vc_ref, not in_low), zero), ones_hi], axis=1)
    v_both = jnp.concatenate([v_even, v_odd], axis=0)
    pairs = [kv * (A_GROUP // 2) + p for p in range(A_GROUP // 2)]
    cols = [slice(pi * 2 * A_HEAD_DIM, (pi + 1) * 2 * A_HEAD_DIM) for pi in pairs]
    q_st = jnp.concatenate([q_ref[rows, c] for c in cols], axis=0)
    bias_e = jnp.concatenate([bias_ref[var, 2 * pi] for pi in pairs], axis=0)
    bias_o = jnp.concatenate([bias_ref[var, 2 * pi + 1] for pi in pairs], axis=0)
    s_e = lax.dot_general(q_st, k_even, nt, preferred_element_type=F32) + bias_e
    s_o = lax.dot_general(q_st, k_odd, nt, preferred_element_type=F32) + bias_o
    p_e = jnp.exp2(s_e - jnp.max(s_e, axis=-1, keepdims=True)).astype(BF16)
    p_o = jnp.exp2(s_o - jnp.max(s_o, axis=-1, keepdims=True)).astype(BF16)
    acc = jnp.dot(jnp.concatenate([p_e, p_o], axis=1), v_both,
                  preferred_element_type=F32)
    out = (acc[:, :2 * A_HEAD_DIM] / acc[:, 2 * A_HEAD_DIM:]).astype(out_ref.dtype)
    for p, c in enumerate(cols):
        out_ref[rows, c] = out[p * W:(p + 1) * W]


def _mixers_kernel(sink_ref, rel_ref, bucket_ref, qk_ref, v_ref, og_ref, gb_ref, ng_ref,
                   q_ref, kp_ref, kc_ref, vp_ref, vc_ref, *refs):
    n_seq = qk_ref.shape[0]
    grow_refs = refs[:n_seq]
    hm_ref, ha_ref, c_ref, m_ref, bias_ref = refs[n_seq:]
    _mlstm_init(c_ref, m_ref)
    _swa_init(sink_ref, rel_ref, bucket_ref, bias_ref)
    for blk in range(SWA_BLOCKS):
        seqs = [(_mlstm_gates(blk, grow_refs[b], gb_ref, m_ref.at[b]),
                 qk_ref.at[b], v_ref.at[b], og_ref.at[b], hm_ref.at[b], c_ref.at[b])
                for b in range(n_seq)]
        swa = [functools.partial(_swa_kv_head, blk, u, q_ref.at[b], kp_ref.at[b], kc_ref.at[b],
                                 vp_ref.at[b], vc_ref.at[b], ha_ref.at[b], bias_ref)
               for b in range(n_seq) for u in range(A_KV_HEADS)]
        per_gap = len(swa) // 4
        _mlstm_chunk(blk, seqs, ng_ref,
                     between=[lambda g=g: [f() for f in swa[g * per_gap:(g + 1) * per_gap]]
                              for g in range(4)])


def _proj_cols(name):
    width = dict(_PROJ_OUT)[name]
    off = _PROJ_OUT_OFFS[name]
    assert off % width == 0
    return width, off // width


def _mixers(proj, g_rows, gate_b, norm_g, sinks, rel_bias, bucket):
    B, S, _ = proj.shape
    W = WINDOW
    tq = SWA_BLOCKS * W
    assert tq == MLSTM_BLOCKS * MLSTM_CHUNK

    def cur(name):
        n, j = _proj_cols(name)
        return pl.BlockSpec((B, tq, n), lambda i: (0, i, j))

    def prev(name):
        n, j = _proj_cols(name)
        return pl.BlockSpec((B, W, n), lambda i: (0, jnp.maximum(i * SWA_BLOCKS - 1, 0), j))

    out = lambda n: pl.BlockSpec((B, tq, n), lambda i: (0, i, 0))
    gate_rows = [pl.BlockSpec((2 * M_HEADS, tq), lambda i, b=b: (0, b * (S // tq) + i)) for b in range(B)]
    smem = pl.BlockSpec(memory_space=pltpu.SMEM)
    return pl.pallas_call(
        _mixers_kernel,
        grid=(S // tq,),
        in_specs=[smem, smem, _const_spec((W, 2 * W)),
                  cur("qk"), cur("v"), cur("o"),
                  _const_spec((2 * M_HEADS, 1)), _const_spec((1, M_V_WIDTH)),
                  cur("aq"), prev("ak"), cur("ak"), prev("av"), cur("av")] + gate_rows,
        out_specs=[out(M_V_WIDTH), out(A_Q_WIDTH)],
        out_shape=[jax.ShapeDtypeStruct((B, S, M_V_WIDTH), BF16),
                   jax.ShapeDtypeStruct((B, S, A_Q_WIDTH), BF16)],
        scratch_shapes=[pltpu.VMEM((B, M_HEADS, M_QK_DIM, M_V_DIM + 128), F32),
                        pltpu.VMEM((B, 2 * M_HEADS, MLSTM_CHUNK), F32),
                        pltpu.VMEM((2, A_Q_HEADS, W, 2 * W), F32)],
        compiler_params=pltpu.CompilerParams(
            dimension_semantics=("arbitrary",), vmem_limit_bytes=VMEM_LIMIT_BYTES),
        name="mixers",
    )(sinks, rel_bias, bucket, proj, proj, proj, gate_b, norm_g, proj, proj, proj, proj, proj,
      *([g_rows] * B))


def _post_kernel(x_ref, hm_ref, ha_ref, ga_ref, gb_ref, wm_ref, wa_ref, wo_ref,
                 g_ref, wu_ref, wd_ref, fg_ref, out_ref, *, final):
    sub = PROJ_SUB
    for t in range(x_ref.shape[0] // sub):
        rows = slice(t * sub, (t + 1) * sub)
        y = (ga_ref[rows, :] * jnp.dot(hm_ref[rows, :], wm_ref[...], preferred_element_type=F32)
             + gb_ref[rows, :] * jnp.dot(ha_ref[rows, :], wa_ref[...], preferred_element_type=F32))
        x = x_ref[rows, :] + jnp.dot(y.astype(BF16), wo_ref[...], preferred_element_type=F32)
        h = _rmsnorm(x, g_ref[...]).astype(BF16)
        u = jnp.maximum(jnp.dot(h, wu_ref[...], preferred_element_type=F32), 0.0)
        x = x + jnp.dot((u * u).astype(BF16), wd_ref[...], preferred_element_type=F32)
        if final:
            x = _rmsnorm(x, fg_ref[...])
        out_ref[rows, :] = x


def _post(layer, x2, hm, ha, proj, wm, wa, wo, g, wu, wd, fg, final):
    T = x2.shape[0]
    tm = POST_TM
    row = pl.BlockSpec((tm, D_MODEL), lambda i: (i, 0))
    gate = lambda name: pl.BlockSpec((tm, D_MODEL), lambda i: (i, _proj_cols(name)[1]))
    wspec = _layer_spec(layer, (D_MODEL, D_MODEL))
    vec = _const_spec((1, D_MODEL))
    return pl.pallas_call(
        functools.partial(_post_kernel, final=final),
        grid=(T // tm,),
        in_specs=[row, row, row, gate("ga"), gate("gb"), wspec, wspec, wspec,
                  vec, _layer_spec(layer, (D_MODEL, D_FF)), _layer_spec(layer, (D_FF, D_MODEL)), vec],
        out_specs=row,
        out_shape=jax.ShapeDtypeStruct((T, D_MODEL), F32),
        compiler_params=pltpu.CompilerParams(
            dimension_semantics=("parallel",), vmem_limit_bytes=VMEM_LIMIT_BYTES),
        name="post",
    )(x2, hm, ha, proj, proj, wm, wa, wo, g, wu, wd, fg)


def _t5_bucket(n):
    max_exact = N_BUCKETS // 2
    n = np.maximum(n, 0)
    large = max_exact + (np.log(np.maximum(n, 1) / max_exact)
                         / np.log(MAX_DISTANCE / max_exact)
                         * (N_BUCKETS - max_exact)).astype(np.int32)
    large = np.minimum(large, N_BUCKETS - 1)
    return np.where(n < max_exact, n, large).astype(np.int32)


def _bucket_table():
    W = WINDOW
    dist = np.arange(W)[:, None] + W - np.arange(2 * W)[None, :]
    valid = (dist >= 0) & (dist < W)
    return jnp.asarray(np.where(valid, _t5_bucket(dist), -1).astype(np.int32))


def _regroup_kernel(w_ref, out_ref):
    n_gate = 2 * M_HEADS
    gate0 = sum(IN_SIZES[:3])
    tail0 = gate0 + n_gate
    n_tail = sum(IN_SIZES[5:])
    step = REGROUP_PIECE
    for src, dst, n, scale in ((0, 0, gate0, 1.0),
                               (tail0, gate0, A_Q_WIDTH, A_HEAD_DIM ** -0.5 * LOG2E),
                               (tail0 + A_Q_WIDTH, gate0 + A_Q_WIDTH, n_tail - A_Q_WIDTH, 1.0)):
        for o in range(0, n, step):
            piece = w_ref[src + o:src + o + step, :]
            if scale != 1.0:
                piece = piece * scale
            out_ref[:, dst + o:dst + o + step] = piece.T.astype(BF16)
    gate_rows = jnp.concatenate(
        [w_ref[gate0:tail0, :], jnp.zeros((128 - n_gate, w_ref.shape[1]), F32)], axis=0)
    out_ref[:, gate0 + n_tail:] = gate_rows.T.astype(BF16)


def _regroup_w_in(w_in):
    w_t = jnp.swapaxes(w_in, 1, 2)
    depth, n_in, d = w_t.shape
    tc = REGROUP_COLS
    return pl.pallas_call(
        _regroup_kernel,
        grid=(depth, d // tc),
        in_specs=[pl.BlockSpec((None, n_in, tc), lambda l, i: (l, 0, i))],
        out_specs=pl.BlockSpec((None, tc, _INPROJ_WIDTH), lambda l, i: (l, i, 0)),
        out_shape=jax.ShapeDtypeStruct((depth, d, _INPROJ_WIDTH), BF16),
        compiler_params=pltpu.CompilerParams(
            dimension_semantics=("parallel", "parallel"), vmem_limit_bytes=VMEM_LIMIT_BYTES),
        name="regroup",
    )(w_t)


def kernel(x, norm_mix_g, w_in, conv_w, conv_b, b_igate, b_fgate, mlstm_norm_g, attn_sinks,
           rel_bias, w_branch_m, w_branch_a, w_out, norm_mlp_g, w_up, w_down, final_norm_g):
    B, S, D = x.shape
    T = B * S
    bucket = _bucket_table()
    w_all = _regroup_w_in(w_in)
    wm, wa, wo, wu, wd = (w.astype(BF16) for w in (w_branch_m, w_branch_a, w_out, w_up, w_down))
    x2 = x.reshape(T, D)
    for l in range(DEPTH):
        proj, g_rows = _inproj(
            l, x2, norm_mix_g[l].reshape(1, D), conv_w[l], conv_b[l].reshape(1, -1), w_all, S)
        gate_b = jnp.concatenate([b_igate[l], b_fgate[l]]).reshape(2 * M_HEADS, 1)
        hm, ha = _mixers(proj.reshape(B, S, -1), g_rows, gate_b, mlstm_norm_g[l].reshape(1, -1),
                         attn_sinks[l], rel_bias, bucket)
        x2 = _post(l, x2, hm.reshape(T, -1), ha.reshape(T, -1), proj, wm, wa, wo,
                   norm_mlp_g[l].reshape(1, D), wu, wd,
                   final_norm_g.reshape(1, D), final=(l == DEPTH - 1))
    return x2.reshape(B, S, D)
```

```python
import functools

import numpy as np
import jax
import jax.numpy as jnp
from jax import lax
from jax.experimental import pallas as pl
from jax.experimental.pallas import tpu as pltpu

D_MODEL = 1024
DEPTH = 2
M_HEADS = 4
M_QK_DIM = 128
M_V_DIM = 256
M_QK_WIDTH = M_HEADS * M_QK_DIM
M_V_WIDTH = M_HEADS * M_V_DIM
CONV_K = 4
A_Q_HEADS = 16
A_KV_HEADS = 4
A_HEAD_DIM = 64
A_GROUP = A_Q_HEADS // A_KV_HEADS
A_Q_WIDTH = A_Q_HEADS * A_HEAD_DIM
A_KV_WIDTH = A_KV_HEADS * A_HEAD_DIM
WINDOW = 128
N_BUCKETS = 32
MAX_DISTANCE = 128
D_FF = 4 * D_MODEL
EPS = 1e-6
IN_SIZES = (2 * M_QK_WIDTH, M_V_WIDTH, M_V_WIDTH, M_HEADS, M_HEADS,
            A_Q_WIDTH, A_KV_WIDTH, A_KV_WIDTH, D_MODEL, D_MODEL)

INPROJ_TM = 512
INPROJ_SUB = 128
POST_TM = 512
PROJ_SUB = 256
REGROUP_COLS = 256
REGROUP_PIECE = 512
MLSTM_CHUNK = 128
MLSTM_BLOCKS = 2
CONV_HALO = 8
SWA_BLOCKS = 2
NEG_BIG = -1e30
LOG2E = 1.4426950408889634
VMEM_LIMIT_BYTES = 56 * 1024 * 1024

F32 = jnp.float32
BF16 = jnp.bfloat16


def _const_spec(shape):
    nd = len(shape)
    return pl.BlockSpec(shape, lambda *_: (0,) * nd, pipeline_mode=pl.Buffered(1))


def _layer_spec(layer, shape):
    nd = len(shape)
    return pl.BlockSpec((None,) + tuple(shape), lambda *_: (layer,) + (0,) * nd,
                        pipeline_mode=pl.Buffered(1))


def _rmsnorm(x, g):
    return x * lax.rsqrt(jnp.mean(x * x, axis=-1, keepdims=True) + EPS) * g


_INPROJ_GROUPS = (("qk", 2 * M_QK_WIDTH), ("v", M_V_WIDTH), ("o", M_V_WIDTH),
                  ("aq", A_Q_WIDTH), ("ak", A_KV_WIDTH), ("av", A_KV_WIDTH),
                  ("ga", D_MODEL), ("gb", D_MODEL), ("g", 128))
_INPROJ_OFFS = dict(zip((n for n, _ in _INPROJ_GROUPS),
                        np.cumsum([0] + [w for _, w in _INPROJ_GROUPS])[:-1].tolist()))
_INPROJ_WIDTH = sum(w for _, w in _INPROJ_GROUPS)
_PROJ_OUT = tuple((n, 2 * w if n in ("ak", "av") else w) for n, w in _INPROJ_GROUPS[:-1])
_PROJ_OUT_OFFS = dict(zip((n for n, _ in _PROJ_OUT),
                          np.cumsum([0] + [w for _, w in _PROJ_OUT])[:-1].tolist()))
_PROJ_OUT_WIDTH = sum(w for _, w in _PROJ_OUT)


def _inproj_kernel(x_ref, g_ref, cw_ref, cb_ref, w_ref, *refs, tiles_per_seq):
    out_ref, gates_ref, halo_ref = refs

    @pl.when(pl.program_id(0) % tiles_per_seq == 0)
    def _():
        halo_ref[...] = jnp.zeros_like(halo_ref)

    sub = INPROJ_SUB
    for t in range(x_ref.shape[0] // sub):
        _inproj_rows(slice(t * sub, (t + 1) * sub), x_ref, g_ref, cw_ref, cb_ref, w_ref,
                     out_ref, gates_ref, halo_ref)


def _inproj_rows(rows, x_ref, g_ref, cw_ref, cb_ref, w_ref, out_ref, gates_ref, halo_ref):
    n_rows = rows.stop - rows.start
    h = _rmsnorm(x_ref[rows, :], g_ref[...]).astype(BF16)
    res = jnp.dot(h, w_ref[...], preferred_element_type=F32)
    group = lambda name, width: res[:, _INPROJ_OFFS[name]:_INPROJ_OFFS[name] + width]

    head_row = lax.broadcasted_iota(jnp.int32, (CONV_HALO, 1), 0)
    gw = 2 * M_QK_WIDTH // 4
    for c in range(4):
        cols = slice(c * gw, (c + 1) * gw)
        r = res[:, cols]
        prev = halo_ref[:, cols]
        halo_ref[:, cols] = r[n_rows - CONV_HALO:, :]
        y = cb_ref[:, cols] + r * cw_ref[CONV_K - 1:CONV_K, cols]
        for s in range(1, CONV_K):
            rolled = pltpu.roll(r, s, axis=0)
            head = jnp.where(head_row < s, pltpu.roll(prev, s, axis=0), rolled[:CONV_HALO])
            shifted = jnp.concatenate([head, rolled[CONV_HALO:]], axis=0)
            y = y + shifted * cw_ref[CONV_K - 1 - s:CONV_K - s, cols]
        y = y * jax.nn.sigmoid(y)
        if c < 2:
            y = y * (M_QK_DIM ** -0.5)
        out_ref[rows, cols] = y.astype(BF16)

    for name, width in _INPROJ_GROUPS[1:-1]:
        val = group(name, width)
        if name in ("o", "ga", "gb"):
            val = jax.nn.sigmoid(val)
        if name in ("ak", "av"):
            half = A_HEAD_DIM
            swapped = [pltpu.roll(val[:, c:c + 2 * half], half, axis=1) for c in range(0, width, 2 * half)]
            val = jnp.concatenate([val] + swapped, axis=1)
        off = _PROJ_OUT_OFFS[name]
        out_ref[rows, off:off + val.shape[1]] = val.astype(BF16)
    gates_ref[:, rows] = group("g", 128).T[:2 * M_HEADS, :]


def _inproj(layer, x2, g, conv_w, conv_b, w_all, seq_len):
    T = x2.shape[0]
    tm = INPROJ_TM
    row = lambda n: pl.BlockSpec((tm, n), lambda i: (i, 0))
    return pl.pallas_call(
        functools.partial(_inproj_kernel, tiles_per_seq=seq_len // tm),
        grid=(T // tm,),
        in_specs=[row(D_MODEL), _const_spec((1, D_MODEL)), _const_spec((CONV_K, 2 * M_QK_WIDTH)),
                  _const_spec((1, 2 * M_QK_WIDTH)), _layer_spec(layer, (D_MODEL, _INPROJ_WIDTH))],
        out_specs=[row(_PROJ_OUT_WIDTH), pl.BlockSpec((2 * M_HEADS, tm), lambda i: (0, i))],
        out_shape=[jax.ShapeDtypeStruct((T, _PROJ_OUT_WIDTH), BF16),
                   jax.ShapeDtypeStruct((2 * M_HEADS, T), F32)],
        scratch_shapes=[pltpu.VMEM((CONV_HALO, 2 * M_QK_WIDTH), F32)],
        compiler_params=pltpu.CompilerParams(
            dimension_semantics=("arbitrary",), vmem_limit_bytes=VMEM_LIMIT_BYTES),
        name="inproj",
    )(x2, g, conv_w, conv_b, w_all)


def _mlstm_init(c_ref, m_ref):
    @pl.when(pl.program_id(0) == 0)
    def _():
        c_ref[...] = jnp.zeros_like(c_ref)
        m_ref[...] = jnp.zeros_like(m_ref)


def _mlstm_gates(blk, grow_ref, gb_ref, m_ref):
    L = MLSTM_CHUNK
    rows = slice(blk * L, (blk + 1) * L)
    row_i = lax.broadcasted_iota(jnp.int32, (L, L), 0)
    col_j = lax.broadcasted_iota(jnp.int32, (L, L), 1)
    triu_f = (row_i <= col_j).astype(F32)
    g = grow_ref[:, rows] + gb_ref[...]
    logf = pltpu.roll(jax.nn.log_sigmoid(g), M_HEADS, axis=0) * LOG2E
    b = jnp.dot(logf, triu_f, preferred_element_type=F32,
                precision=lax.Precision.HIGHEST)
    a = g * LOG2E - b
    m_prev = m_ref[...]
    m_last = jnp.maximum(m_prev, jnp.max(a, axis=-1, keepdims=True))
    wk = jnp.exp2(a - m_last)
    decay = jnp.exp2(m_prev - m_last)
    m_ref[...] = jnp.sum(logf, axis=-1, keepdims=True) + m_last
    return a, logf, m_prev, wk, decay


def _mlstm_chunk(blk, seqs, ng_ref, between):
    L = MLSTM_CHUNK
    rows = slice(blk * L, (blk + 1) * L)
    row_i = lax.broadcasted_iota(jnp.int32, (L, L), 0)
    col_j = lax.broadcasted_iota(jnp.int32, (L, L), 1)
    tril = col_j <= row_i
    eye_b = jnp.where(col_j == row_i, 1.0, 0.0).astype(BF16)
    ones_b = jnp.ones((L, 128), BF16)
    mean_b = jnp.full((M_V_DIM, 128), 1.0 / M_V_DIM, BF16)
    nt = (((1,), (1,)), ((), ()))
    units = [(seq, h) for seq in seqs for h in range(M_HEADS)]
    n_units = range(len(units))

    qb, v_ext, s_raw, k_t, m_prev = [], [], [], [], []
    for u, ((gates, qk_ref, v_ref, _, _, _), h) in enumerate(units):
        qb.append(qk_ref[rows, h * M_QK_DIM:(h + 1) * M_QK_DIM])
        kb = qk_ref[rows, M_QK_WIDTH + h * M_QK_DIM:M_QK_WIDTH + (h + 1) * M_QK_DIM]
        v_ext.append(jnp.concatenate([v_ref[rows, h * M_V_DIM:(h + 1) * M_V_DIM], ones_b], axis=1))
        st = lax.dot_general(jnp.concatenate([qb[u], eye_b], axis=0), kb, nt,
                             preferred_element_type=F32)
        s_raw.append(st[:L])
        k_t.append(st[L:])
        m_prev.append(gates[2][h:h + 1, 0:1])
    between[0]()

    both, floor = [], []
    zeros_b = jnp.zeros((M_QK_DIM, M_QK_DIM), BF16)
    for u, ((gates, _, _, _, _, c_ref), h) in enumerate(units):
        a, logf, _, wk, decay_rows = gates
        a_mat = jnp.where(tril, a[h:h + 1, :], -jnp.inf)
        m_row = jnp.maximum(m_prev[u], jnp.max(a_mat, axis=-1, keepdims=True))
        b_col = jnp.sum(jnp.where(tril, logf[h:h + 1, :], 0.0), axis=-1, keepdims=True)
        floor.append(jnp.exp2(-(b_col + m_row)))
        s = s_raw[u] * jnp.exp2(a_mat - m_row)
        q_inter = (qb[u].astype(F32) * jnp.exp2(m_prev[u] - m_row)).astype(BF16)
        kw_t = (k_t[u] * wk[h:h + 1, :]).astype(BF16)
        c_prev = c_ref[h]
        lhs = jnp.concatenate([jnp.concatenate([s.astype(BF16), q_inter], axis=1),
                               jnp.concatenate([kw_t, zeros_b], axis=1)], axis=0)
        res = jnp.dot(lhs, jnp.concatenate([v_ext[u], c_prev.astype(BF16)], axis=0),
                      preferred_element_type=F32)
        both.append(res[:L])
        c_ref[h] = decay_rows[h:h + 1, 0:1] * c_prev + res[L:]
    between[1]()

    hh = []
    for u in n_units:
        inv = 1.0 / jnp.maximum(jnp.abs(both[u][:, M_V_DIM:]), floor[u])
        hh.append(both[u][:, :M_V_DIM] * jnp.concatenate([inv, inv], axis=1))
    msq = [jnp.mean(x * x, axis=-1, keepdims=True) for x in hh]
    between[2]()

    for u, ((_, _, _, og_ref, out_ref, _), h) in enumerate(units):
        sl = slice(h * M_V_DIM, (h + 1) * M_V_DIM)
        hn = hh[u] * lax.rsqrt(msq[u] + EPS) * ng_ref[:, sl]
        out_ref[rows, sl] = hn.astype(BF16) * og_ref[rows, sl]
    between[3]()


def _swa_init(sink_ref, rel_ref, bucket_ref, bias_ref):
    W = WINDOW

    @pl.when(pl.program_id(0) == 0)
    def _():
        bucket = bucket_ref[...]
        neg = jnp.full((W, 2 * W), NEG_BIG, F32)
        for hq in range(A_Q_HEADS):
            bias_ref[1, hq] = neg

        def fill(bb, carry):
            hit = bucket == bb
            for hq in range(A_Q_HEADS):
                bias_ref[1, hq] = jnp.where(hit, rel_ref[bb, hq] * LOG2E, bias_ref[1, hq])
            return carry
        lax.fori_loop(0, N_BUCKETS, fill, 0)
        col = lax.broadcasted_iota(jnp.int32, (W, 2 * W), 1)
        for hq in range(A_Q_HEADS):
            sink = sink_ref[hq] * LOG2E
            bias_ref[0, hq] = jnp.where(col == 0, sink, jnp.where(col < W, NEG_BIG, bias_ref[1, hq]))
            bias_ref[1, hq] = jnp.where(col == 0, sink, bias_ref[1, hq])


def _swa_kv_head(blk, kv, q_ref, kp_ref, kc_ref, vp_ref, vc_ref, out_ref, bias_ref):
    W = WINDOW
    lane_lo = lax.broadcasted_iota(jnp.int32, (2 * W, 2 * A_HEAD_DIM), 1) < A_HEAD_DIM
    not_sink = lax.broadcasted_iota(jnp.int32, (2 * W, 2 * A_HEAD_DIM), 0) > 0
    keep_lo = lane_lo & not_sink
    keep_hi = jnp.logical_not(lane_lo) & not_sink
    zero = jnp.zeros((), BF16)
    ones_lo = jnp.where(lane_lo, 1.0, 0.0).astype(BF16)
    ones_hi = jnp.where(lane_lo, 0.0, 1.0).astype(BF16)
    nt = (((1,), (1,)), ((), ()))
    rows = slice(blk * W, (blk + 1) * W)
    var = jnp.minimum(pl.program_id(0), 1) if blk == 0 else 1
    def tile(ref, rws, swapped):
        c0 = (A_KV_WIDTH if swapped else 0) + (kv // 2) * 2 * A_HEAD_DIM
        return ref[rws, c0:c0 + 2 * A_HEAD_DIM]

    def prev_cur(p_ref, c_ref, swapped):
        prev = (tile(p_ref, slice(None), swapped) if blk == 0
                else tile(c_ref, slice((blk - 1) * W, blk * W), swapped))
        return jnp.concatenate([prev, tile(c_ref, rows, swapped)], axis=0)

    in_low = kv % 2 == 1
    k_even = jnp.where(keep_lo, prev_cur(kp_ref, kc_ref, in_low), zero)
    k_odd = jnp.where(keep_hi, prev_cur(kp_ref, kc_ref, not in_low), zero)
    v_even = jnp.concatenate([jnp.where(keep_lo, prev_cur(vp_ref, vc_ref, in_low), zero), ones_lo], axis=1)
    v_odd = jnp.concatenate([jnp.where(keep_hi, prev_cur(vp_ref, vc_ref, not in_low), zero), ones_hi], axis=1)
    v_both = jnp.concatenate([v_even, v_odd], axis=0)
    pairs = [kv * (A_GROUP // 2) + p for p in range(A_GROUP // 2)]
    cols = [slice(pi * 2 * A_HEAD_DIM, (pi + 1) * 2 * A_HEAD_DIM) for pi in pairs]
    q_st = jnp.concatenate([q_ref[rows, c] for c in cols], axis=0)
    bias_e = jnp.concatenate([bias_ref[var, 2 * pi] for pi in pairs], axis=0)
    bias_o = jnp.concatenate([bias_ref[var, 2 * pi + 1] for pi in pairs], axis=0)
    s_e = lax.dot_general(q_st, k_even, nt, preferred_element_type=F32) + bias_e
    s_o = lax.dot_general(q_st, k_odd, nt, preferred_element_type=F32) + bias_o
    p_e = jnp.exp2(s_e - jnp.max(s_e, axis=-1, keepdims=True)).astype(BF16)
    p_o = jnp.exp2(s_o - jnp.max(s_o, axis=-1, keepdims=True)).astype(BF16)
    acc = jnp.dot(jnp.concatenate([p_e, p_o], axis=1), v_both,
                  preferred_element_type=F32)
    out = (acc[:, :2 * A_HEAD_DIM] / acc[:, 2 * A_HEAD_DIM:]).astype(out_ref.dtype)
    for p, c in enumerate(cols):
        out_ref[rows, c] = out[p * W:(p + 1) * W]


def _mixers_kernel(sink_ref, rel_ref, bucket_ref, qk_ref, v_ref, og_ref, gb_ref, ng_ref,
                   q_ref, kp_ref, kc_ref, vp_ref, vc_ref, *refs):
    n_seq = qk_ref.shape[0]
    grow_refs = refs[:n_seq]
    hm_ref, ha_ref, c_ref, m_ref, bias_ref = refs[n_seq:]
    _mlstm_init(c_ref, m_ref)
    _swa_init(sink_ref, rel_ref, bucket_ref, bias_ref)
    for blk in range(SWA_BLOCKS):
        seqs = [(_mlstm_gates(blk, grow_refs[b], gb_ref, m_ref.at[b]),
                 qk_ref.at[b], v_ref.at[b], og_ref.at[b], hm_ref.at[b], c_ref.at[b])
                for b in range(n_seq)]
        swa = [functools.partial(_swa_kv_head, blk, u, q_ref.at[b], kp_ref.at[b], kc_ref.at[b],
                                 vp_ref.at[b], vc_ref.at[b], ha_ref.at[b], bias_ref)
               for b in range(n_seq) for u in range(A_KV_HEADS)]
        per_gap = len(swa) // 4
        _mlstm_chunk(blk, seqs, ng_ref,
                     between=[lambda g=g: [f() for f in swa[g * per_gap:(g + 1) * per_gap]]
                              for g in range(4)])


def _proj_cols(name):
    width = dict(_PROJ_OUT)[name]
    off = _PROJ_OUT_OFFS[name]
    assert off % width == 0
    return width, off // width


def _mixers(proj, g_rows, gate_b, norm_g, sinks, rel_bias, bucket):
    B, S, _ = proj.shape
    W = WINDOW
    tq = SWA_BLOCKS * W
    assert tq == MLSTM_BLOCKS * MLSTM_CHUNK

    def cur(name):
        n, j = _proj_cols(name)
        return pl.BlockSpec((B, tq, n), lambda i: (0, i, j))

    def prev(name):
        n, j = _proj_cols(name)
        return pl.BlockSpec((B, W, n), lambda i: (0, jnp.maximum(i * SWA_BLOCKS - 1, 0), j))

    out = lambda n: pl.BlockSpec((B, tq, n), lambda i: (0, i, 0))
    gate_rows = [pl.BlockSpec((2 * M_HEADS, tq), lambda i, b=b: (0, b * (S // tq) + i)) for b in range(B)]
    smem = pl.BlockSpec(memory_space=pltpu.SMEM)
    return pl.pallas_call(
        _mixers_kernel,
        grid=(S // tq,),
        in_specs=[smem, smem, _const_spec((W, 2 * W)),
                  cur("qk"), cur("v"), cur("o"),
                  _const_spec((2 * M_HEADS, 1)), _const_spec((1, M_V_WIDTH)),
                  cur("aq"), prev("ak"), cur("ak"), prev("av"), cur("av")] + gate_rows,
        out_specs=[out(M_V_WIDTH), out(A_Q_WIDTH)],
        out_shape=[jax.ShapeDtypeStruct((B, S, M_V_WIDTH), BF16),
                   jax.ShapeDtypeStruct((B, S, A_Q_WIDTH), BF16)],
        scratch_shapes=[pltpu.VMEM((B, M_HEADS, M_QK_DIM, M_V_DIM + 128), F32),
                        pltpu.VMEM((B, 2 * M_HEADS, MLSTM_CHUNK), F32),
                        pltpu.VMEM((2, A_Q_HEADS, W, 2 * W), F32)],
        compiler_params=pltpu.CompilerParams(
            dimension_semantics=("arbitrary",), vmem_limit_bytes=VMEM_LIMIT_BYTES),
        name="mixers",
    )(sinks, rel_bias, bucket, proj, proj, proj, gate_b, norm_g, proj, proj, proj, proj, proj,
      *([g_rows] * B))


def _post_kernel(x_ref, hm_ref, ha_ref, ga_ref, gb_ref, wm_ref, wa_ref, wo_ref,
                 g_ref, wu_ref, wd_ref, fg_ref, out_ref, *, final):
    sub = PROJ_SUB
    for t in range(x_ref.shape[0] // sub):
        rows = slice(t * sub, (t + 1) * sub)
        y = (ga_ref[rows, :] * jnp.dot(hm_ref[rows, :], wm_ref[...], preferred_element_type=F32)
             + gb_ref[rows, :] * jnp.dot(ha_ref[rows, :], wa_ref[...], preferred_element_type=F32))
        x = x_ref[rows, :] + jnp.dot(y.astype(BF16), wo_ref[...], preferred_element_type=F32)
        h = _rmsnorm(x, g_ref[...]).astype(BF16)
        u = jnp.maximum(jnp.dot(h, wu_ref[...], preferred_element_type=F32), 0.0)
        x = x + jnp.dot((u * u).astype(BF16), wd_ref[...], preferred_element_type=F32)
        if final:
            x = _rmsnorm(x, fg_ref[...])
        out_ref[rows, :] = x


def _post(layer, x2, hm, ha, proj, wm, wa, wo, g, wu, wd, fg, final):
    T = x2.shape[0]
    tm = POST_TM
    row = pl.BlockSpec((tm, D_MODEL), lambda i: (i, 0))
    gate = lambda name: pl.BlockSpec((tm, D_MODEL), lambda i: (i, _proj_cols(name)[1]))
    wspec = _layer_spec(layer, (D_MODEL, D_MODEL))
    vec = _const_spec((1, D_MODEL))
    return pl.pallas_call(
        functools.partial(_post_kernel, final=final),
        grid=(T // tm,),
        in_specs=[row, row, row, gate("ga"), gate("gb"), wspec, wspec, wspec,
                  vec, _layer_spec(layer, (D_MODEL, D_FF)), _layer_spec(layer, (D_FF, D_MODEL)), vec],
        out_specs=row,
        out_shape=jax.ShapeDtypeStruct((T, D_MODEL), F32),
        compiler_params=pltpu.CompilerParams(
            dimension_semantics=("parallel",), vmem_limit_bytes=VMEM_LIMIT_BYTES),
        name="post",
    )(x2, hm, ha, proj, proj, wm, wa, wo, g, wu, wd, fg)


def _t5_bucket(n):
    max_exact = N_BUCKETS // 2
    n = np.maximum(n, 0)
    large = max_exact + (np.log(np.maximum(n, 1) / max_exact)
                         / np.log(MAX_DISTANCE / max_exact)
                         * (N_BUCKETS - max_exact)).astype(np.int32)
    large = np.minimum(large, N_BUCKETS - 1)
    return np.where(n < max_exact, n, large).astype(np.int32)


def _bucket_table():
    W = WINDOW
    dist = np.arange(W)[:, None] + W - np.arange(2 * W)[None, :]
    valid = (dist >= 0) & (dist < W)
    return jnp.asarray(np.where(valid, _t5_bucket(dist), -1).astype(np.int32))


def _regroup_kernel(w_ref, out_ref):
    n_gate = 2 * M_HEADS
    gate0 = sum(IN_SIZES[:3])
    tail0 = gate0 + n_gate
    n_tail = sum(IN_SIZES[5:])
    step = REGROUP_PIECE
    for src, dst, n, scale in ((0, 0, gate0, 1.0),
                               (tail0, gate0, A_Q_WIDTH, A_HEAD_DIM ** -0.5 * LOG2E),
                               (tail0 + A_Q_WIDTH, gate0 + A_Q_WIDTH, n_tail - A_Q_WIDTH, 1.0)):
        for o in range(0, n, step):
            piece = w_ref[src + o:src + o + step, :]
            if scale != 1.0:
                piece = piece * scale
            out_ref[:, dst + o:dst + o + step] = piece.T.astype(BF16)
    gate_rows = jnp.concatenate(
        [w_ref[gate0:tail0, :], jnp.zeros((128 - n_gate, w_ref.shape[1]), F32)], axis=0)
    out_ref[:, gate0 + n_tail:] = gate_rows.T.astype(BF16)


def _regroup_w_in(w_in):
    w_t = jnp.swapaxes(w_in, 1, 2)
    depth, n_in, d = w_t.shape
    tc = REGROUP_COLS
    return pl.pallas_call(
        _regroup_kernel,
        grid=(depth, d // tc),
        in_specs=[pl.BlockSpec((None, n_in, tc), lambda l, i: (l, 0, i))],
        out_specs=pl.BlockSpec((None, tc, _INPROJ_WIDTH), lambda l, i: (l, i, 0)),
        out_shape=jax.ShapeDtypeStruct((depth, d, _INPROJ_WIDTH), BF16),
        compiler_params=pltpu.CompilerParams(
            dimension_semantics=("parallel", "parallel"), vmem_limit_bytes=VMEM_LIMIT_BYTES),
        name="regroup",
    )(w_t)


def kernel(x, norm_mix_g, w_in, conv_w, conv_b, b_igate, b_fgate, mlstm_norm_g, attn_sinks,
           rel_bias, w_branch_m, w_branch_a, w_out, norm_mlp_g, w_up, w_down, final_norm_g):
    B, S, D = x.shape
    T = B * S
    bucket = _bucket_table()
    w_all = _regroup_w_in(w_in)
    wm, wa, wo, wu, wd = (w.astype(BF16) for w in (w_branch_m, w_branch_a, w_out, w_up, w_down))
    x2 = x.reshape(T, D)
    for l in range(DEPTH):
        proj, g_rows = _inproj(
            l, x2, norm_mix_g[l].reshape(1, D), conv_w[l], conv_b[l].reshape(1, -1), w_all, S)
        gate_b = jnp.concatenate([b_igate[l], b_fgate[l]]).reshape(2 * M_HEADS, 1)
        hm, ha = _mixers(proj.reshape(B, S, -1), g_rows, gate_b, mlstm_norm_g[l].reshape(1, -1),
                         attn_sinks[l], rel_bias, bucket)
        x2 = _post(l, x2, hm.reshape(T, -1), ha.reshape(T, -1), proj, wm, wa, wo,
                   norm_mlp_g[l].reshape(1, D), wu, wd,
                   final_norm_g.reshape(1, D), final=(l == DEPTH - 1))
    return x2.reshape(B, S, D)
```

```python
import functools

import numpy as np
import jax
import jax.numpy as jnp
from jax import lax
from jax.experimental import pallas as pl
from jax.experimental.pallas import tpu as pltpu

D_MODEL = 1024
DEPTH = 2
M_HEADS = 4
M_QK_DIM = 128
M_V_DIM = 256
M_QK_WIDTH = M_HEADS * M_QK_DIM
M_V_WIDTH = M_HEADS * M_V_DIM
CONV_K = 4
A_Q_HEADS = 16
A_KV_HEADS = 4
A_HEAD_DIM = 64
A_GROUP = A_Q_HEADS // A_KV_HEADS
A_Q_WIDTH = A_Q_HEADS * A_HEAD_DIM
A_KV_WIDTH = A_KV_HEADS * A_HEAD_DIM
WINDOW = 128
N_BUCKETS = 32
MAX_DISTANCE = 128
D_FF = 4 * D_MODEL
EPS = 1e-6
IN_SIZES = (2 * M_QK_WIDTH, M_V_WIDTH, M_V_WIDTH, M_HEADS, M_HEADS,
            A_Q_WIDTH, A_KV_WIDTH, A_KV_WIDTH, D_MODEL, D_MODEL)

INPROJ_TM = 512
INPROJ_SUB = 128
POST_TM = 512
PROJ_SUB = 256
REGROUP_COLS = 256
REGROUP_PIECE = 512
MLSTM_CHUNK = 128
MLSTM_BLOCKS = 2
CONV_HALO = 8
SWA_BLOCKS = 2
NEG_BIG = -1e30
LOG2E = 1.4426950408889634
VMEM_LIMIT_BYTES = 56 * 1024 * 1024

F32 = jnp.float32
BF16 = jnp.bfloat16


def _const_spec(shape):
    nd = len(shape)
    return pl.BlockSpec(shape, lambda *_: (0,) * nd, pipeline_mode=pl.Buffered(1))


def _layer_spec(layer, shape):
    nd = len(shape)
    return pl.BlockSpec((None,) + tuple(shape), lambda *_: (layer,) + (0,) * nd,
                        pipeline_mode=pl.Buffered(1))


def _rmsnorm(x, g):
    return x * lax.rsqrt(jnp.mean(x * x, axis=-1, keepdims=True) + EPS) * g


_INPROJ_GROUPS = (("qk", 2 * M_QK_WIDTH), ("v", M_V_WIDTH), ("o", M_V_WIDTH),
                  ("aq", A_Q_WIDTH), ("ak", A_KV_WIDTH), ("av", A_KV_WIDTH),
                  ("ga", D_MODEL), ("gb", D_MODEL), ("g", 128))
_INPROJ_OFFS = dict(zip((n for n, _ in _INPROJ_GROUPS),
                        np.cumsum([0] + [w for _, w in _INPROJ_GROUPS])[:-1].tolist()))
_INPROJ_WIDTH = sum(w for _, w in _INPROJ_GROUPS)
_PROJ_OUT = tuple((n, 2 * w if n in ("ak", "av") else w) for n, w in _INPROJ_GROUPS[:-1])
_PROJ_OUT_OFFS = dict(zip((n for n, _ in _PROJ_OUT),
                          np.cumsum([0] + [w for _, w in _PROJ_OUT])[:-1].tolist()))
_PROJ_OUT_WIDTH = sum(w for _, w in _PROJ_OUT)


def _inproj_kernel(x_ref, g_ref, cw_ref, cb_ref, w_ref, *refs, tiles_per_seq):
    out_ref, gates_ref, halo_ref = refs

    @pl.when(pl.program_id(0) % tiles_per_seq == 0)
    def _():
        halo_ref[...] = jnp.zeros_like(halo_ref)

    sub = INPROJ_SUB
    for t in range(x_ref.shape[0] // sub):
        _inproj_rows(slice(t * sub, (t + 1) * sub), x_ref, g_ref, cw_ref, cb_ref, w_ref,
                     out_ref, gates_ref, halo_ref)


def _inproj_rows(rows, x_ref, g_ref, cw_ref, cb_ref, w_ref, out_ref, gates_ref, halo_ref):
    n_rows = rows.stop - rows.start
    h = _rmsnorm(x_ref[rows, :], g_ref[...]).astype(BF16)
    res = jnp.dot(h, w_ref[...], preferred_element_type=F32)
    group = lambda name, width: res[:, _INPROJ_OFFS[name]:_INPROJ_OFFS[name] + width]

    head_row = lax.broadcasted_iota(jnp.int32, (CONV_HALO, 1), 0)
    gw = 2 * M_QK_WIDTH // 4
    for c in range(4):
        cols = slice(c * gw, (c + 1) * gw)
        r = res[:, cols]
        prev = halo_ref[:, cols]
        halo_ref[:, cols] = r[n_rows - CONV_HALO:, :]
        y = cb_ref[:, cols] + r * cw_ref[CONV_K - 1:CONV_K, cols]
        for s in range(1, CONV_K):
            rolled = pltpu.roll(r, s, axis=0)
            head = jnp.where(head_row < s, pltpu.roll(prev, s, axis=0), rolled[:CONV_HALO])
            shifted = jnp.concatenate([head, rolled[CONV_HALO:]], axis=0)
            y = y + shifted * cw_ref[CONV_K - 1 - s:CONV_K - s, cols]
        y = y * jax.nn.sigmoid(y)
        if c < 2:
            y = y * (M_QK_DIM ** -0.5)
        out_ref[rows, cols] = y.astype(BF16)

    for name, width in _INPROJ_GROUPS[1:-1]:
        val = group(name, width)
        if name == "o":
            val = jax.nn.sigmoid(val)
        if name in ("ak", "av"):
            half = A_HEAD_DIM
            swapped = [pltpu.roll(val[:, c:c + 2 * half], half, axis=1) for c in range(0, width, 2 * half)]
            val = jnp.concatenate([val] + swapped, axis=1)
        off = _PROJ_OUT_OFFS[name]
        out_ref[rows, off:off + val.shape[1]] = val.astype(BF16)
    gates_ref[:, rows] = group("g", 128).T[:2 * M_HEADS, :]


def _inproj(layer, x2, g, conv_w, conv_b, w_all, seq_len):
    T = x2.shape[0]
    tm = INPROJ_TM
    row = lambda n: pl.BlockSpec((tm, n), lambda i: (i, 0))
    return pl.pallas_call(
        functools.partial(_inproj_kernel, tiles_per_seq=seq_len // tm),
        grid=(T // tm,),
        in_specs=[row(D_MODEL), _const_spec((1, D_MODEL)), _const_spec((CONV_K, 2 * M_QK_WIDTH)),
                  _const_spec((1, 2 * M_QK_WIDTH)), _layer_spec(layer, (D_MODEL, _INPROJ_WIDTH))],
        out_specs=[row(_PROJ_OUT_WIDTH), pl.BlockSpec((2 * M_HEADS, tm), lambda i: (0, i))],
        out_shape=[jax.ShapeDtypeStruct((T, _PROJ_OUT_WIDTH), BF16),
                   jax.ShapeDtypeStruct((2 * M_HEADS, T), F32)],
        scratch_shapes=[pltpu.VMEM((CONV_HALO, 2 * M_QK_WIDTH), F32)],
        compiler_params=pltpu.CompilerParams(
            dimension_semantics=("arbitrary",), vmem_limit_bytes=VMEM_LIMIT_BYTES),
        name="inproj",
    )(x2, g, conv_w, conv_b, w_all)


def _mlstm_init(c_ref, m_ref):
    @pl.when(pl.program_id(0) == 0)
    def _():
        c_ref[...] = jnp.zeros_like(c_ref)
        m_ref[...] = jnp.zeros_like(m_ref)


def _mlstm_gates(blk, grow_ref, gb_ref, m_ref):
    L = MLSTM_CHUNK
    rows = slice(blk * L, (blk + 1) * L)
    row_i = lax.broadcasted_iota(jnp.int32, (L, L), 0)
    col_j = lax.broadcasted_iota(jnp.int32, (L, L), 1)
    triu_f = (row_i <= col_j).astype(F32)
    g = grow_ref[:, rows] + gb_ref[...]
    logf = pltpu.roll(jax.nn.log_sigmoid(g), M_HEADS, axis=0) * LOG2E
    b = jnp.dot(logf, triu_f, preferred_element_type=F32,
                precision=lax.Precision.HIGHEST)
    a = g * LOG2E - b
    m_prev = m_ref[...]
    m_last = jnp.maximum(m_prev, jnp.max(a, axis=-1, keepdims=True))
    wk = jnp.exp2(a - m_last)
    decay = jnp.exp2(m_prev - m_last)
    m_ref[...] = jnp.sum(logf, axis=-1, keepdims=True) + m_last
    return a, logf, m_prev, wk, decay


def _mlstm_chunk(blk, seqs, ng_ref, between):
    L = MLSTM_CHUNK
    rows = slice(blk * L, (blk + 1) * L)
    row_i = lax.broadcasted_iota(jnp.int32, (L, L), 0)
    col_j = lax.broadcasted_iota(jnp.int32, (L, L), 1)
    tril = col_j <= row_i
    eye_b = jnp.where(col_j == row_i, 1.0, 0.0).astype(BF16)
    ones_b = jnp.ones((L, 128), BF16)
    mean_b = jnp.full((M_V_DIM, 128), 1.0 / M_V_DIM, BF16)
    nt = (((1,), (1,)), ((), ()))
    units = [(seq, h) for seq in seqs for h in range(M_HEADS)]
    n_units = range(len(units))

    qb, v_ext, s_raw, k_t, m_prev = [], [], [], [], []
    for u, ((gates, qk_ref, v_ref, _, _, _), h) in enumerate(units):
        qb.append(qk_ref[rows, h * M_QK_DIM:(h + 1) * M_QK_DIM])
        kb = qk_ref[rows, M_QK_WIDTH + h * M_QK_DIM:M_QK_WIDTH + (h + 1) * M_QK_DIM]
        v_ext.append(jnp.concatenate([v_ref[rows, h * M_V_DIM:(h + 1) * M_V_DIM], ones_b], axis=1))
        st = lax.dot_general(jnp.concatenate([qb[u], eye_b], axis=0), kb, nt,
                             preferred_element_type=F32)
        s_raw.append(st[:L])
        k_t.append(st[L:])
        m_prev.append(gates[2][h:h + 1, 0:1])
    between[0]()

    both, floor = [], []
    zeros_b = jnp.zeros((M_QK_DIM, M_QK_DIM), BF16)
    for u, ((gates, _, _, _, _, c_ref), h) in enumerate(units):
        a, logf, _, wk, decay_rows = gates
        a_mat = jnp.where(tril, a[h:h + 1, :], -jnp.inf)
        m_row = jnp.maximum(m_prev[u], jnp.max(a_mat, axis=-1, keepdims=True))
        b_col = jnp.sum(jnp.where(tril, logf[h:h + 1, :], 0.0), axis=-1, keepdims=True)
        floor.append(jnp.exp2(-(b_col + m_row)))
        s = s_raw[u] * jnp.exp2(a_mat - m_row)
        q_inter = (qb[u].astype(F32) * jnp.exp2(m_prev[u] - m_row)).astype(BF16)
        kw_t = (k_t[u] * wk[h:h + 1, :]).astype(BF16)
        c_prev = c_ref[h]
        lhs = jnp.concatenate([jnp.concatenate([s.astype(BF16), q_inter], axis=1),
                               jnp.concatenate([kw_t, zeros_b], axis=1)], axis=0)
        res = jnp.dot(lhs, jnp.concatenate([v_ext[u], c_prev.astype(BF16)], axis=0),
                      preferred_element_type=F32)
        both.append(res[:L])
        c_ref[h] = decay_rows[h:h + 1, 0:1] * c_prev + res[L:]
    between[1]()

    hh = []
    for u in n_units:
        inv = 1.0 / jnp.maximum(jnp.abs(both[u][:, M_V_DIM:]), floor[u])
        hh.append(both[u][:, :M_V_DIM] * jnp.concatenate([inv, inv], axis=1))
    msq = [jnp.mean(x * x, axis=-1, keepdims=True) for x in hh]
    between[2]()

    for u, ((_, _, _, og_ref, out_ref, _), h) in enumerate(units):
        sl = slice(h * M_V_DIM, (h + 1) * M_V_DIM)
        hn = hh[u] * lax.rsqrt(msq[u] + EPS) * ng_ref[:, sl]
        out_ref[rows, sl] = hn.astype(BF16) * og_ref[rows, sl]
    between[3]()


def _swa_init(sink_ref, rel_ref, bucket_ref, bias_ref):
    W = WINDOW

    @pl.when(pl.program_id(0) == 0)
    def _():
        bucket = bucket_ref[...]
        neg = jnp.full((W, 2 * W), NEG_BIG, F32)
        for hq in range(A_Q_HEADS):
            bias_ref[1, hq] = neg

        def fill(bb, carry):
            hit = bucket == bb
            for hq in range(A_Q_HEADS):
                bias_ref[1, hq] = jnp.where(hit, rel_ref[bb, hq] * LOG2E, bias_ref[1, hq])
            return carry
        lax.fori_loop(0, N_BUCKETS, fill, 0)
        col = lax.broadcasted_iota(jnp.int32, (W, 2 * W), 1)
        for hq in range(A_Q_HEADS):
            sink = sink_ref[hq] * LOG2E
            bias_ref[0, hq] = jnp.where(col == 0, sink, jnp.where(col < W, NEG_BIG, bias_ref[1, hq]))
            bias_ref[1, hq] = jnp.where(col == 0, sink, bias_ref[1, hq])


def _swa_kv_head(blk, kv, q_ref, kp_ref, kc_ref, vp_ref, vc_ref, out_ref, bias_ref):
    W = WINDOW
    lane_lo = lax.broadcasted_iota(jnp.int32, (2 * W, 2 * A_HEAD_DIM), 1) < A_HEAD_DIM
    not_sink = lax.broadcasted_iota(jnp.int32, (2 * W, 2 * A_HEAD_DIM), 0) > 0
    keep_lo = lane_lo & not_sink
    keep_hi = jnp.logical_not(lane_lo) & not_sink
    zero = jnp.zeros((), BF16)
    ones_lo = jnp.where(lane_lo, 1.0, 0.0).astype(BF16)
    ones_hi = jnp.where(lane_lo, 0.0, 1.0).astype(BF16)
    nt = (((1,), (1,)), ((), ()))
    rows = slice(blk * W, (blk + 1) * W)
    var = jnp.minimum(pl.program_id(0), 1) if blk == 0 else 1
    def tile(ref, rws, swapped):
        c0 = (A_KV_WIDTH if swapped else 0) + (kv // 2) * 2 * A_HEAD_DIM
        return ref[rws, c0:c0 + 2 * A_HEAD_DIM]

    def prev_cur(p_ref, c_ref, swapped):
        prev = (tile(p_ref, slice(None), swapped) if blk == 0
                else tile(c_ref, slice((blk - 1) * W, blk * W), swapped))
        return jnp.concatenate([prev, tile(c_ref, rows, swapped)], axis=0)

    in_low = kv % 2 == 1
    k_even = jnp.where(keep_lo, prev_cur(kp_ref, kc_ref, in_low), zero)
    k_odd = jnp.where(keep_hi, prev_cur(kp_ref, kc_ref, not in_low), zero)
    v_even = jnp.concatenate([jnp.where(keep_lo, prev_cur(vp_ref, vc_ref, in_low), zero), ones_lo], axis=1)
    v_odd = jnp.concatenate([jnp.where(keep_hi, prev_cur(vp_ref, vc_ref, not in_low), zero), ones_hi], axis=1)
    v_both = jnp.concatenate([v_even, v_odd], axis=0)
    pairs = [kv * (A_GROUP // 2) + p for p in range(A_GROUP // 2)]
    cols = [slice(pi * 2 * A_HEAD_DIM, (pi + 1) * 2 * A_HEAD_DIM) for pi in pairs]
    q_st = jnp.concatenate([q_ref[rows, c] for c in cols], axis=0)
    bias_e = jnp.concatenate([bias_ref[var, 2 * pi] for pi in pairs], axis=0)
    bias_o = jnp.concatenate([bias_ref[var, 2 * pi + 1] for pi in pairs], axis=0)
    s_e = lax.dot_general(q_st, k_even, nt, preferred_element_type=F32) + bias_e
    s_o = lax.dot_general(q_st, k_odd, nt, preferred_element_type=F32) + bias_o
    p_e = jnp.exp2(s_e - jnp.max(s_e, axis=-1, keepdims=True)).astype(BF16)
    p_o = jnp.exp2(s_o - jnp.max(s_o, axis=-1, keepdims=True)).astype(BF16)
    acc = jnp.dot(jnp.concatenate([p_e, p_o], axis=1), v_both,
                  preferred_element_type=F32)
    out = (acc[:, :2 * A_HEAD_DIM] / acc[:, 2 * A_HEAD_DIM:]).astype(out_ref.dtype)
    for p, c in enumerate(cols):
        out_ref[rows, c] = out[p * W:(p + 1) * W]


def _mixers_kernel(sink_ref, rel_ref, bucket_ref, qk_ref, v_ref, og_ref, gb_ref, ng_ref,
                   q_ref, kp_ref, kc_ref, vp_ref, vc_ref, *refs):
    n_seq = qk_ref.shape[0]
    grow_refs = refs[:n_seq]
    hm_ref, ha_ref, c_ref, m_ref, bias_ref = refs[n_seq:]
    _mlstm_init(c_ref, m_ref)
    _swa_init(sink_ref, rel_ref, bucket_ref, bias_ref)
    for blk in range(SWA_BLOCKS):
        seqs = [(_mlstm_gates(blk, grow_refs[b], gb_ref, m_ref.at[b]),
                 qk_ref.at[b], v_ref.at[b], og_ref.at[b], hm_ref.at[b], c_ref.at[b])
                for b in range(n_seq)]
        swa = [functools.partial(_swa_kv_head, blk, u, q_ref.at[b], kp_ref.at[b], kc_ref.at[b],
                                 vp_ref.at[b], vc_ref.at[b], ha_ref.at[b], bias_ref)
               for b in range(n_seq) for u in range(A_KV_HEADS)]
        per_gap = len(swa) // 4
        _mlstm_chunk(blk, seqs, ng_ref,
                     between=[lambda g=g: [f() for f in swa[g * per_gap:(g + 1) * per_gap]]
                              for g in range(4)])


def _proj_cols(name):
    width = dict(_PROJ_OUT)[name]
    off = _PROJ_OUT_OFFS[name]
    assert off % width == 0
    return width, off // width


def _mixers(proj, g_rows, gate_b, norm_g, sinks, rel_bias, bucket):
    B, S, _ = proj.shape
    W = WINDOW
    tq = SWA_BLOCKS * W
    assert tq == MLSTM_BLOCKS * MLSTM_CHUNK

    def cur(name):
        n, j = _proj_cols(name)
        return pl.BlockSpec((B, tq, n), lambda i: (0, i, j))

    def prev(name):
        n, j = _proj_cols(name)
        return pl.BlockSpec((B, W, n), lambda i: (0, jnp.maximum(i * SWA_BLOCKS - 1, 0), j))

    out = lambda n: pl.BlockSpec((B, tq, n), lambda i: (0, i, 0))
    gate_rows = [pl.BlockSpec((2 * M_HEADS, tq), lambda i, b=b: (0, b * (S // tq) + i)) for b in range(B)]
    smem = pl.BlockSpec(memory_space=pltpu.SMEM)
    return pl.pallas_call(
        _mixers_kernel,
        grid=(S // tq,),
        in_specs=[smem, smem, _const_spec((W, 2 * W)),
                  cur("qk"), cur("v"), cur("o"),
                  _const_spec((2 * M_HEADS, 1)), _const_spec((1, M_V_WIDTH)),
                  cur("aq"), prev("ak"), cur("ak"), prev("av"), cur("av")] + gate_rows,
        out_specs=[out(M_V_WIDTH), out(A_Q_WIDTH)],
        out_shape=[jax.ShapeDtypeStruct((B, S, M_V_WIDTH), BF16),
                   jax.ShapeDtypeStruct((B, S, A_Q_WIDTH), BF16)],
        scratch_shapes=[pltpu.VMEM((B, M_HEADS, M_QK_DIM, M_V_DIM + 128), F32),
                        pltpu.VMEM((B, 2 * M_HEADS, MLSTM_CHUNK), F32),
                        pltpu.VMEM((2, A_Q_HEADS, W, 2 * W), F32)],
        compiler_params=pltpu.CompilerParams(
            dimension_semantics=("arbitrary",), vmem_limit_bytes=VMEM_LIMIT_BYTES),
        name="mixers",
    )(sinks, rel_bias, bucket, proj, proj, proj, gate_b, norm_g, proj, proj, proj, proj, proj,
      *([g_rows] * B))


def _post_kernel(x_ref, hm_ref, ha_ref, ga_ref, gb_ref, wm_ref, wa_ref, wo_ref,
                 g_ref, wu_ref, wd_ref, fg_ref, out_ref, *, final):
    sub = PROJ_SUB
    for t in range(x_ref.shape[0] // sub):
        rows = slice(t * sub, (t + 1) * sub)
        y = (jax.nn.sigmoid(ga_ref[rows, :].astype(F32))
             * jnp.dot(hm_ref[rows, :], wm_ref[...], preferred_element_type=F32)
             + jax.nn.sigmoid(gb_ref[rows, :].astype(F32))
             * jnp.dot(ha_ref[rows, :], wa_ref[...], preferred_element_type=F32))
        x = x_ref[rows, :] + jnp.dot(y.astype(BF16), wo_ref[...], preferred_element_type=F32)
        h = _rmsnorm(x, g_ref[...]).astype(BF16)
        u = jnp.maximum(jnp.dot(h, wu_ref[...], preferred_element_type=F32), 0.0)
        x = x + jnp.dot((u * u).astype(BF16), wd_ref[...], preferred_element_type=F32)
        if final:
            x = _rmsnorm(x, fg_ref[...])
        out_ref[rows, :] = x


def _post(layer, x2, hm, ha, proj, wm, wa, wo, g, wu, wd, fg, final):
    T = x2.shape[0]
    tm = POST_TM
    row = pl.BlockSpec((tm, D_MODEL), lambda i: (i, 0))
    gate = lambda name: pl.BlockSpec((tm, D_MODEL), lambda i: (i, _proj_cols(name)[1]))
    wspec = _layer_spec(layer, (D_MODEL, D_MODEL))
    vec = _const_spec((1, D_MODEL))
    return pl.pallas_call(
        functools.partial(_post_kernel, final=final),
        grid=(T // tm,),
        in_specs=[row, row, row, gate("ga"), gate("gb"), wspec, wspec, wspec,
                  vec, _layer_spec(layer, (D_MODEL, D_FF)), _layer_spec(layer, (D_FF, D_MODEL)), vec],
        out_specs=row,
        out_shape=jax.ShapeDtypeStruct((T, D_MODEL), F32),
        compiler_params=pltpu.CompilerParams(
            dimension_semantics=("parallel",), vmem_limit_bytes=VMEM_LIMIT_BYTES),
        name="post",
    )(x2, hm, ha, proj, proj, wm, wa, wo, g, wu, wd, fg)


def _t5_bucket(n):
    max_exact = N_BUCKETS // 2
    n = np.maximum(n, 0)
    large = max_exact + (np.log(np.maximum(n, 1) / max_exact)
                         / np.log(MAX_DISTANCE / max_exact)
                         * (N_BUCKETS - max_exact)).astype(np.int32)
    large = np.minimum(large, N_BUCKETS - 1)
    return np.where(n < max_exact, n, large).astype(np.int32)


def _bucket_table():
    W = WINDOW
    dist = np.arange(W)[:, None] + W - np.arange(2 * W)[None, :]
    valid = (dist >= 0) & (dist < W)
    return jnp.asarray(np.where(valid, _t5_bucket(dist), -1).astype(np.int32))


def _regroup_kernel(w_ref, out_ref):
    n_gate = 2 * M_HEADS
    gate0 = sum(IN_SIZES[:3])
    tail0 = gate0 + n_gate
    n_tail = sum(IN_SIZES[5:])
    step = REGROUP_PIECE
    for src, dst, n, scale in ((0, 0, gate0, 1.0),
                               (tail0, gate0, A_Q_WIDTH, A_HEAD_DIM ** -0.5 * LOG2E),
                               (tail0 + A_Q_WIDTH, gate0 + A_Q_WIDTH, n_tail - A_Q_WIDTH, 1.0)):
        for o in range(0, n, step):
            piece = w_ref[src + o:src + o + step, :]
            if scale != 1.0:
                piece = piece * scale
            out_ref[:, dst + o:dst + o + step] = piece.T.astype(BF16)
    gate_rows = jnp.concatenate(
        [w_ref[gate0:tail0, :], jnp.zeros((128 - n_gate, w_ref.shape[1]), F32)], axis=0)
    out_ref[:, gate0 + n_tail:] = gate_rows.T.astype(BF16)


def _regroup_w_in(w_in):
    w_t = jnp.swapaxes(w_in, 1, 2)
    depth, n_in, d = w_t.shape
    tc = REGROUP_COLS
    return pl.pallas_call(
        _regroup_kernel,
        grid=(depth, d // tc),
        in_specs=[pl.BlockSpec((None, n_in, tc), lambda l, i: (l, 0, i))],
        out_specs=pl.BlockSpec((None, tc, _INPROJ_WIDTH), lambda l, i: (l, i, 0)),
        out_shape=jax.ShapeDtypeStruct((depth, d, _INPROJ_WIDTH), BF16),
        compiler_params=pltpu.CompilerParams(
            dimension_semantics=("parallel", "parallel"), vmem_limit_bytes=VMEM_LIMIT_BYTES),
        name="regroup",
    )(w_t)


def kernel(x, norm_mix_g, w_in, conv_w, conv_b, b_igate, b_fgate, mlstm_norm_g, attn_sinks,
           rel_bias, w_branch_m, w_branch_a, w_out, norm_mlp_g, w_up, w_down, final_norm_g):
    B, S, D = x.shape
    T = B * S
    bucket = _bucket_table()
    w_all = _regroup_w_in(w_in)
    wm, wa, wo, wu, wd = (w.astype(BF16) for w in (w_branch_m, w_branch_a, w_out, w_up, w_down))
    x2 = x.reshape(T, D)
    for l in range(DEPTH):
        proj, g_rows = _inproj(
            l, x2, norm_mix_g[l].reshape(1, D), conv_w[l], conv_b[l].reshape(1, -1), w_all, S)
        gate_b = jnp.concatenate([b_igate[l], b_fgate[l]]).reshape(2 * M_HEADS, 1)
        hm, ha = _mixers(proj.reshape(B, S, -1), g_rows, gate_b, mlstm_norm_g[l].reshape(1, -1),
                         attn_sinks[l], rel_bias, bucket)
        x2 = _post(l, x2, hm.reshape(T, -1), ha.reshape(T, -1), proj, wm, wa, wo,
                   norm_mlp_g[l].reshape(1, D), wu, wd,
                   final_norm_g.reshape(1, D), final=(l == DEPTH - 1))
    return x2.reshape(B, S, D)
```

```python
import functools

import numpy as np
import jax
import jax.numpy as jnp
from jax import lax
from jax.experimental import pallas as pl
from jax.experimental.pallas import tpu as pltpu

D_MODEL = 1024
DEPTH = 2
M_HEADS = 4
M_QK_DIM = 128
M_V_DIM = 256
M_QK_WIDTH = M_HEADS * M_QK_DIM
M_V_WIDTH = M_HEADS * M_V_DIM
CONV_K = 4
A_Q_HEADS = 16
A_KV_HEADS = 4
A_HEAD_DIM = 64
A_GROUP = A_Q_HEADS // A_KV_HEADS
A_Q_WIDTH = A_Q_HEADS * A_HEAD_DIM
A_KV_WIDTH = A_KV_HEADS * A_HEAD_DIM
WINDOW = 128
N_BUCKETS = 32
MAX_DISTANCE = 128
D_FF = 4 * D_MODEL
EPS = 1e-6
IN_SIZES = (2 * M_QK_WIDTH, M_V_WIDTH, M_V_WIDTH, M_HEADS, M_HEADS,
            A_Q_WIDTH, A_KV_WIDTH, A_KV_WIDTH, D_MODEL, D_MODEL)

INPROJ_TM = 512
INPROJ_SUB = 128
POST_TM = 512
PROJ_SUB = 256
REGROUP_COLS = 256
REGROUP_PIECE = 512
MLSTM_CHUNK = 128
MLSTM_BLOCKS = 2
CONV_HALO = 8
SWA_BLOCKS = 2
NEG_BIG = -1e30
LOG2E = 1.4426950408889634
VMEM_LIMIT_BYTES = 56 * 1024 * 1024

F32 = jnp.float32
BF16 = jnp.bfloat16


def _const_spec(shape):
    nd = len(shape)
    return pl.BlockSpec(shape, lambda *_: (0,) * nd, pipeline_mode=pl.Buffered(1))


def _layer_spec(layer, shape):
    nd = len(shape)
    return pl.BlockSpec((None,) + tuple(shape), lambda *_: (layer,) + (0,) * nd,
                        pipeline_mode=pl.Buffered(1))


def _rmsnorm(x, g):
    return x * lax.rsqrt(jnp.mean(x * x, axis=-1, keepdims=True) + EPS) * g


_INPROJ_GROUPS = (("qk", 2 * M_QK_WIDTH), ("v", M_V_WIDTH), ("o", M_V_WIDTH),
                  ("aq", A_Q_WIDTH), ("ak", A_KV_WIDTH), ("av", A_KV_WIDTH),
                  ("ga", D_MODEL), ("gb", D_MODEL), ("g", 128))
_INPROJ_OFFS = dict(zip((n for n, _ in _INPROJ_GROUPS),
                        np.cumsum([0] + [w for _, w in _INPROJ_GROUPS])[:-1].tolist()))
_INPROJ_WIDTH = sum(w for _, w in _INPROJ_GROUPS)
_PROJ_OUT = tuple((n, 2 * w if n in ("ak", "av") else w) for n, w in _INPROJ_GROUPS[:-1])
_PROJ_OUT_OFFS = dict(zip((n for n, _ in _PROJ_OUT),
                          np.cumsum([0] + [w for _, w in _PROJ_OUT])[:-1].tolist()))
_PROJ_OUT_WIDTH = sum(w for _, w in _PROJ_OUT)


def _inproj_kernel(x_ref, g_ref, cw_ref, cb_ref, w_ref, *refs, tiles_per_seq):
    out_ref, gates_ref, halo_ref = refs

    @pl.when(pl.program_id(0) % tiles_per_seq == 0)
    def _():
        halo_ref[...] = jnp.zeros_like(halo_ref)

    sub = INPROJ_SUB
    for t in range(x_ref.shape[0] // sub):
        _inproj_rows(slice(t * sub, (t + 1) * sub), x_ref, g_ref, cw_ref, cb_ref, w_ref,
                     out_ref, gates_ref, halo_ref)


def _inproj_rows(rows, x_ref, g_ref, cw_ref, cb_ref, w_ref, out_ref, gates_ref, halo_ref):
    n_rows = rows.stop - rows.start
    h = _rmsnorm(x_ref[rows, :], g_ref[...]).astype(BF16)
    res = jnp.dot(h, w_ref[...], preferred_element_type=F32)
    group = lambda name, width: res[:, _INPROJ_OFFS[name]:_INPROJ_OFFS[name] + width]

    head_row = lax.broadcasted_iota(jnp.int32, (CONV_HALO, 1), 0)
    gw = 2 * M_QK_WIDTH // 4
    for c in range(4):
        cols = slice(c * gw, (c + 1) * gw)
        r = res[:, cols]
        prev = halo_ref[:, cols]
        halo_ref[:, cols] = r[n_rows - CONV_HALO:, :]
        y = cb_ref[:, cols] + r * cw_ref[CONV_K - 1:CONV_K, cols]
        for s in range(1, CONV_K):
            rolled = pltpu.roll(r, s, axis=0)
            head = jnp.where(head_row < s, pltpu.roll(prev, s, axis=0), rolled[:CONV_HALO])
            shifted = jnp.concatenate([head, rolled[CONV_HALO:]], axis=0)
            y = y + shifted * cw_ref[CONV_K - 1 - s:CONV_K - s, cols]
        y = y * jax.nn.sigmoid(y)
        if c < 2:
            y = y * (M_QK_DIM ** -0.5)
        out_ref[rows, cols] = y.astype(BF16)

    for name, width in _INPROJ_GROUPS[1:-1]:
        val = group(name, width)
        if name == "o":
            val = jax.nn.sigmoid(val)
        if name in ("ak", "av"):
            half = A_HEAD_DIM
            swapped = [pltpu.roll(val[:, c:c + 2 * half], half, axis=1) for c in range(0, width, 2 * half)]
            val = jnp.concatenate([val] + swapped, axis=1)
        off = _PROJ_OUT_OFFS[name]
        out_ref[rows, off:off + val.shape[1]] = val.astype(BF16)
    gates_ref[:, rows] = group("g", 128).T[:2 * M_HEADS, :]


def _inproj(layer, x2, g, conv_w, conv_b, w_all, seq_len):
    T = x2.shape[0]
    tm = INPROJ_TM
    row = lambda n: pl.BlockSpec((tm, n), lambda i: (i, 0))
    return pl.pallas_call(
        functools.partial(_inproj_kernel, tiles_per_seq=seq_len // tm),
        grid=(T // tm,),
        in_specs=[row(D_MODEL), _const_spec((1, D_MODEL)), _const_spec((CONV_K, 2 * M_QK_WIDTH)),
                  _const_spec((1, 2 * M_QK_WIDTH)), _layer_spec(layer, (D_MODEL, _INPROJ_WIDTH))],
        out_specs=[row(_PROJ_OUT_WIDTH), pl.BlockSpec((2 * M_HEADS, tm), lambda i: (0, i))],
        out_shape=[jax.ShapeDtypeStruct((T, _PROJ_OUT_WIDTH), BF16),
                   jax.ShapeDtypeStruct((2 * M_HEADS, T), F32)],
        scratch_shapes=[pltpu.VMEM((CONV_HALO, 2 * M_QK_WIDTH), F32)],
        compiler_params=pltpu.CompilerParams(
            dimension_semantics=("arbitrary",), vmem_limit_bytes=VMEM_LIMIT_BYTES),
        name="inproj",
    )(x2, g, conv_w, conv_b, w_all)


def _mlstm_init(c_ref, m_ref):
    @pl.when(pl.program_id(0) == 0)
    def _():
        c_ref[...] = jnp.zeros_like(c_ref)
        m_ref[...] = jnp.zeros_like(m_ref)


def _mlstm_gates(blk, grow_ref, gb_ref, m_ref):
    L = MLSTM_CHUNK
    rows = slice(blk * L, (blk + 1) * L)
    row_i = lax.broadcasted_iota(jnp.int32, (L, L), 0)
    col_j = lax.broadcasted_iota(jnp.int32, (L, L), 1)
    triu_f = (row_i <= col_j).astype(F32)
    g = grow_ref[:, rows] + gb_ref[...]
    logf = pltpu.roll(jax.nn.log_sigmoid(g), M_HEADS, axis=0) * LOG2E
    b = jnp.dot(logf, triu_f, preferred_element_type=F32,
                precision=lax.Precision.HIGHEST)
    a = g * LOG2E - b
    m_prev = m_ref[...]
    m_last = jnp.maximum(m_prev, jnp.max(a, axis=-1, keepdims=True))
    wk = jnp.exp2(a - m_last)
    decay = jnp.exp2(m_prev - m_last)
    m_ref[...] = jnp.sum(logf, axis=-1, keepdims=True) + m_last
    return a, logf, m_prev, wk, decay


def _mlstm_chunk(blk, seqs, ng_ref, between):
    L = MLSTM_CHUNK
    rows = slice(blk * L, (blk + 1) * L)
    row_i = lax.broadcasted_iota(jnp.int32, (L, L), 0)
    col_j = lax.broadcasted_iota(jnp.int32, (L, L), 1)
    tril = col_j <= row_i
    eye_b = jnp.where(col_j == row_i, 1.0, 0.0).astype(BF16)
    ones_b = jnp.ones((L, 128), BF16)
    mean_b = jnp.full((M_V_DIM, 128), 1.0 / M_V_DIM, BF16)
    nt = (((1,), (1,)), ((), ()))
    units = [(seq, h) for seq in seqs for h in range(M_HEADS)]
    n_units = range(len(units))

    qb, v_ext, s_raw, k_t, m_prev = [], [], [], [], []
    for u, ((gates, qk_ref, v_ref, _, _, _), h) in enumerate(units):
        qb.append(qk_ref[rows, h * M_QK_DIM:(h + 1) * M_QK_DIM])
        kb = qk_ref[rows, M_QK_WIDTH + h * M_QK_DIM:M_QK_WIDTH + (h + 1) * M_QK_DIM]
        v_ext.append(jnp.concatenate([v_ref[rows, h * M_V_DIM:(h + 1) * M_V_DIM], ones_b], axis=1))
        s_raw.append(lax.dot_general(qb[u], kb, nt, preferred_element_type=F32))
        k_t.append(kb.astype(F32).T)
        m_prev.append(gates[2][h:h + 1, 0:1])
    between[0]()

    both, floor = [], []
    zeros_b = jnp.zeros((M_QK_DIM, M_QK_DIM), BF16)
    for u, ((gates, _, _, _, _, c_ref), h) in enumerate(units):
        a, logf, _, wk, decay_rows = gates
        a_mat = jnp.where(tril, a[h:h + 1, :], -jnp.inf)
        m_row = jnp.maximum(m_prev[u], jnp.max(a_mat, axis=-1, keepdims=True))
        b_col = jnp.sum(jnp.where(tril, logf[h:h + 1, :], 0.0), axis=-1, keepdims=True)
        floor.append(jnp.exp2(-(b_col + m_row)))
        s = s_raw[u] * jnp.exp2(a_mat - m_row)
        q_inter = (qb[u].astype(F32) * jnp.exp2(m_prev[u] - m_row)).astype(BF16)
        kw_t = (k_t[u] * wk[h:h + 1, :]).astype(BF16)
        c_prev = c_ref[h]
        lhs = jnp.concatenate([jnp.concatenate([s.astype(BF16), q_inter], axis=1),
                               jnp.concatenate([kw_t, zeros_b], axis=1)], axis=0)
        res = jnp.dot(lhs, jnp.concatenate([v_ext[u], c_prev.astype(BF16)], axis=0),
                      preferred_element_type=F32)
        both.append(res[:L])
        c_ref[h] = decay_rows[h:h + 1, 0:1] * c_prev + res[L:]
    between[1]()

    hh = []
    for u in n_units:
        inv = 1.0 / jnp.maximum(jnp.abs(both[u][:, M_V_DIM:]), floor[u])
        hh.append(both[u][:, :M_V_DIM] * jnp.concatenate([inv, inv], axis=1))
    msq = [jnp.mean(x * x, axis=-1, keepdims=True) for x in hh]
    between[2]()

    for u, ((_, _, _, og_ref, out_ref, _), h) in enumerate(units):
        sl = slice(h * M_V_DIM, (h + 1) * M_V_DIM)
        hn = hh[u] * lax.rsqrt(msq[u] + EPS) * ng_ref[:, sl]
        out_ref[rows, sl] = hn.astype(BF16) * og_ref[rows, sl]
    between[3]()


def _swa_init(sink_ref, rel_ref, bucket_ref, bias_ref):
    W = WINDOW

    @pl.when(pl.program_id(0) == 0)
    def _():
        bucket = bucket_ref[...]
        neg = jnp.full((W, 2 * W), NEG_BIG, F32)
        for hq in range(A_Q_HEADS):
            bias_ref[1, hq] = neg

        def fill(bb, carry):
            hit = bucket == bb
            for hq in range(A_Q_HEADS):
                bias_ref[1, hq] = jnp.where(hit, rel_ref[bb, hq] * LOG2E, bias_ref[1, hq])
            return carry
        lax.fori_loop(0, N_BUCKETS, fill, 0)
        col = lax.broadcasted_iota(jnp.int32, (W, 2 * W), 1)
        for hq in range(A_Q_HEADS):
            sink = sink_ref[hq] * LOG2E
            bias_ref[0, hq] = jnp.where(col == 0, sink, jnp.where(col < W, NEG_BIG, bias_ref[1, hq]))
            bias_ref[1, hq] = jnp.where(col == 0, sink, bias_ref[1, hq])


def _swa_kv_head(blk, kv, q_ref, kp_ref, kc_ref, vp_ref, vc_ref, out_ref, bias_ref):
    W = WINDOW
    lane_lo = lax.broadcasted_iota(jnp.int32, (2 * W, 2 * A_HEAD_DIM), 1) < A_HEAD_DIM
    not_sink = lax.broadcasted_iota(jnp.int32, (2 * W, 2 * A_HEAD_DIM), 0) > 0
    keep_lo = lane_lo & not_sink
    keep_hi = jnp.logical_not(lane_lo) & not_sink
    zero = jnp.zeros((), BF16)
    ones_lo = jnp.where(lane_lo, 1.0, 0.0).astype(BF16)
    ones_hi = jnp.where(lane_lo, 0.0, 1.0).astype(BF16)
    nt = (((1,), (1,)), ((), ()))
    rows = slice(blk * W, (blk + 1) * W)
    var = jnp.minimum(pl.program_id(0), 1) if blk == 0 else 1
    def tile(ref, rws, swapped):
        c0 = (A_KV_WIDTH if swapped else 0) + (kv // 2) * 2 * A_HEAD_DIM
        return ref[rws, c0:c0 + 2 * A_HEAD_DIM]

    def prev_cur(p_ref, c_ref, swapped):
        prev = (tile(p_ref, slice(None), swapped) if blk == 0
                else tile(c_ref, slice((blk - 1) * W, blk * W), swapped))
        return jnp.concatenate([prev, tile(c_ref, rows, swapped)], axis=0)

    in_low = kv % 2 == 1
    k_even = jnp.where(keep_lo, prev_cur(kp_ref, kc_ref, in_low), zero)
    k_odd = jnp.where(keep_hi, prev_cur(kp_ref, kc_ref, not in_low), zero)
    v_even = jnp.concatenate([jnp.where(keep_lo, prev_cur(vp_ref, vc_ref, in_low), zero), ones_lo], axis=1)
    v_odd = jnp.concatenate([jnp.where(keep_hi, prev_cur(vp_ref, vc_ref, not in_low), zero), ones_hi], axis=1)
    v_both = jnp.concatenate([v_even, v_odd], axis=0)
    pairs = [kv * (A_GROUP // 2) + p for p in range(A_GROUP // 2)]
    cols = [slice(pi * 2 * A_HEAD_DIM, (pi + 1) * 2 * A_HEAD_DIM) for pi in pairs]
    q_st = jnp.concatenate([q_ref[rows, c] for c in cols], axis=0)
    bias_e = jnp.concatenate([bias_ref[var, 2 * pi] for pi in pairs], axis=0)
    bias_o = jnp.concatenate([bias_ref[var, 2 * pi + 1] for pi in pairs], axis=0)
    s_e = lax.dot_general(q_st, k_even, nt, preferred_element_type=F32) + bias_e
    s_o = lax.dot_general(q_st, k_odd, nt, preferred_element_type=F32) + bias_o
    p_e = jnp.exp2(s_e - jnp.max(s_e, axis=-1, keepdims=True)).astype(BF16)
    p_o = jnp.exp2(s_o - jnp.max(s_o, axis=-1, keepdims=True)).astype(BF16)
    acc = jnp.dot(jnp.concatenate([p_e, p_o], axis=1), v_both,
                  preferred_element_type=F32)
    out = (acc[:, :2 * A_HEAD_DIM] / acc[:, 2 * A_HEAD_DIM:]).astype(out_ref.dtype)
    for p, c in enumerate(cols):
        out_ref[rows, c] = out[p * W:(p + 1) * W]


def _mixers_kernel(sink_ref, rel_ref, bucket_ref, qk_ref, v_ref, og_ref, gb_ref, ng_ref,
                   q_ref, kp_ref, kc_ref, vp_ref, vc_ref, *refs):
    n_seq = qk_ref.shape[0]
    grow_refs = refs[:n_seq]
    hm_ref, ha_ref, c_ref, m_ref, bias_ref = refs[n_seq:]
    _mlstm_init(c_ref, m_ref)
    _swa_init(sink_ref, rel_ref, bucket_ref, bias_ref)
    for blk in range(SWA_BLOCKS):
        seqs = [(_mlstm_gates(blk, grow_refs[b], gb_ref, m_ref.at[b]),
                 qk_ref.at[b], v_ref.at[b], og_ref.at[b], hm_ref.at[b], c_ref.at[b])
                for b in range(n_seq)]
        swa = [functools.partial(_swa_kv_head, blk, u, q_ref.at[b], kp_ref.at[b], kc_ref.at[b],
                                 vp_ref.at[b], vc_ref.at[b], ha_ref.at[b], bias_ref)
               for b in range(n_seq) for u in range(A_KV_HEADS)]
        per_gap = len(swa) // 4
        _mlstm_chunk(blk, seqs, ng_ref,
                     between=[lambda g=g: [f() for f in swa[g * per_gap:(g + 1) * per_gap]]
                              for g in range(4)])


def _proj_cols(name):
    width = dict(_PROJ_OUT)[name]
    off = _PROJ_OUT_OFFS[name]
    assert off % width == 0
    return width, off // width


def _mixers(proj, g_rows, gate_b, norm_g, sinks, rel_bias, bucket):
    B, S, _ = proj.shape
    W = WINDOW
    tq = SWA_BLOCKS * W
    assert tq == MLSTM_BLOCKS * MLSTM_CHUNK

    def cur(name):
        n, j = _proj_cols(name)
        return pl.BlockSpec((B, tq, n), lambda i: (0, i, j))

    def prev(name):
        n, j = _proj_cols(name)
        return pl.BlockSpec((B, W, n), lambda i: (0, jnp.maximum(i * SWA_BLOCKS - 1, 0), j))

    out = lambda n: pl.BlockSpec((B, tq, n), lambda i: (0, i, 0))
    gate_rows = [pl.BlockSpec((2 * M_HEADS, tq), lambda i, b=b: (0, b * (S // tq) + i)) for b in range(B)]
    smem = pl.BlockSpec(memory_space=pltpu.SMEM)
    return pl.pallas_call(
        _mixers_kernel,
        grid=(S // tq,),
        in_specs=[smem, smem, _const_spec((W, 2 * W)),
                  cur("qk"), cur("v"), cur("o"),
                  _const_spec((2 * M_HEADS, 1)), _const_spec((1, M_V_WIDTH)),
                  cur("aq"), prev("ak"), cur("ak"), prev("av"), cur("av")] + gate_rows,
        out_specs=[out(M_V_WIDTH), out(A_Q_WIDTH)],
        out_shape=[jax.ShapeDtypeStruct((B, S, M_V_WIDTH), BF16),
                   jax.ShapeDtypeStruct((B, S, A_Q_WIDTH), BF16)],
        scratch_shapes=[pltpu.VMEM((B, M_HEADS, M_QK_DIM, M_V_DIM + 128), F32),
                        pltpu.VMEM((B, 2 * M_HEADS, MLSTM_CHUNK), F32),
                        pltpu.VMEM((2, A_Q_HEADS, W, 2 * W), F32)],
        compiler_params=pltpu.CompilerParams(
            dimension_semantics=("arbitrary",), vmem_limit_bytes=VMEM_LIMIT_BYTES),
        name="mixers",
    )(sinks, rel_bias, bucket, proj, proj, proj, gate_b, norm_g, proj, proj, proj, proj, proj,
      *([g_rows] * B))


def _post_kernel(x_ref, hm_ref, ha_ref, ga_ref, gb_ref, wm_ref, wa_ref, wo_ref,
                 g_ref, wu_ref, wd_ref, fg_ref, out_ref, *, final):
    sub = PROJ_SUB
    for t in range(x_ref.shape[0] // sub):
        rows = slice(t * sub, (t + 1) * sub)
        y = (jax.nn.sigmoid(ga_ref[rows, :].astype(F32))
             * jnp.dot(hm_ref[rows, :], wm_ref[...], preferred_element_type=F32)
             + jax.nn.sigmoid(gb_ref[rows, :].astype(F32))
             * jnp.dot(ha_ref[rows, :], wa_ref[...], preferred_element_type=F32))
        x = x_ref[rows, :] + jnp.dot(y.astype(BF16), wo_ref[...], preferred_element_type=F32)
        h = _rmsnorm(x, g_ref[...]).astype(BF16)
        u = jnp.maximum(jnp.dot(h, wu_ref[...], preferred_element_type=F32), 0.0)
        x = x + jnp.dot((u * u).astype(BF16), wd_ref[...], preferred_element_type=F32)
        if final:
            x = _rmsnorm(x, fg_ref[...])
        out_ref[rows, :] = x


def _post(layer, x2, hm, ha, proj, wm, wa, wo, g, wu, wd, fg, final):
    T = x2.shape[0]
    tm = POST_TM
    row = pl.BlockSpec((tm, D_MODEL), lambda i: (i, 0))
    gate = lambda name: pl.BlockSpec((tm, D_MODEL), lambda i: (i, _proj_cols(name)[1]))
    wspec = _layer_spec(layer, (D_MODEL, D_MODEL))
    vec = _const_spec((1, D_MODEL))
    return pl.pallas_call(
        functools.partial(_post_kernel, final=final),
        grid=(T // tm,),
        in_specs=[row, row, row, gate("ga"), gate("gb"), wspec, wspec, wspec,
                  vec, _layer_spec(layer, (D_MODEL, D_FF)), _layer_spec(layer, (D_FF, D_MODEL)), vec],
        out_specs=row,
        out_shape=jax.ShapeDtypeStruct((T, D_MODEL), F32),
        compiler_params=pltpu.CompilerParams(
            dimension_semantics=("parallel",), vmem_limit_bytes=VMEM_LIMIT_BYTES),
        name="post",
    )(x2, hm, ha, proj, proj, wm, wa, wo, g, wu, wd, fg)


def _t5_bucket(n):
    max_exact = N_BUCKETS // 2
    n = np.maximum(n, 0)
    large = max_exact + (np.log(np.maximum(n, 1) / max_exact)
                         / np.log(MAX_DISTANCE / max_exact)
                         * (N_BUCKETS - max_exact)).astype(np.int32)
    large = np.minimum(large, N_BUCKETS - 1)
    return np.where(n < max_exact, n, large).astype(np.int32)


def _bucket_table():
    W = WINDOW
    dist = np.arange(W)[:, None] + W - np.arange(2 * W)[None, :]
    valid = (dist >= 0) & (dist < W)
    return jnp.asarray(np.where(valid, _t5_bucket(dist), -1).astype(np.int32))


def _regroup_kernel(w_ref, out_ref):
    n_gate = 2 * M_HEADS
    gate0 = sum(IN_SIZES[:3])
    tail0 = gate0 + n_gate
    n_tail = sum(IN_SIZES[5:])
    step = REGROUP_PIECE
    for src, dst, n, scale in ((0, 0, gate0, 1.0),
                               (tail0, gate0, A_Q_WIDTH, A_HEAD_DIM ** -0.5 * LOG2E),
                               (tail0 + A_Q_WIDTH, gate0 + A_Q_WIDTH, n_tail - A_Q_WIDTH, 1.0)):
        for o in range(0, n, step):
            piece = w_ref[src + o:src + o + step, :]
            if scale != 1.0:
                piece = piece * scale
            out_ref[:, dst + o:dst + o + step] = piece.T.astype(BF16)
    gate_rows = jnp.concatenate(
        [w_ref[gate0:tail0, :], jnp.zeros((128 - n_gate, w_ref.shape[1]), F32)], axis=0)
    out_ref[:, gate0 + n_tail:] = gate_rows.T.astype(BF16)


def _regroup_w_in(w_in):
    w_t = jnp.swapaxes(w_in, 1, 2)
    depth, n_in, d = w_t.shape
    tc = REGROUP_COLS
    return pl.pallas_call(
        _regroup_kernel,
        grid=(depth, d // tc),
        in_specs=[pl.BlockSpec((None, n_in, tc), lambda l, i: (l, 0, i))],
        out_specs=pl.BlockSpec((None, tc, _INPROJ_WIDTH), lambda l, i: (l, i, 0)),
        out_shape=jax.ShapeDtypeStruct((depth, d, _INPROJ_WIDTH), BF16),
        compiler_params=pltpu.CompilerParams(
            dimension_semantics=("parallel", "parallel"), vmem_limit_bytes=VMEM_LIMIT_BYTES),
        name="regroup",
    )(w_t)


def kernel(x, norm_mix_g, w_in, conv_w, conv_b, b_igate, b_fgate, mlstm_norm_g, attn_sinks,
           rel_bias, w_branch_m, w_branch_a, w_out, norm_mlp_g, w_up, w_down, final_norm_g):
    B, S, D = x.shape
    T = B * S
    bucket = _bucket_table()
    w_all = _regroup_w_in(w_in)
    wm, wa, wo, wu, wd = (w.astype(BF16) for w in (w_branch_m, w_branch_a, w_out, w_up, w_down))
    x2 = x.reshape(T, D)
    for l in range(DEPTH):
        proj, g_rows = _inproj(
            l, x2, norm_mix_g[l].reshape(1, D), conv_w[l], conv_b[l].reshape(1, -1), w_all, S)
        gate_b = jnp.concatenate([b_igate[l], b_fgate[l]]).reshape(2 * M_HEADS, 1)
        hm, ha = _mixers(proj.reshape(B, S, -1), g_rows, gate_b, mlstm_norm_g[l].reshape(1, -1),
                         attn_sinks[l], rel_bias, bucket)
        x2 = _post(l, x2, hm.reshape(T, -1), ha.reshape(T, -1), proj, wm, wa, wo,
                   norm_mlp_g[l].reshape(1, D), wu, wd,
                   final_norm_g.reshape(1, D), final=(l == DEPTH - 1))
    return x2.reshape(B, S, D)
```

```python
import functools

import numpy as np
import jax
import jax.numpy as jnp
from jax import lax
from jax.experimental import pallas as pl
from jax.experimental.pallas import tpu as pltpu

D_MODEL = 1024
DEPTH = 2
M_HEADS = 4
M_QK_DIM = 128
M_V_DIM = 256
M_QK_WIDTH = M_HEADS * M_QK_DIM
M_V_WIDTH = M_HEADS * M_V_DIM
CONV_K = 4
A_Q_HEADS = 16
A_KV_HEADS = 4
A_HEAD_DIM = 64
A_GROUP = A_Q_HEADS // A_KV_HEADS
A_Q_WIDTH = A_Q_HEADS * A_HEAD_DIM
A_KV_WIDTH = A_KV_HEADS * A_HEAD_DIM
WINDOW = 128
N_BUCKETS = 32
MAX_DISTANCE = 128
D_FF = 4 * D_MODEL
EPS = 1e-6
IN_SIZES = (2 * M_QK_WIDTH, M_V_WIDTH, M_V_WIDTH, M_HEADS, M_HEADS,
            A_Q_WIDTH, A_KV_WIDTH, A_KV_WIDTH, D_MODEL, D_MODEL)

INPROJ_TM = 512
INPROJ_SUB = 128
POST_TM = 512
PROJ_SUB = 256
REGROUP_COLS = 256
REGROUP_PIECE = 512
MLSTM_CHUNK = 128
MLSTM_BLOCKS = 4
CONV_HALO = 8
SWA_BLOCKS = 4
SWA_UNITS_PER_GAP = (2, 2, 2, 2)
NEG_BIG = -1e30
LOG2E = 1.4426950408889634
VMEM_LIMIT_BYTES = 56 * 1024 * 1024

F32 = jnp.float32
BF16 = jnp.bfloat16


def _const_spec(shape):
    nd = len(shape)
    return pl.BlockSpec(shape, lambda *_: (0,) * nd, pipeline_mode=pl.Buffered(1))


def _layer_spec(layer, shape):
    nd = len(shape)
    return pl.BlockSpec((None,) + tuple(shape), lambda *_: (layer,) + (0,) * nd,
                        pipeline_mode=pl.Buffered(1))


def _rmsnorm(x, g):
    return x * lax.rsqrt(jnp.mean(x * x, axis=-1, keepdims=True) + EPS) * g


_INPROJ_GROUPS = (("qk", 2 * M_QK_WIDTH), ("v", M_V_WIDTH), ("o", M_V_WIDTH),
                  ("aq", A_Q_WIDTH), ("ak", A_KV_WIDTH), ("av", A_KV_WIDTH),
                  ("ga", D_MODEL), ("gb", D_MODEL), ("g", 128))
_INPROJ_OFFS = dict(zip((n for n, _ in _INPROJ_GROUPS),
                        np.cumsum([0] + [w for _, w in _INPROJ_GROUPS])[:-1].tolist()))
_INPROJ_WIDTH = sum(w for _, w in _INPROJ_GROUPS)
_PROJ_OUT = tuple((n, 2 * w if n in ("ak", "av") else w) for n, w in _INPROJ_GROUPS[:-1])
_PROJ_OUT_OFFS = dict(zip((n for n, _ in _PROJ_OUT),
                          np.cumsum([0] + [w for _, w in _PROJ_OUT])[:-1].tolist()))
_PROJ_OUT_WIDTH = sum(w for _, w in _PROJ_OUT)


def _inproj_kernel(x_ref, g_ref, cw_ref, cb_ref, w_ref, *refs, tiles_per_seq):
    out_ref, gates_ref, halo_ref = refs

    @pl.when(pl.program_id(0) % tiles_per_seq == 0)
    def _():
        halo_ref[...] = jnp.zeros_like(halo_ref)

    sub = INPROJ_SUB
    for t in range(x_ref.shape[0] // sub):
        _inproj_rows(slice(t * sub, (t + 1) * sub), x_ref, g_ref, cw_ref, cb_ref, w_ref,
                     out_ref, gates_ref, halo_ref)


def _inproj_rows(rows, x_ref, g_ref, cw_ref, cb_ref, w_ref, out_ref, gates_ref, halo_ref):
    n_rows = rows.stop - rows.start
    h = _rmsnorm(x_ref[rows, :], g_ref[...]).astype(BF16)
    res = jnp.dot(h, w_ref[...], preferred_element_type=F32)
    group = lambda name, width: res[:, _INPROJ_OFFS[name]:_INPROJ_OFFS[name] + width]

    head_row = lax.broadcasted_iota(jnp.int32, (CONV_HALO, 1), 0)
    gw = 2 * M_QK_WIDTH // 4
    for c in range(4):
        cols = slice(c * gw, (c + 1) * gw)
        r = res[:, cols]
        prev = halo_ref[:, cols]
        halo_ref[:, cols] = r[n_rows - CONV_HALO:, :]
        y = cb_ref[:, cols] + r * cw_ref[CONV_K - 1:CONV_K, cols]
        for s in range(1, CONV_K):
            rolled = pltpu.roll(r, s, axis=0)
            head = jnp.where(head_row < s, pltpu.roll(prev, s, axis=0), rolled[:CONV_HALO])
            shifted = jnp.concatenate([head, rolled[CONV_HALO:]], axis=0)
            y = y + shifted * cw_ref[CONV_K - 1 - s:CONV_K - s, cols]
        y = y * jax.nn.sigmoid(y)
        if c < 2:
            y = y * (M_QK_DIM ** -0.5)
        out_ref[rows, cols] = y.astype(BF16)

    for name, width in _INPROJ_GROUPS[1:-1]:
        val = group(name, width)
        if name == "o":
            val = jax.nn.sigmoid(val)
        if name in ("ak", "av"):
            half = A_HEAD_DIM
            swapped = [pltpu.roll(val[:, c:c + 2 * half], half, axis=1) for c in range(0, width, 2 * half)]
            val = jnp.concatenate([val] + swapped, axis=1)
        off = _PROJ_OUT_OFFS[name]
        out_ref[rows, off:off + val.shape[1]] = val.astype(BF16)
    gates_ref[:, rows] = group("g", 128).T[:2 * M_HEADS, :]


def _inproj(layer, x2, g, conv_w, conv_b, w_all, seq_len):
    T = x2.shape[0]
    tm = INPROJ_TM
    row = lambda n: pl.BlockSpec((tm, n), lambda i: (i, 0))
    return pl.pallas_call(
        functools.partial(_inproj_kernel, tiles_per_seq=seq_len // tm),
        grid=(T // tm,),
        in_specs=[row(D_MODEL), _const_spec((1, D_MODEL)), _const_spec((CONV_K, 2 * M_QK_WIDTH)),
                  _const_spec((1, 2 * M_QK_WIDTH)), _layer_spec(layer, (D_MODEL, _INPROJ_WIDTH))],
        out_specs=[row(_PROJ_OUT_WIDTH), pl.BlockSpec((2 * M_HEADS, tm), lambda i: (0, i))],
        out_shape=[jax.ShapeDtypeStruct((T, _PROJ_OUT_WIDTH), BF16),
                   jax.ShapeDtypeStruct((2 * M_HEADS, T), F32)],
        scratch_shapes=[pltpu.VMEM((CONV_HALO, 2 * M_QK_WIDTH), F32)],
        compiler_params=pltpu.CompilerParams(
            dimension_semantics=("arbitrary",), vmem_limit_bytes=VMEM_LIMIT_BYTES),
        name="inproj",
    )(x2, g, conv_w, conv_b, w_all)


def _mlstm_init(c_ref, m_ref):
    @pl.when(pl.program_id(0) == 0)
    def _():
        c_ref[...] = jnp.zeros_like(c_ref)
        m_ref[...] = jnp.zeros_like(m_ref)


def _mlstm_gates(blk, grow_ref, gb_ref, m_ref):
    L = MLSTM_CHUNK
    rows = slice(blk * L, (blk + 1) * L)
    row_i = lax.broadcasted_iota(jnp.int32, (L, L), 0)
    col_j = lax.broadcasted_iota(jnp.int32, (L, L), 1)
    triu_f = (row_i <= col_j).astype(F32)
    g = grow_ref[:, rows] + gb_ref[...]
    logf = pltpu.roll(jax.nn.log_sigmoid(g), M_HEADS, axis=0) * LOG2E
    b = jnp.dot(logf, triu_f, preferred_element_type=F32,
                precision=lax.Precision.HIGHEST)
    a = g * LOG2E - b
    m_prev = m_ref[...]
    m_last = jnp.maximum(m_prev, jnp.max(a, axis=-1, keepdims=True))
    wk = jnp.exp2(a - m_last)
    decay = jnp.exp2(m_prev - m_last)
    m_ref[...] = jnp.sum(logf, axis=-1, keepdims=True) + m_last
    return a, logf, m_prev, wk, decay


def _mlstm_chunk(blk, seqs, ng_ref, between):
    L = MLSTM_CHUNK
    rows = slice(blk * L, (blk + 1) * L)
    row_i = lax.broadcasted_iota(jnp.int32, (L, L), 0)
    col_j = lax.broadcasted_iota(jnp.int32, (L, L), 1)
    tril = col_j <= row_i
    eye_b = jnp.where(col_j == row_i, 1.0, 0.0).astype(BF16)
    ones_b = jnp.ones((L, 128), BF16)
    mean_b = jnp.full((M_V_DIM, 128), 1.0 / M_V_DIM, BF16)
    nt = (((1,), (1,)), ((), ()))
    units = [(seq, h) for seq in seqs for h in range(M_HEADS)]
    n_units = range(len(units))

    qb, v_ext, s_raw, k_t, m_prev = [], [], [], [], []
    for u, ((gates, qk_ref, v_ref, _, _, _), h) in enumerate(units):
        qb.append(qk_ref[rows, h * M_QK_DIM:(h + 1) * M_QK_DIM])
        kb = qk_ref[rows, M_QK_WIDTH + h * M_QK_DIM:M_QK_WIDTH + (h + 1) * M_QK_DIM]
        v_ext.append(jnp.concatenate([v_ref[rows, h * M_V_DIM:(h + 1) * M_V_DIM], ones_b], axis=1))
        s_raw.append(lax.dot_general(qb[u], kb, nt, preferred_element_type=F32))
        k_t.append(kb.astype(F32).T)
        m_prev.append(gates[2][h:h + 1, 0:1])
    between[0]()

    both, floor = [], []
    zeros_b = jnp.zeros((M_QK_DIM, M_QK_DIM), BF16)
    for u, ((gates, _, _, _, _, c_ref), h) in enumerate(units):
        a, logf, _, wk, decay_rows = gates
        a_mat = jnp.where(tril, a[h:h + 1, :], -jnp.inf)
        m_row = jnp.maximum(m_prev[u], jnp.max(a_mat, axis=-1, keepdims=True))
        b_col = jnp.sum(jnp.where(tril, logf[h:h + 1, :], 0.0), axis=-1, keepdims=True)
        floor.append(jnp.exp2(-(b_col + m_row)))
        s = s_raw[u] * jnp.exp2(a_mat - m_row)
        q_inter = (qb[u].astype(F32) * jnp.exp2(m_prev[u] - m_row)).astype(BF16)
        kw_t = (k_t[u] * wk[h:h + 1, :]).astype(BF16)
        c_prev = c_ref[h]
        lhs = jnp.concatenate([jnp.concatenate([s.astype(BF16), q_inter], axis=1),
                               jnp.concatenate([kw_t, zeros_b], axis=1)], axis=0)
        res = jnp.dot(lhs, jnp.concatenate([v_ext[u], c_prev.astype(BF16)], axis=0),
                      preferred_element_type=F32)
        both.append(res[:L])
        c_ref[h] = decay_rows[h:h + 1, 0:1] * c_prev + res[L:]
    between[1]()

    hh = []
    for u in n_units:
        inv = 1.0 / jnp.maximum(jnp.abs(both[u][:, M_V_DIM:]), floor[u])
        hh.append(both[u][:, :M_V_DIM] * jnp.concatenate([inv, inv], axis=1))
    msq = [jnp.mean(x * x, axis=-1, keepdims=True) for x in hh]
    between[2]()

    for u, ((_, _, _, og_ref, out_ref, _), h) in enumerate(units):
        sl = slice(h * M_V_DIM, (h + 1) * M_V_DIM)
        hn = hh[u] * lax.rsqrt(msq[u] + EPS) * ng_ref[:, sl]
        out_ref[rows, sl] = hn.astype(BF16) * og_ref[rows, sl]
    between[3]()


def _swa_init(sink_ref, rel_ref, bucket_ref, bias_ref):
    W = WINDOW

    @pl.when(pl.program_id(0) == 0)
    def _():
        bucket = bucket_ref[...]
        neg = jnp.full((W, 2 * W), NEG_BIG, F32)
        for hq in range(A_Q_HEADS):
            bias_ref[1, hq] = neg

        def fill(bb, carry):
            hit = bucket == bb
            for hq in range(A_Q_HEADS):
                bias_ref[1, hq] = jnp.where(hit, rel_ref[bb, hq] * LOG2E, bias_ref[1, hq])
            return carry
        lax.fori_loop(0, N_BUCKETS, fill, 0)
        col = lax.broadcasted_iota(jnp.int32, (W, 2 * W), 1)
        for hq in range(A_Q_HEADS):
            sink = sink_ref[hq] * LOG2E
            bias_ref[0, hq] = jnp.where(col == 0, sink, jnp.where(col < W, NEG_BIG, bias_ref[1, hq]))
            bias_ref[1, hq] = jnp.where(col == 0, sink, bias_ref[1, hq])


def _swa_kv_head(blk, kv, q_ref, kp_ref, kc_ref, vp_ref, vc_ref, out_ref, bias_ref):
    W = WINDOW
    lane_lo = lax.broadcasted_iota(jnp.int32, (2 * W, 2 * A_HEAD_DIM), 1) < A_HEAD_DIM
    not_sink = lax.broadcasted_iota(jnp.int32, (2 * W, 2 * A_HEAD_DIM), 0) > 0
    keep_lo = lane_lo & not_sink
    keep_hi = jnp.logical_not(lane_lo) & not_sink
    zero = jnp.zeros((), BF16)
    ones_lo = jnp.where(lane_lo, 1.0, 0.0).astype(BF16)
    ones_hi = jnp.where(lane_lo, 0.0, 1.0).astype(BF16)
    nt = (((1,), (1,)), ((), ()))
    rows = slice(blk * W, (blk + 1) * W)
    var = jnp.minimum(pl.program_id(0), 1) if blk == 0 else 1
    def tile(ref, rws, swapped):
        c0 = (A_KV_WIDTH if swapped else 0) + (kv // 2) * 2 * A_HEAD_DIM
        return ref[rws, c0:c0 + 2 * A_HEAD_DIM]

    def prev_cur(p_ref, c_ref, swapped):
        prev = (tile(p_ref, slice(None), swapped) if blk == 0
                else tile(c_ref, slice((blk - 1) * W, blk * W), swapped))
        return jnp.concatenate([prev, tile(c_ref, rows, swapped)], axis=0)

    in_low = kv % 2 == 1
    k_even = jnp.where(keep_lo, prev_cur(kp_ref, kc_ref, in_low), zero)
    k_odd = jnp.where(keep_hi, prev_cur(kp_ref, kc_ref, not in_low), zero)
    v_even = jnp.concatenate([jnp.where(keep_lo, prev_cur(vp_ref, vc_ref, in_low), zero), ones_lo], axis=1)
    v_odd = jnp.concatenate([jnp.where(keep_hi, prev_cur(vp_ref, vc_ref, not in_low), zero), ones_hi], axis=1)
    v_both = jnp.concatenate([v_even, v_odd], axis=0)
    pairs = [kv * (A_GROUP // 2) + p for p in range(A_GROUP // 2)]
    cols = [slice(pi * 2 * A_HEAD_DIM, (pi + 1) * 2 * A_HEAD_DIM) for pi in pairs]
    q_st = jnp.concatenate([q_ref[rows, c] for c in cols], axis=0)
    bias_e = jnp.concatenate([bias_ref[var, 2 * pi] for pi in pairs], axis=0)
    bias_o = jnp.concatenate([bias_ref[var, 2 * pi + 1] for pi in pairs], axis=0)
    s_e = lax.dot_general(q_st, k_even, nt, preferred_element_type=F32) + bias_e
    s_o = lax.dot_general(q_st, k_odd, nt, preferred_element_type=F32) + bias_o
    p_e = jnp.exp2(s_e - jnp.max(s_e, axis=-1, keepdims=True)).astype(BF16)
    p_o = jnp.exp2(s_o - jnp.max(s_o, axis=-1, keepdims=True)).astype(BF16)
    acc = jnp.dot(jnp.concatenate([p_e, p_o], axis=1), v_both,
                  preferred_element_type=F32)
    out = (acc[:, :2 * A_HEAD_DIM] / acc[:, 2 * A_HEAD_DIM:]).astype(out_ref.dtype)
    for p, c in enumerate(cols):
        out_ref[rows, c] = out[p * W:(p + 1) * W]


def _mixers_kernel(sink_ref, rel_ref, bucket_ref, qk_ref, v_ref, og_ref, gb_ref, ng_ref,
                   q_ref, kp_ref, kc_ref, vp_ref, vc_ref, *refs):
    n_seq = qk_ref.shape[0]
    grow_refs = refs[:n_seq]
    hm_ref, ha_ref, c_ref, m_ref, bias_ref = refs[n_seq:]
    _mlstm_init(c_ref, m_ref)
    _swa_init(sink_ref, rel_ref, bucket_ref, bias_ref)
    for blk in range(SWA_BLOCKS):
        seqs = [(_mlstm_gates(blk, grow_refs[b], gb_ref, m_ref.at[b]),
                 qk_ref.at[b], v_ref.at[b], og_ref.at[b], hm_ref.at[b], c_ref.at[b])
                for b in range(n_seq)]
        swa = [functools.partial(_swa_kv_head, blk, u, q_ref.at[b], kp_ref.at[b], kc_ref.at[b],
                                 vp_ref.at[b], vc_ref.at[b], ha_ref.at[b], bias_ref)
               for b in range(n_seq) for u in range(A_KV_HEADS)]
        cuts = np.cumsum((0,) + SWA_UNITS_PER_GAP)
        assert cuts[-1] == len(swa)
        _mlstm_chunk(blk, seqs, ng_ref,
                     between=[lambda g=g: [f() for f in swa[cuts[g]:cuts[g + 1]]] for g in range(4)])


def _proj_cols(name):
    width = dict(_PROJ_OUT)[name]
    off = _PROJ_OUT_OFFS[name]
    assert off % width == 0
    return width, off // width


def _mixers(proj, g_rows, gate_b, norm_g, sinks, rel_bias, bucket):
    B, S, _ = proj.shape
    W = WINDOW
    tq = SWA_BLOCKS * W
    assert tq == MLSTM_BLOCKS * MLSTM_CHUNK

    def cur(name):
        n, j = _proj_cols(name)
        return pl.BlockSpec((B, tq, n), lambda i: (0, i, j))

    def prev(name):
        n, j = _proj_cols(name)
        return pl.BlockSpec((B, W, n), lambda i: (0, jnp.maximum(i * SWA_BLOCKS - 1, 0), j))

    out = lambda n: pl.BlockSpec((B, tq, n), lambda i: (0, i, 0))
    gate_rows = [pl.BlockSpec((2 * M_HEADS, tq), lambda i, b=b: (0, b * (S // tq) + i)) for b in range(B)]
    smem = pl.BlockSpec(memory_space=pltpu.SMEM)
    return pl.pallas_call(
        _mixers_kernel,
        grid=(S // tq,),
        in_specs=[smem, smem, _const_spec((W, 2 * W)),
                  cur("qk"), cur("v"), cur("o"),
                  _const_spec((2 * M_HEADS, 1)), _const_spec((1, M_V_WIDTH)),
                  cur("aq"), prev("ak"), cur("ak"), prev("av"), cur("av")] + gate_rows,
        out_specs=[out(M_V_WIDTH), out(A_Q_WIDTH)],
        out_shape=[jax.ShapeDtypeStruct((B, S, M_V_WIDTH), BF16),
                   jax.ShapeDtypeStruct((B, S, A_Q_WIDTH), BF16)],
        scratch_shapes=[pltpu.VMEM((B, M_HEADS, M_QK_DIM, M_V_DIM + 128), F32),
                        pltpu.VMEM((B, 2 * M_HEADS, MLSTM_CHUNK), F32),
                        pltpu.VMEM((2, A_Q_HEADS, W, 2 * W), F32)],
        compiler_params=pltpu.CompilerParams(
            dimension_semantics=("arbitrary",), vmem_limit_bytes=VMEM_LIMIT_BYTES),
        name="mixers",
    )(sinks, rel_bias, bucket, proj, proj, proj, gate_b, norm_g, proj, proj, proj, proj, proj,
      *([g_rows] * B))


def _post_kernel(x_ref, hm_ref, ha_ref, ga_ref, gb_ref, wm_ref, wa_ref, wo_ref,
                 g_ref, wu_ref, wd_ref, fg_ref, out_ref, *, final):
    sub = PROJ_SUB
    for t in range(x_ref.shape[0] // sub):
        rows = slice(t * sub, (t + 1) * sub)
        y = (jax.nn.sigmoid(ga_ref[rows, :].astype(F32))
             * jnp.dot(hm_ref[rows, :], wm_ref[...], preferred_element_type=F32)
             + jax.nn.sigmoid(gb_ref[rows, :].astype(F32))
             * jnp.dot(ha_ref[rows, :], wa_ref[...], preferred_element_type=F32))
        x = x_ref[rows, :] + jnp.dot(y.astype(BF16), wo_ref[...], preferred_element_type=F32)
        h = _rmsnorm(x, g_ref[...]).astype(BF16)
        u = jnp.maximum(jnp.dot(h, wu_ref[...], preferred_element_type=F32), 0.0)
        x = x + jnp.dot((u * u).astype(BF16), wd_ref[...], preferred_element_type=F32)
        if final:
            x = _rmsnorm(x, fg_ref[...])
        out_ref[rows, :] = x


def _post(layer, x2, hm, ha, proj, wm, wa, wo, g, wu, wd, fg, final):
    T = x2.shape[0]
    tm = POST_TM
    row = pl.BlockSpec((tm, D_MODEL), lambda i: (i, 0))
    gate = lambda name: pl.BlockSpec((tm, D_MODEL), lambda i: (i, _proj_cols(name)[1]))
    wspec = _layer_spec(layer, (D_MODEL, D_MODEL))
    vec = _const_spec((1, D_MODEL))
    return pl.pallas_call(
        functools.partial(_post_kernel, final=final),
        grid=(T // tm,),
        in_specs=[row, row, row, gate("ga"), gate("gb"), wspec, wspec, wspec,
                  vec, _layer_spec(layer, (D_MODEL, D_FF)), _layer_spec(layer, (D_FF, D_MODEL)), vec],
        out_specs=row,
        out_shape=jax.ShapeDtypeStruct((T, D_MODEL), F32),
        compiler_params=pltpu.CompilerParams(
            dimension_semantics=("parallel",), vmem_limit_bytes=VMEM_LIMIT_BYTES),
        name="post",
    )(x2, hm, ha, proj, proj, wm, wa, wo, g, wu, wd, fg)


def _t5_bucket(n):
    max_exact = N_BUCKETS // 2
    n = np.maximum(n, 0)
    large = max_exact + (np.log(np.maximum(n, 1) / max_exact)
                         / np.log(MAX_DISTANCE / max_exact)
                         * (N_BUCKETS - max_exact)).astype(np.int32)
    large = np.minimum(large, N_BUCKETS - 1)
    return np.where(n < max_exact, n, large).astype(np.int32)


def _bucket_table():
    W = WINDOW
    dist = np.arange(W)[:, None] + W - np.arange(2 * W)[None, :]
    valid = (dist >= 0) & (dist < W)
    return jnp.asarray(np.where(valid, _t5_bucket(dist), -1).astype(np.int32))


def _regroup_kernel(w_ref, out_ref):
    n_gate = 2 * M_HEADS
    gate0 = sum(IN_SIZES[:3])
    tail0 = gate0 + n_gate
    n_tail = sum(IN_SIZES[5:])
    step = REGROUP_PIECE
    for src, dst, n, scale in ((0, 0, gate0, 1.0),
                               (tail0, gate0, A_Q_WIDTH, A_HEAD_DIM ** -0.5 * LOG2E),
                               (tail0 + A_Q_WIDTH, gate0 + A_Q_WIDTH, n_tail - A_Q_WIDTH, 1.0)):
        for o in range(0, n, step):
            piece = w_ref[src + o:src + o + step, :]
            if scale != 1.0:
                piece = piece * scale
            out_ref[:, dst + o:dst + o + step] = piece.T.astype(BF16)
    gate_rows = jnp.concatenate(
        [w_ref[gate0:tail0, :], jnp.zeros((128 - n_gate, w_ref.shape[1]), F32)], axis=0)
    out_ref[:, gate0 + n_tail:] = gate_rows.T.astype(BF16)


def _regroup_w_in(w_in):
    w_t = jnp.swapaxes(w_in, 1, 2)
    depth, n_in, d = w_t.shape
    tc = REGROUP_COLS
    return pl.pallas_call(
        _regroup_kernel,
        grid=(depth, d // tc),
        in_specs=[pl.BlockSpec((None, n_in, tc), lambda l, i: (l, 0, i))],
        out_specs=pl.BlockSpec((None, tc, _INPROJ_WIDTH), lambda l, i: (l, i, 0)),
        out_shape=jax.ShapeDtypeStruct((depth, d, _INPROJ_WIDTH), BF16),
        compiler_params=pltpu.CompilerParams(
            dimension_semantics=("parallel", "parallel"), vmem_limit_bytes=VMEM_LIMIT_BYTES),
        name="regroup",
    )(w_t)


def kernel(x, norm_mix_g, w_in, conv_w, conv_b, b_igate, b_fgate, mlstm_norm_g, attn_sinks,
           rel_bias, w_branch_m, w_branch_a, w_out, norm_mlp_g, w_up, w_down, final_norm_g):
    B, S, D = x.shape
    T = B * S
    bucket = _bucket_table()
    w_all = _regroup_w_in(w_in)
    wm, wa, wo, wu, wd = (w.astype(BF16) for w in (w_branch_m, w_branch_a, w_out, w_up, w_down))
    x2 = x.reshape(T, D)
    for l in range(DEPTH):
        proj, g_rows = _inproj(
            l, x2, norm_mix_g[l].reshape(1, D), conv_w[l], conv_b[l].reshape(1, -1), w_all, S)
        gate_b = jnp.concatenate([b_igate[l], b_fgate[l]]).reshape(2 * M_HEADS, 1)
        hm, ha = _mixers(proj.reshape(B, S, -1), g_rows, gate_b, mlstm_norm_g[l].reshape(1, -1),
                         attn_sinks[l], rel_bias, bucket)
        x2 = _post(l, x2, hm.reshape(T, -1), ha.reshape(T, -1), proj, wm, wa, wo,
                   norm_mlp_g[l].reshape(1, D), wu, wd,
                   final_norm_g.reshape(1, D), final=(l == DEPTH - 1))
    return x2.reshape(B, S, D)
```

```python
import functools

import numpy as np
import jax
import jax.numpy as jnp
from jax import lax
from jax.experimental import pallas as pl
from jax.experimental.pallas import tpu as pltpu

D_MODEL = 1024
DEPTH = 2
M_HEADS = 4
M_QK_DIM = 128
M_V_DIM = 256
M_QK_WIDTH = M_HEADS * M_QK_DIM
M_V_WIDTH = M_HEADS * M_V_DIM
CONV_K = 4
A_Q_HEADS = 16
A_KV_HEADS = 4
A_HEAD_DIM = 64
A_GROUP = A_Q_HEADS // A_KV_HEADS
A_Q_WIDTH = A_Q_HEADS * A_HEAD_DIM
A_KV_WIDTH = A_KV_HEADS * A_HEAD_DIM
WINDOW = 128
N_BUCKETS = 32
MAX_DISTANCE = 128
D_FF = 4 * D_MODEL
EPS = 1e-6
IN_SIZES = (2 * M_QK_WIDTH, M_V_WIDTH, M_V_WIDTH, M_HEADS, M_HEADS,
            A_Q_WIDTH, A_KV_WIDTH, A_KV_WIDTH, D_MODEL, D_MODEL)

INPROJ_TM = 512
INPROJ_SUB = 128
POST_TM = 512
PROJ_SUB = 256
REGROUP_COLS = 256
REGROUP_PIECE = 512
MLSTM_CHUNK = 128
MLSTM_BLOCKS = 4
CONV_HALO = 8
SWA_BLOCKS = 4
SWA_UNITS_PER_GAP = (2, 2, 2, 2)
NEG_BIG = -1e30
LOG2E = 1.4426950408889634
LANES = 128
V7X_VMEM_BYTES = 64 * 1024 * 1024
VMEM_LIMIT_BYTES = V7X_VMEM_BYTES - 8 * 1024 * 1024

F32 = jnp.float32
BF16 = jnp.bfloat16


def _const_spec(shape):
    nd = len(shape)
    return pl.BlockSpec(shape, lambda *_: (0,) * nd, pipeline_mode=pl.Buffered(1))


def _layer_spec(layer, shape):
    nd = len(shape)
    return pl.BlockSpec((None,) + tuple(shape), lambda *_: (layer,) + (0,) * nd,
                        pipeline_mode=pl.Buffered(1))


def _rmsnorm(x, g):
    return x * lax.rsqrt(jnp.mean(x * x, axis=-1, keepdims=True) + EPS) * g


_INPROJ_GROUPS = (("qk", 2 * M_QK_WIDTH), ("v", M_V_WIDTH), ("o", M_V_WIDTH),
                  ("aq", A_Q_WIDTH), ("ak", A_KV_WIDTH), ("av", A_KV_WIDTH),
                  ("ga", D_MODEL), ("gb", D_MODEL), ("g", LANES))
_INPROJ_OFFS = dict(zip((n for n, _ in _INPROJ_GROUPS),
                        np.cumsum([0] + [w for _, w in _INPROJ_GROUPS])[:-1].tolist()))
_INPROJ_WIDTH = sum(w for _, w in _INPROJ_GROUPS)
_PROJ_OUT = tuple((n, 2 * w if n in ("ak", "av") else w) for n, w in _INPROJ_GROUPS[:-1])
_PROJ_OUT_OFFS = dict(zip((n for n, _ in _PROJ_OUT),
                          np.cumsum([0] + [w for _, w in _PROJ_OUT])[:-1].tolist()))
_PROJ_OUT_WIDTH = sum(w for _, w in _PROJ_OUT)


def _inproj_kernel(x_ref, g_ref, cw_ref, cb_ref, w_ref, out_ref, gates_ref, halo_ref, *, tiles_per_seq):
    @pl.when(pl.program_id(0) % tiles_per_seq == 0)
    def _():
        halo_ref[...] = jnp.zeros_like(halo_ref)

    sub = INPROJ_SUB
    for t in range(x_ref.shape[0] // sub):
        _inproj_rows(slice(t * sub, (t + 1) * sub), x_ref, g_ref, cw_ref, cb_ref, w_ref,
                     out_ref, gates_ref, halo_ref)


def _inproj_rows(rows, x_ref, g_ref, cw_ref, cb_ref, w_ref, out_ref, gates_ref, halo_ref):
    n_rows = rows.stop - rows.start
    h = _rmsnorm(x_ref[rows, :], g_ref[...]).astype(BF16)
    res = jnp.dot(h, w_ref[...], preferred_element_type=F32)
    group = lambda name, width: res[:, _INPROJ_OFFS[name]:_INPROJ_OFFS[name] + width]

    head_row = lax.broadcasted_iota(jnp.int32, (CONV_HALO, 1), 0)
    gw = 2 * M_QK_WIDTH // 4
    for c in range(4):
        cols = slice(c * gw, (c + 1) * gw)
        r = res[:, cols]
        prev = halo_ref[:, cols]
        halo_ref[:, cols] = r[n_rows - CONV_HALO:, :]
        y = cb_ref[:, cols] + r * cw_ref[CONV_K - 1:CONV_K, cols]
        for s in range(1, CONV_K):
            rolled = pltpu.roll(r, s, axis=0)
            head = jnp.where(head_row < s, pltpu.roll(prev, s, axis=0), rolled[:CONV_HALO])
            shifted = jnp.concatenate([head, rolled[CONV_HALO:]], axis=0)
            y = y + shifted * cw_ref[CONV_K - 1 - s:CONV_K - s, cols]
        y = y * jax.nn.sigmoid(y)
        if c < 2:
            y = y * (M_QK_DIM ** -0.5)
        out_ref[rows, cols] = y.astype(BF16)

    for name, width in _INPROJ_GROUPS[1:-1]:
        val = group(name, width)
        if name == "o":
            val = jax.nn.sigmoid(val)
        if name in ("ak", "av"):
            half = A_HEAD_DIM
            swapped = [pltpu.roll(val[:, c:c + 2 * half], half, axis=1) for c in range(0, width, 2 * half)]
            val = jnp.concatenate([val] + swapped, axis=1)
        off = _PROJ_OUT_OFFS[name]
        out_ref[rows, off:off + val.shape[1]] = val.astype(BF16)
    gates_ref[:, rows] = group("g", LANES).T[:2 * M_HEADS, :]


def _inproj(layer, x2, g, conv_w, conv_b, w_all, seq_len):
    T = x2.shape[0]
    tm = INPROJ_TM
    row = lambda n: pl.BlockSpec((tm, n), lambda i: (i, 0))
    return pl.pallas_call(
        functools.partial(_inproj_kernel, tiles_per_seq=seq_len // tm),
        grid=(T // tm,),
        in_specs=[row(D_MODEL), _const_spec((1, D_MODEL)), _const_spec((CONV_K, 2 * M_QK_WIDTH)),
                  _const_spec((1, 2 * M_QK_WIDTH)), _layer_spec(layer, (D_MODEL, _INPROJ_WIDTH))],
        out_specs=[row(_PROJ_OUT_WIDTH), pl.BlockSpec((2 * M_HEADS, tm), lambda i: (0, i))],
        out_shape=[jax.ShapeDtypeStruct((T, _PROJ_OUT_WIDTH), BF16),
                   jax.ShapeDtypeStruct((2 * M_HEADS, T), F32)],
        scratch_shapes=[pltpu.VMEM((CONV_HALO, 2 * M_QK_WIDTH), F32)],
        compiler_params=pltpu.CompilerParams(
            dimension_semantics=("arbitrary",), vmem_limit_bytes=VMEM_LIMIT_BYTES),
        name="inproj",
    )(x2, g, conv_w, conv_b, w_all)


def _mlstm_init(c_ref, m_ref):
    @pl.when(pl.program_id(0) == 0)
    def _():
        c_ref[...] = jnp.zeros_like(c_ref)
        m_ref[...] = jnp.zeros_like(m_ref)


def _mlstm_gates(blk, grow_ref, gb_ref, m_ref):
    L = MLSTM_CHUNK
    rows = slice(blk * L, (blk + 1) * L)
    row_i = lax.broadcasted_iota(jnp.int32, (L, L), 0)
    col_j = lax.broadcasted_iota(jnp.int32, (L, L), 1)
    triu_f = (row_i <= col_j).astype(F32)
    g = grow_ref[:, rows] + gb_ref[...]
    logf = pltpu.roll(jax.nn.log_sigmoid(g), M_HEADS, axis=0) * LOG2E
    b = jnp.dot(logf, triu_f, preferred_element_type=F32,
                precision=lax.Precision.HIGHEST)
    a = g * LOG2E - b
    m_prev = m_ref[...]
    m_last = jnp.maximum(m_prev, jnp.max(a, axis=-1, keepdims=True))
    wk = jnp.exp2(a - m_last)
    decay = jnp.exp2(m_prev - m_last)
    m_ref[...] = jnp.sum(logf, axis=-1, keepdims=True) + m_last
    return a, logf, m_prev, wk, decay


def _mlstm_chunk(blk, seqs, ng_ref, between):
    L = MLSTM_CHUNK
    rows = slice(blk * L, (blk + 1) * L)
    row_i = lax.broadcasted_iota(jnp.int32, (L, L), 0)
    col_j = lax.broadcasted_iota(jnp.int32, (L, L), 1)
    tril = col_j <= row_i
    ones_b = jnp.ones((L, LANES), BF16)
    nt = (((1,), (1,)), ((), ()))
    units = [(seq, h) for seq in seqs for h in range(M_HEADS)]
    n_units = range(len(units))

    qb, v_ext, s_raw, k_t, m_prev = [], [], [], [], []
    for u, ((gates, qk_ref, v_ref, _, _, _), h) in enumerate(units):
        qb.append(qk_ref[rows, h * M_QK_DIM:(h + 1) * M_QK_DIM])
        kb = qk_ref[rows, M_QK_WIDTH + h * M_QK_DIM:M_QK_WIDTH + (h + 1) * M_QK_DIM]
        v_ext.append(jnp.concatenate([v_ref[rows, h * M_V_DIM:(h + 1) * M_V_DIM], ones_b], axis=1))
        s_raw.append(lax.dot_general(qb[u], kb, nt, preferred_element_type=F32))
        k_t.append(kb.astype(F32).T)
        m_prev.append(gates[2][h:h + 1, 0:1])
    between[0]()

    both, floor = [], []
    zeros_b = jnp.zeros((M_QK_DIM, M_QK_DIM), BF16)
    for u, ((gates, _, _, _, _, c_ref), h) in enumerate(units):
        a, logf, _, wk, decay_rows = gates
        a_mat = jnp.where(tril, a[h:h + 1, :], -jnp.inf)
        m_row = jnp.maximum(m_prev[u], jnp.max(a_mat, axis=-1, keepdims=True))
        b_col = jnp.sum(jnp.where(tril, logf[h:h + 1, :], 0.0), axis=-1, keepdims=True)
        floor.append(jnp.exp2(-(b_col + m_row)))
        s = s_raw[u] * jnp.exp2(a_mat - m_row)
        q_inter = (qb[u].astype(F32) * jnp.exp2(m_prev[u] - m_row)).astype(BF16)
        kw_t = (k_t[u] * wk[h:h + 1, :]).astype(BF16)
        c_prev = c_ref[h]
        lhs = jnp.concatenate([jnp.concatenate([s.astype(BF16), q_inter], axis=1),
                               jnp.concatenate([kw_t, zeros_b], axis=1)], axis=0)
        res = jnp.dot(lhs, jnp.concatenate([v_ext[u], c_prev.astype(BF16)], axis=0),
                      preferred_element_type=F32)
        both.append(res[:L])
        c_ref[h] = decay_rows[h:h + 1, 0:1] * c_prev + res[L:]
    between[1]()

    hh = []
    for u in n_units:
        inv = 1.0 / jnp.maximum(jnp.abs(both[u][:, M_V_DIM:]), floor[u])
        hh.append(both[u][:, :M_V_DIM] * jnp.concatenate([inv, inv], axis=1))
    msq = [jnp.mean(x * x, axis=-1, keepdims=True) for x in hh]
    between[2]()

    for u, ((_, _, _, og_ref, out_ref, _), h) in enumerate(units):
        sl = slice(h * M_V_DIM, (h + 1) * M_V_DIM)
        hn = hh[u] * lax.rsqrt(msq[u] + EPS) * ng_ref[:, sl]
        out_ref[rows, sl] = hn.astype(BF16) * og_ref[rows, sl]
    between[3]()


def _swa_init(sink_ref, rel_ref, bucket_ref, bias_ref):
    W = WINDOW

    @pl.when(pl.program_id(0) == 0)
    def _():
        bucket = bucket_ref[...]
        neg = jnp.full((W, 2 * W), NEG_BIG, F32)
        for hq in range(A_Q_HEADS):
            bias_ref[1, hq] = neg

        def fill(bb, carry):
            hit = bucket == bb
            for hq in range(A_Q_HEADS):
                bias_ref[1, hq] = jnp.where(hit, rel_ref[bb, hq] * LOG2E, bias_ref[1, hq])
            return carry
        lax.fori_loop(0, N_BUCKETS, fill, 0)
        col = lax.broadcasted_iota(jnp.int32, (W, 2 * W), 1)
        for hq in range(A_Q_HEADS):
            sink = sink_ref[hq] * LOG2E
            bias_ref[0, hq] = jnp.where(col == 0, sink, jnp.where(col < W, NEG_BIG, bias_ref[1, hq]))
            bias_ref[1, hq] = jnp.where(col == 0, sink, bias_ref[1, hq])


def _swa_kv_head(blk, kv, q_ref, kp_ref, kc_ref, vp_ref, vc_ref, out_ref, bias_ref):
    W = WINDOW
    lane_lo = lax.broadcasted_iota(jnp.int32, (2 * W, 2 * A_HEAD_DIM), 1) < A_HEAD_DIM
    not_sink = lax.broadcasted_iota(jnp.int32, (2 * W, 2 * A_HEAD_DIM), 0) > 0
    keep_lo = lane_lo & not_sink
    keep_hi = jnp.logical_not(lane_lo) & not_sink
    zero = jnp.zeros((), BF16)
    ones_lo = jnp.where(lane_lo, 1.0, 0.0).astype(BF16)
    ones_hi = jnp.where(lane_lo, 0.0, 1.0).astype(BF16)
    nt = (((1,), (1,)), ((), ()))
    rows = slice(blk * W, (blk + 1) * W)
    var = jnp.minimum(pl.program_id(0), 1) if blk == 0 else 1
    def tile(ref, rws, swapped):
        c0 = (A_KV_WIDTH if swapped else 0) + (kv // 2) * 2 * A_HEAD_DIM
        return ref[rws, c0:c0 + 2 * A_HEAD_DIM]

    def prev_cur(p_ref, c_ref, swapped):
        prev = (tile(p_ref, slice(None), swapped) if blk == 0
                else tile(c_ref, slice((blk - 1) * W, blk * W), swapped))
        return jnp.concatenate([prev, tile(c_ref, rows, swapped)], axis=0)

    in_low = kv % 2 == 1
    k_even = jnp.where(keep_lo, prev_cur(kp_ref, kc_ref, in_low), zero)
    k_odd = jnp.where(keep_hi, prev_cur(kp_ref, kc_ref, not in_low), zero)
    v_even = jnp.concatenate([jnp.where(keep_lo, prev_cur(vp_ref, vc_ref, in_low), zero), ones_lo], axis=1)
    v_odd = jnp.concatenate([jnp.where(keep_hi, prev_cur(vp_ref, vc_ref, not in_low), zero), ones_hi], axis=1)
    v_both = jnp.concatenate([v_even, v_odd], axis=0)
    pairs = [kv * (A_GROUP // 2) + p for p in range(A_GROUP // 2)]
    cols = [slice(pi * 2 * A_HEAD_DIM, (pi + 1) * 2 * A_HEAD_DIM) for pi in pairs]
    q_st = jnp.concatenate([q_ref[rows, c] for c in cols], axis=0)
    bias_e = jnp.concatenate([bias_ref[var, 2 * pi] for pi in pairs], axis=0)
    bias_o = jnp.concatenate([bias_ref[var, 2 * pi + 1] for pi in pairs], axis=0)
    s_e = lax.dot_general(q_st, k_even, nt, preferred_element_type=F32) + bias_e
    s_o = lax.dot_general(q_st, k_odd, nt, preferred_element_type=F32) + bias_o
    p_e = jnp.exp2(s_e - jnp.max(s_e, axis=-1, keepdims=True)).astype(BF16)
    p_o = jnp.exp2(s_o - jnp.max(s_o, axis=-1, keepdims=True)).astype(BF16)
    acc = jnp.dot(jnp.concatenate([p_e, p_o], axis=1), v_both,
                  preferred_element_type=F32)
    out = (acc[:, :2 * A_HEAD_DIM] / acc[:, 2 * A_HEAD_DIM:]).astype(out_ref.dtype)
    for p, c in enumerate(cols):
        out_ref[rows, c] = out[p * W:(p + 1) * W]


def _mixers_kernel(sink_ref, rel_ref, bucket_ref, qk_ref, v_ref, og_ref, gb_ref, ng_ref,
                   q_ref, kp_ref, kc_ref, vp_ref, vc_ref, *refs):
    n_seq = qk_ref.shape[0]
    grow_refs = refs[:n_seq]
    hm_ref, ha_ref, c_ref, m_ref, bias_ref = refs[n_seq:]
    _mlstm_init(c_ref, m_ref)
    _swa_init(sink_ref, rel_ref, bucket_ref, bias_ref)
    for blk in range(SWA_BLOCKS):
        seqs = [(_mlstm_gates(blk, grow_refs[b], gb_ref, m_ref.at[b]),
                 qk_ref.at[b], v_ref.at[b], og_ref.at[b], hm_ref.at[b], c_ref.at[b])
                for b in range(n_seq)]
        swa = [functools.partial(_swa_kv_head, blk, u, q_ref.at[b], kp_ref.at[b], kc_ref.at[b],
                                 vp_ref.at[b], vc_ref.at[b], ha_ref.at[b], bias_ref)
               for b in range(n_seq) for u in range(A_KV_HEADS)]
        cuts = np.cumsum((0,) + SWA_UNITS_PER_GAP)
        assert cuts[-1] == len(swa)
        _mlstm_chunk(blk, seqs, ng_ref,
                     between=[lambda g=g: [f() for f in swa[cuts[g]:cuts[g + 1]]] for g in range(4)])


def _proj_cols(name):
    width = dict(_PROJ_OUT)[name]
    off = _PROJ_OUT_OFFS[name]
    assert off % width == 0
    return width, off // width


def _mixers(proj, g_rows, gate_b, norm_g, sinks, rel_bias, bucket):
    B, S, _ = proj.shape
    W = WINDOW
    tq = SWA_BLOCKS * W
    assert tq == MLSTM_BLOCKS * MLSTM_CHUNK

    def cur(name):
        n, j = _proj_cols(name)
        return pl.BlockSpec((B, tq, n), lambda i: (0, i, j))

    def prev(name):
        n, j = _proj_cols(name)
        return pl.BlockSpec((B, W, n), lambda i: (0, jnp.maximum(i * SWA_BLOCKS - 1, 0), j))

    out = lambda n: pl.BlockSpec((B, tq, n), lambda i: (0, i, 0))
    gate_rows = [pl.BlockSpec((2 * M_HEADS, tq), lambda i, b=b: (0, b * (S // tq) + i)) for b in range(B)]
    smem = pl.BlockSpec(memory_space=pltpu.SMEM)
    return pl.pallas_call(
        _mixers_kernel,
        grid=(S // tq,),
        in_specs=[smem, smem, _const_spec((W, 2 * W)),
                  cur("qk"), cur("v"), cur("o"),
                  _const_spec((2 * M_HEADS, 1)), _const_spec((1, M_V_WIDTH)),
                  cur("aq"), prev("ak"), cur("ak"), prev("av"), cur("av")] + gate_rows,
        out_specs=[out(M_V_WIDTH), out(A_Q_WIDTH)],
        out_shape=[jax.ShapeDtypeStruct((B, S, M_V_WIDTH), BF16),
                   jax.ShapeDtypeStruct((B, S, A_Q_WIDTH), BF16)],
        scratch_shapes=[pltpu.VMEM((B, M_HEADS, M_QK_DIM, M_V_DIM + LANES), F32),
                        pltpu.VMEM((B, 2 * M_HEADS, MLSTM_CHUNK), F32),
                        pltpu.VMEM((2, A_Q_HEADS, W, 2 * W), F32)],
        compiler_params=pltpu.CompilerParams(
            dimension_semantics=("arbitrary",), vmem_limit_bytes=VMEM_LIMIT_BYTES),
        name="mixers",
    )(sinks, rel_bias, bucket, proj, proj, proj, gate_b, norm_g, proj, proj, proj, proj, proj,
      *([g_rows] * B))


def _post_kernel(x_ref, hm_ref, ha_ref, ga_ref, gb_ref, wm_ref, wa_ref, wo_ref,
                 g_ref, wu_ref, wd_ref, fg_ref, out_ref, *, final):
    sub = PROJ_SUB
    for t in range(x_ref.shape[0] // sub):
        rows = slice(t * sub, (t + 1) * sub)
        y = (jax.nn.sigmoid(ga_ref[rows, :].astype(F32))
             * jnp.dot(hm_ref[rows, :], wm_ref[...], preferred_element_type=F32)
             + jax.nn.sigmoid(gb_ref[rows, :].astype(F32))
             * jnp.dot(ha_ref[rows, :], wa_ref[...], preferred_element_type=F32))
        x = x_ref[rows, :] + jnp.dot(y.astype(BF16), wo_ref[...], preferred_element_type=F32)
        h = _rmsnorm(x, g_ref[...]).astype(BF16)
        u = jnp.maximum(jnp.dot(h, wu_ref[...], preferred_element_type=F32), 0.0)
        x = x + jnp.dot((u * u).astype(BF16), wd_ref[...], preferred_element_type=F32)
        if final:
            x = _rmsnorm(x, fg_ref[...])
        out_ref[rows, :] = x


def _post(layer, x2, hm, ha, proj, wm, wa, wo, g, wu, wd, fg, final):
    T = x2.shape[0]
    tm = POST_TM
    row = pl.BlockSpec((tm, D_MODEL), lambda i: (i, 0))
    gate = lambda name: pl.BlockSpec((tm, D_MODEL), lambda i: (i, _proj_cols(name)[1]))
    wspec = _layer_spec(layer, (D_MODEL, D_MODEL))
    vec = _const_spec((1, D_MODEL))
    return pl.pallas_call(
        functools.partial(_post_kernel, final=final),
        grid=(T // tm,),
        in_specs=[row, row, row, gate("ga"), gate("gb"), wspec, wspec, wspec,
                  vec, _layer_spec(layer, (D_MODEL, D_FF)), _layer_spec(layer, (D_FF, D_MODEL)), vec],
        out_specs=row,
        out_shape=jax.ShapeDtypeStruct((T, D_MODEL), F32),
        compiler_params=pltpu.CompilerParams(
            dimension_semantics=("parallel",), vmem_limit_bytes=VMEM_LIMIT_BYTES),
        name="post",
    )(x2, hm, ha, proj, proj, wm, wa, wo, g, wu, wd, fg)


def _t5_bucket(n):
    max_exact = N_BUCKETS // 2
    n = np.maximum(n, 0)
    large = max_exact + (np.log(np.maximum(n, 1) / max_exact)
                         / np.log(MAX_DISTANCE / max_exact)
                         * (N_BUCKETS - max_exact)).astype(np.int32)
    large = np.minimum(large, N_BUCKETS - 1)
    return np.where(n < max_exact, n, large).astype(np.int32)


def _bucket_table():
    W = WINDOW
    dist = np.arange(W)[:, None] + W - np.arange(2 * W)[None, :]
    valid = (dist >= 0) & (dist < W)
    return jnp.asarray(np.where(valid, _t5_bucket(dist), -1).astype(np.int32))


def _regroup_kernel(w_ref, out_ref):
    n_gate = 2 * M_HEADS
    gate0 = sum(IN_SIZES[:3])
    tail0 = gate0 + n_gate
    n_tail = sum(IN_SIZES[5:])
    step = REGROUP_PIECE
    for src, dst, n, scale in ((0, 0, gate0, 1.0),
                               (tail0, gate0, A_Q_WIDTH, A_HEAD_DIM ** -0.5 * LOG2E),
                               (tail0 + A_Q_WIDTH, gate0 + A_Q_WIDTH, n_tail - A_Q_WIDTH, 1.0)):
        for o in range(0, n, step):
            piece = w_ref[src + o:src + o + step, :]
            if scale != 1.0:
                piece = piece * scale
            out_ref[:, dst + o:dst + o + step] = piece.T.astype(BF16)
    gate_rows = jnp.concatenate(
        [w_ref[gate0:tail0, :], jnp.zeros((LANES - n_gate, w_ref.shape[1]), F32)], axis=0)
    out_ref[:, gate0 + n_tail:] = gate_rows.T.astype(BF16)


def _regroup_w_in(w_in):
    w_t = jnp.swapaxes(w_in, 1, 2)
    depth, n_in, d = w_t.shape
    tc = REGROUP_COLS
    return pl.pallas_call(
        _regroup_kernel,
        grid=(depth, d // tc),
        in_specs=[pl.BlockSpec((None, n_in, tc), lambda l, i: (l, 0, i))],
        out_specs=pl.BlockSpec((None, tc, _INPROJ_WIDTH), lambda l, i: (l, i, 0)),
        out_shape=jax.ShapeDtypeStruct((depth, d, _INPROJ_WIDTH), BF16),
        compiler_params=pltpu.CompilerParams(
            dimension_semantics=("parallel", "parallel"), vmem_limit_bytes=VMEM_LIMIT_BYTES),
        name="regroup",
    )(w_t)


def kernel(x, norm_mix_g, w_in, conv_w, conv_b, b_igate, b_fgate, mlstm_norm_g, attn_sinks,
           rel_bias, w_branch_m, w_branch_a, w_out, norm_mlp_g, w_up, w_down, final_norm_g):
    B, S, D = x.shape
    T = B * S
    bucket = _bucket_table()
    w_all = _regroup_w_in(w_in)
    wm, wa, wo, wu, wd = (w.astype(BF16) for w in (w_branch_m, w_branch_a, w_out, w_up, w_down))
    x2 = x.reshape(T, D)
    for l in range(DEPTH):
        proj, g_rows = _inproj(
            l, x2, norm_mix_g[l].reshape(1, D), conv_w[l], conv_b[l].reshape(1, -1), w_all, S)
        gate_b = jnp.concatenate([b_igate[l], b_fgate[l]]).reshape(2 * M_HEADS, 1)
        hm, ha = _mixers(proj.reshape(B, S, -1), g_rows, gate_b, mlstm_norm_g[l].reshape(1, -1),
                         attn_sinks[l], rel_bias, bucket)
        x2 = _post(l, x2, hm.reshape(T, -1), ha.reshape(T, -1), proj, wm, wa, wo,
                   norm_mlp_g[l].reshape(1, D), wu, wd,
                   final_norm_g.reshape(1, D), final=(l == DEPTH - 1))
    return x2.reshape(B, S, D)
```

```python
import functools

import numpy as np
import jax
import jax.numpy as jnp
from jax import lax
from jax.experimental import pallas as pl
from jax.experimental.pallas import tpu as pltpu

D_MODEL = 1024
DEPTH = 2
M_HEADS = 4
M_QK_DIM = 128
M_V_DIM = 256
M_QK_WIDTH = M_HEADS * M_QK_DIM
M_V_WIDTH = M_HEADS * M_V_DIM
CONV_K = 4
A_Q_HEADS = 16
A_KV_HEADS = 4
A_HEAD_DIM = 64
A_GROUP = A_Q_HEADS // A_KV_HEADS
A_Q_WIDTH = A_Q_HEADS * A_HEAD_DIM
A_KV_WIDTH = A_KV_HEADS * A_HEAD_DIM
WINDOW = 128
N_BUCKETS = 32
MAX_DISTANCE = 128
D_FF = 4 * D_MODEL
EPS = 1e-6
IN_SIZES = (2 * M_QK_WIDTH, M_V_WIDTH, M_V_WIDTH, M_HEADS, M_HEADS,
            A_Q_WIDTH, A_KV_WIDTH, A_KV_WIDTH, D_MODEL, D_MODEL)

INPROJ_TM = 512
INPROJ_SUB = 512
POST_TM = 512
PROJ_SUB = 512
REGROUP_COLS = 256
REGROUP_PIECE = 512
MLSTM_CHUNK = 128
MLSTM_BLOCKS = 4
CONV_HALO = 8
SWA_BLOCKS = 4
SWA_UNITS_PER_GAP = (2, 2, 2, 2)
NEG_BIG = -1e30
LOG2E = 1.4426950408889634
LANES = 128
V7X_VMEM_BYTES = 64 * 1024 * 1024
VMEM_LIMIT_BYTES = V7X_VMEM_BYTES - 8 * 1024 * 1024

F32 = jnp.float32
BF16 = jnp.bfloat16


def _const_spec(shape):
    nd = len(shape)
    return pl.BlockSpec(shape, lambda *_: (0,) * nd, pipeline_mode=pl.Buffered(1))


def _layer_spec(layer, shape):
    nd = len(shape)
    return pl.BlockSpec((None,) + tuple(shape), lambda *_: (layer,) + (0,) * nd,
                        pipeline_mode=pl.Buffered(1))


def _rmsnorm(x, g):
    return x * lax.rsqrt(jnp.mean(x * x, axis=-1, keepdims=True) + EPS) * g


_INPROJ_GROUPS = (("qk", 2 * M_QK_WIDTH), ("v", M_V_WIDTH), ("o", M_V_WIDTH),
                  ("aq", A_Q_WIDTH), ("ak", A_KV_WIDTH), ("av", A_KV_WIDTH),
                  ("ga", D_MODEL), ("gb", D_MODEL), ("g", LANES))
_INPROJ_OFFS = dict(zip((n for n, _ in _INPROJ_GROUPS),
                        np.cumsum([0] + [w for _, w in _INPROJ_GROUPS])[:-1].tolist()))
_INPROJ_WIDTH = sum(w for _, w in _INPROJ_GROUPS)
_PROJ_OUT = tuple((n, 2 * w if n in ("ak", "av") else w) for n, w in _INPROJ_GROUPS[:-1])
_PROJ_OUT_OFFS = dict(zip((n for n, _ in _PROJ_OUT),
                          np.cumsum([0] + [w for _, w in _PROJ_OUT])[:-1].tolist()))
_PROJ_OUT_WIDTH = sum(w for _, w in _PROJ_OUT)


def _inproj_kernel(x_ref, g_ref, cw_ref, cb_ref, w_ref, out_ref, gates_ref, halo_ref, *, tiles_per_seq):
    @pl.when(pl.program_id(0) % tiles_per_seq == 0)
    def _():
        halo_ref[...] = jnp.zeros_like(halo_ref)

    sub = INPROJ_SUB
    for t in range(x_ref.shape[0] // sub):
        _inproj_rows(slice(t * sub, (t + 1) * sub), x_ref, g_ref, cw_ref, cb_ref, w_ref,
                     out_ref, gates_ref, halo_ref)


def _inproj_rows(rows, x_ref, g_ref, cw_ref, cb_ref, w_ref, out_ref, gates_ref, halo_ref):
    n_rows = rows.stop - rows.start
    h = _rmsnorm(x_ref[rows, :], g_ref[...]).astype(BF16)
    res = jnp.dot(h, w_ref[...], preferred_element_type=F32)
    group = lambda name, width: res[:, _INPROJ_OFFS[name]:_INPROJ_OFFS[name] + width]

    head_row = lax.broadcasted_iota(jnp.int32, (CONV_HALO, 1), 0)
    gw = 2 * M_QK_WIDTH // 4
    for c in range(4):
        cols = slice(c * gw, (c + 1) * gw)
        r = res[:, cols]
        prev = halo_ref[:, cols]
        halo_ref[:, cols] = r[n_rows - CONV_HALO:, :]
        y = cb_ref[:, cols] + r * cw_ref[CONV_K - 1:CONV_K, cols]
        for s in range(1, CONV_K):
            rolled = pltpu.roll(r, s, axis=0)
            head = jnp.where(head_row < s, pltpu.roll(prev, s, axis=0), rolled[:CONV_HALO])
            shifted = jnp.concatenate([head, rolled[CONV_HALO:]], axis=0)
            y = y + shifted * cw_ref[CONV_K - 1 - s:CONV_K - s, cols]
        y = y * jax.nn.sigmoid(y)
        if c < 2:
            y = y * (M_QK_DIM ** -0.5)
        out_ref[rows, cols] = y.astype(BF16)

    for name, width in _INPROJ_GROUPS[1:-1]:
        val = group(name, width)
        if name == "o":
            val = jax.nn.sigmoid(val)
        if name in ("ak", "av"):
            half = A_HEAD_DIM
            swapped = [pltpu.roll(val[:, c:c + 2 * half], half, axis=1) for c in range(0, width, 2 * half)]
            val = jnp.concatenate([val] + swapped, axis=1)
        off = _PROJ_OUT_OFFS[name]
        out_ref[rows, off:off + val.shape[1]] = val.astype(BF16)
    gates_ref[:, rows] = group("g", LANES).T[:2 * M_HEADS, :]


def _inproj(layer, x2, g, conv_w, conv_b, w_all, seq_len):
    T = x2.shape[0]
    tm = INPROJ_TM
    row = lambda n: pl.BlockSpec((tm, n), lambda i: (i, 0))
    return pl.pallas_call(
        functools.partial(_inproj_kernel, tiles_per_seq=seq_len // tm),
        grid=(T // tm,),
        in_specs=[row(D_MODEL), _const_spec((1, D_MODEL)), _const_spec((CONV_K, 2 * M_QK_WIDTH)),
                  _const_spec((1, 2 * M_QK_WIDTH)), _layer_spec(layer, (D_MODEL, _INPROJ_WIDTH))],
        out_specs=[row(_PROJ_OUT_WIDTH), pl.BlockSpec((2 * M_HEADS, tm), lambda i: (0, i))],
        out_shape=[jax.ShapeDtypeStruct((T, _PROJ_OUT_WIDTH), BF16),
                   jax.ShapeDtypeStruct((2 * M_HEADS, T), F32)],
        scratch_shapes=[pltpu.VMEM((CONV_HALO, 2 * M_QK_WIDTH), F32)],
        compiler_params=pltpu.CompilerParams(
            dimension_semantics=("arbitrary",), vmem_limit_bytes=VMEM_LIMIT_BYTES),
        name="inproj",
    )(x2, g, conv_w, conv_b, w_all)


def _mlstm_init(c_ref, m_ref):
    @pl.when(pl.program_id(0) == 0)
    def _():
        c_ref[...] = jnp.zeros_like(c_ref)
        m_ref[...] = jnp.zeros_like(m_ref)


def _mlstm_gates(blk, grow_ref, gb_ref, m_ref):
    L = MLSTM_CHUNK
    rows = slice(blk * L, (blk + 1) * L)
    row_i = lax.broadcasted_iota(jnp.int32, (L, L), 0)
    col_j = lax.broadcasted_iota(jnp.int32, (L, L), 1)
    triu_f = (row_i <= col_j).astype(F32)
    g = grow_ref[:, rows] + gb_ref[...]
    logf = pltpu.roll(jax.nn.log_sigmoid(g), M_HEADS, axis=0) * LOG2E
    b = jnp.dot(logf, triu_f, preferred_element_type=F32,
                precision=lax.Precision.HIGHEST)
    a = g * LOG2E - b
    m_prev = m_ref[...]
    m_last = jnp.maximum(m_prev, jnp.max(a, axis=-1, keepdims=True))
    wk = jnp.exp2(a - m_last)
    decay = jnp.exp2(m_prev - m_last)
    m_ref[...] = jnp.sum(logf, axis=-1, keepdims=True) + m_last
    return a, logf, m_prev, wk, decay


def _mlstm_chunk(blk, seqs, ng_ref, between):
    L = MLSTM_CHUNK
    rows = slice(blk * L, (blk + 1) * L)
    row_i = lax.broadcasted_iota(jnp.int32, (L, L), 0)
    col_j = lax.broadcasted_iota(jnp.int32, (L, L), 1)
    tril = col_j <= row_i
    ones_b = jnp.ones((L, LANES), BF16)
    nt = (((1,), (1,)), ((), ()))
    units = [(seq, h) for seq in seqs for h in range(M_HEADS)]
    n_units = range(len(units))

    qb, v_ext, s_raw, k_t, m_prev = [], [], [], [], []
    for u, ((gates, qk_ref, v_ref, _, _, _), h) in enumerate(units):
        qb.append(qk_ref[rows, h * M_QK_DIM:(h + 1) * M_QK_DIM])
        kb = qk_ref[rows, M_QK_WIDTH + h * M_QK_DIM:M_QK_WIDTH + (h + 1) * M_QK_DIM]
        v_ext.append(jnp.concatenate([v_ref[rows, h * M_V_DIM:(h + 1) * M_V_DIM], ones_b], axis=1))
        s_raw.append(lax.dot_general(qb[u], kb, nt, preferred_element_type=F32))
        k_t.append(kb.astype(F32).T)
        m_prev.append(gates[2][h:h + 1, 0:1])
    between[0]()

    both, floor = [], []
    zeros_b = jnp.zeros((M_QK_DIM, M_QK_DIM), BF16)
    for u, ((gates, _, _, _, _, c_ref), h) in enumerate(units):
        a, logf, _, wk, decay_rows = gates
        a_mat = jnp.where(tril, a[h:h + 1, :], -jnp.inf)
        m_row = jnp.maximum(m_prev[u], jnp.max(a_mat, axis=-1, keepdims=True))
        b_col = jnp.sum(jnp.where(tril, logf[h:h + 1, :], 0.0), axis=-1, keepdims=True)
        floor.append(jnp.exp2(-(b_col + m_row)))
        s = s_raw[u] * jnp.exp2(a_mat - m_row)
        q_inter = (qb[u].astype(F32) * jnp.exp2(m_prev[u] - m_row)).astype(BF16)
        kw_t = (k_t[u] * wk[h:h + 1, :]).astype(BF16)
        c_prev = c_ref[h]
        lhs = jnp.concatenate([jnp.concatenate([s.astype(BF16), q_inter], axis=1),
                               jnp.concatenate([kw_t, zeros_b], axis=1)], axis=0)
        res = jnp.dot(lhs, jnp.concatenate([v_ext[u], c_prev.astype(BF16)], axis=0),
                      preferred_element_type=F32)
        both.append(res[:L])
        c_ref[h] = decay_rows[h:h + 1, 0:1] * c_prev + res[L:]
    between[1]()

    hh = []
    for u in n_units:
        inv = 1.0 / jnp.maximum(jnp.abs(both[u][:, M_V_DIM:]), floor[u])
        hh.append(both[u][:, :M_V_DIM] * jnp.concatenate([inv, inv], axis=1))
    msq = [jnp.mean(x * x, axis=-1, keepdims=True) for x in hh]
    between[2]()

    for u, ((_, _, _, og_ref, out_ref, _), h) in enumerate(units):
        sl = slice(h * M_V_DIM, (h + 1) * M_V_DIM)
        hn = hh[u] * lax.rsqrt(msq[u] + EPS) * ng_ref[:, sl]
        out_ref[rows, sl] = hn.astype(BF16) * og_ref[rows, sl]
    between[3]()


def _swa_init(sink_ref, rel_ref, bucket_ref, bias_ref):
    W = WINDOW

    @pl.when(pl.program_id(0) == 0)
    def _():
        bucket = bucket_ref[...]
        neg = jnp.full((W, 2 * W), NEG_BIG, F32)
        for hq in range(A_Q_HEADS):
            bias_ref[1, hq] = neg

        def fill(bb, carry):
            hit = bucket == bb
            for hq in range(A_Q_HEADS):
                bias_ref[1, hq] = jnp.where(hit, rel_ref[bb, hq] * LOG2E, bias_ref[1, hq])
            return carry
        lax.fori_loop(0, N_BUCKETS, fill, 0)
        col = lax.broadcasted_iota(jnp.int32, (W, 2 * W), 1)
        for hq in range(A_Q_HEADS):
            sink = sink_ref[hq] * LOG2E
            bias_ref[0, hq] = jnp.where(col == 0, sink, jnp.where(col < W, NEG_BIG, bias_ref[1, hq]))
            bias_ref[1, hq] = jnp.where(col == 0, sink, bias_ref[1, hq])


def _swa_kv_head(blk, kv, q_ref, kp_ref, kc_ref, vp_ref, vc_ref, out_ref, bias_ref):
    W = WINDOW
    lane_lo = lax.broadcasted_iota(jnp.int32, (2 * W, 2 * A_HEAD_DIM), 1) < A_HEAD_DIM
    not_sink = lax.broadcasted_iota(jnp.int32, (2 * W, 2 * A_HEAD_DIM), 0) > 0
    keep_lo = lane_lo & not_sink
    keep_hi = jnp.logical_not(lane_lo) & not_sink
    zero = jnp.zeros((), BF16)
    ones_lo = jnp.where(lane_lo, 1.0, 0.0).astype(BF16)
    ones_hi = jnp.where(lane_lo, 0.0, 1.0).astype(BF16)
    nt = (((1,), (1,)), ((), ()))
    rows = slice(blk * W, (blk + 1) * W)
    var = jnp.minimum(pl.program_id(0), 1) if blk == 0 else 1
    def tile(ref, rws, swapped):
        c0 = (A_KV_WIDTH if swapped else 0) + (kv // 2) * 2 * A_HEAD_DIM
        return ref[rws, c0:c0 + 2 * A_HEAD_DIM]

    def prev_cur(p_ref, c_ref, swapped):
        prev = (tile(p_ref, slice(None), swapped) if blk == 0
                else tile(c_ref, slice((blk - 1) * W, blk * W), swapped))
        return jnp.concatenate([prev, tile(c_ref, rows, swapped)], axis=0)

    in_low = kv % 2 == 1
    k_even = jnp.where(keep_lo, prev_cur(kp_ref, kc_ref, in_low), zero)
    k_odd = jnp.where(keep_hi, prev_cur(kp_ref, kc_ref, not in_low), zero)
    v_even = jnp.concatenate([jnp.where(keep_lo, prev_cur(vp_ref, vc_ref, in_low), zero), ones_lo], axis=1)
    v_odd = jnp.concatenate([jnp.where(keep_hi, prev_cur(vp_ref, vc_ref, not in_low), zero), ones_hi], axis=1)
    v_both = jnp.concatenate([v_even, v_odd], axis=0)
    pairs = [kv * (A_GROUP // 2) + p for p in range(A_GROUP // 2)]
    cols = [slice(pi * 2 * A_HEAD_DIM, (pi + 1) * 2 * A_HEAD_DIM) for pi in pairs]
    q_st = jnp.concatenate([q_ref[rows, c] for c in cols], axis=0)
    bias_e = jnp.concatenate([bias_ref[var, 2 * pi] for pi in pairs], axis=0)
    bias_o = jnp.concatenate([bias_ref[var, 2 * pi + 1] for pi in pairs], axis=0)
    s_e = lax.dot_general(q_st, k_even, nt, preferred_element_type=F32) + bias_e
    s_o = lax.dot_general(q_st, k_odd, nt, preferred_element_type=F32) + bias_o
    p_e = jnp.exp2(s_e - jnp.max(s_e, axis=-1, keepdims=True)).astype(BF16)
    p_o = jnp.exp2(s_o - jnp.max(s_o, axis=-1, keepdims=True)).astype(BF16)
    acc = jnp.dot(jnp.concatenate([p_e, p_o], axis=1), v_both,
                  preferred_element_type=F32)
    out = (acc[:, :2 * A_HEAD_DIM] / acc[:, 2 * A_HEAD_DIM:]).astype(out_ref.dtype)
    for p, c in enumerate(cols):
        out_ref[rows, c] = out[p * W:(p + 1) * W]


def _mixers_kernel(sink_ref, rel_ref, bucket_ref, qk_ref, v_ref, og_ref, gb_ref, ng_ref,
                   q_ref, kp_ref, kc_ref, vp_ref, vc_ref, *refs):
    n_seq = qk_ref.shape[0]
    grow_refs = refs[:n_seq]
    hm_ref, ha_ref, c_ref, m_ref, bias_ref = refs[n_seq:]
    _mlstm_init(c_ref, m_ref)
    _swa_init(sink_ref, rel_ref, bucket_ref, bias_ref)
    for blk in range(SWA_BLOCKS):
        seqs = [(_mlstm_gates(blk, grow_refs[b], gb_ref, m_ref.at[b]),
                 qk_ref.at[b], v_ref.at[b], og_ref.at[b], hm_ref.at[b], c_ref.at[b])
                for b in range(n_seq)]
        swa = [functools.partial(_swa_kv_head, blk, u, q_ref.at[b], kp_ref.at[b], kc_ref.at[b],
                                 vp_ref.at[b], vc_ref.at[b], ha_ref.at[b], bias_ref)
               for b in range(n_seq) for u in range(A_KV_HEADS)]
        cuts = np.cumsum((0,) + SWA_UNITS_PER_GAP)
        assert cuts[-1] == len(swa)
        _mlstm_chunk(blk, seqs, ng_ref,
                     between=[lambda g=g: [f() for f in swa[cuts[g]:cuts[g + 1]]] for g in range(4)])


def _proj_cols(name):
    width = dict(_PROJ_OUT)[name]
    off = _PROJ_OUT_OFFS[name]
    assert off % width == 0
    return width, off // width


def _mixers(proj, g_rows, gate_b, norm_g, sinks, rel_bias, bucket):
    B, S, _ = proj.shape
    W = WINDOW
    tq = SWA_BLOCKS * W
    assert tq == MLSTM_BLOCKS * MLSTM_CHUNK

    def cur(name):
        n, j = _proj_cols(name)
        return pl.BlockSpec((B, tq, n), lambda i: (0, i, j))

    def prev(name):
        n, j = _proj_cols(name)
        return pl.BlockSpec((B, W, n), lambda i: (0, jnp.maximum(i * SWA_BLOCKS - 1, 0), j))

    out = lambda n: pl.BlockSpec((B, tq, n), lambda i: (0, i, 0))
    gate_rows = [pl.BlockSpec((2 * M_HEADS, tq), lambda i, b=b: (0, b * (S // tq) + i)) for b in range(B)]
    smem = pl.BlockSpec(memory_space=pltpu.SMEM)
    return pl.pallas_call(
        _mixers_kernel,
        grid=(S // tq,),
        in_specs=[smem, smem, _const_spec((W, 2 * W)),
                  cur("qk"), cur("v"), cur("o"),
                  _const_spec((2 * M_HEADS, 1)), _const_spec((1, M_V_WIDTH)),
                  cur("aq"), prev("ak"), cur("ak"), prev("av"), cur("av")] + gate_rows,
        out_specs=[out(M_V_WIDTH), out(A_Q_WIDTH)],
        out_shape=[jax.ShapeDtypeStruct((B, S, M_V_WIDTH), BF16),
                   jax.ShapeDtypeStruct((B, S, A_Q_WIDTH), BF16)],
        scratch_shapes=[pltpu.VMEM((B, M_HEADS, M_QK_DIM, M_V_DIM + LANES), F32),
                        pltpu.VMEM((B, 2 * M_HEADS, MLSTM_CHUNK), F32),
                        pltpu.VMEM((2, A_Q_HEADS, W, 2 * W), F32)],
        compiler_params=pltpu.CompilerParams(
            dimension_semantics=("arbitrary",), vmem_limit_bytes=VMEM_LIMIT_BYTES),
        name="mixers",
    )(sinks, rel_bias, bucket, proj, proj, proj, gate_b, norm_g, proj, proj, proj, proj, proj,
      *([g_rows] * B))


def _post_kernel(x_ref, hm_ref, ha_ref, ga_ref, gb_ref, wm_ref, wa_ref, wo_ref,
                 g_ref, wu_ref, wd_ref, fg_ref, out_ref, *, final):
    sub = PROJ_SUB
    for t in range(x_ref.shape[0] // sub):
        rows = slice(t * sub, (t + 1) * sub)
        y = (jax.nn.sigmoid(ga_ref[rows, :].astype(F32))
             * jnp.dot(hm_ref[rows, :], wm_ref[...], preferred_element_type=F32)
             + jax.nn.sigmoid(gb_ref[rows, :].astype(F32))
             * jnp.dot(ha_ref[rows, :], wa_ref[...], preferred_element_type=F32))
        x = x_ref[rows, :] + jnp.dot(y.astype(BF16), wo_ref[...], preferred_element_type=F32)
        h = _rmsnorm(x, g_ref[...]).astype(BF16)
        u = jnp.maximum(jnp.dot(h, wu_ref[...], preferred_element_type=F32), 0.0)
        x = x + jnp.dot((u * u).astype(BF16), wd_ref[...], preferred_element_type=F32)
        if final:
            x = _rmsnorm(x, fg_ref[...])
        out_ref[rows, :] = x


def _post(layer, x2, hm, ha, proj, wm, wa, wo, g, wu, wd, fg, final):
    T = x2.shape[0]
    tm = POST_TM
    row = pl.BlockSpec((tm, D_MODEL), lambda i: (i, 0))
    gate = lambda name: pl.BlockSpec((tm, D_MODEL), lambda i: (i, _proj_cols(name)[1]))
    wspec = _layer_spec(layer, (D_MODEL, D_MODEL))
    vec = _const_spec((1, D_MODEL))
    return pl.pallas_call(
        functools.partial(_post_kernel, final=final),
        grid=(T // tm,),
        in_specs=[row, row, row, gate("ga"), gate("gb"), wspec, wspec, wspec,
                  vec, _layer_spec(layer, (D_MODEL, D_FF)), _layer_spec(layer, (D_FF, D_MODEL)), vec],
        out_specs=row,
        out_shape=jax.ShapeDtypeStruct((T, D_MODEL), F32),
        compiler_params=pltpu.CompilerParams(
            dimension_semantics=("parallel",), vmem_limit_bytes=VMEM_LIMIT_BYTES),
        name="post",
    )(x2, hm, ha, proj, proj, wm, wa, wo, g, wu, wd, fg)


def _t5_bucket(n):
    max_exact = N_BUCKETS // 2
    n = np.maximum(n, 0)
    large = max_exact + (np.log(np.maximum(n, 1) / max_exact)
                         / np.log(MAX_DISTANCE / max_exact)
                         * (N_BUCKETS - max_exact)).astype(np.int32)
    large = np.minimum(large, N_BUCKETS - 1)
    return np.where(n < max_exact, n, large).astype(np.int32)


def _bucket_table():
    W = WINDOW
    dist = np.arange(W)[:, None] + W - np.arange(2 * W)[None, :]
    valid = (dist >= 0) & (dist < W)
    return jnp.asarray(np.where(valid, _t5_bucket(dist), -1).astype(np.int32))


def _regroup_kernel(w_ref, out_ref):
    n_gate = 2 * M_HEADS
    gate0 = sum(IN_SIZES[:3])
    tail0 = gate0 + n_gate
    n_tail = sum(IN_SIZES[5:])
    step = REGROUP_PIECE
    for src, dst, n, scale in ((0, 0, gate0, 1.0),
                               (tail0, gate0, A_Q_WIDTH, A_HEAD_DIM ** -0.5 * LOG2E),
                               (tail0 + A_Q_WIDTH, gate0 + A_Q_WIDTH, n_tail - A_Q_WIDTH, 1.0)):
        for o in range(0, n, step):
            piece = w_ref[src + o:src + o + step, :]
            if scale != 1.0:
                piece = piece * scale
            out_ref[:, dst + o:dst + o + step] = piece.T.astype(BF16)
    gate_rows = jnp.concatenate(
        [w_ref[gate0:tail0, :], jnp.zeros((LANES - n_gate, w_ref.shape[1]), F32)], axis=0)
    out_ref[:, gate0 + n_tail:] = gate_rows.T.astype(BF16)


def _regroup_w_in(w_in):
    w_t = jnp.swapaxes(w_in, 1, 2)
    depth, n_in, d = w_t.shape
    tc = REGROUP_COLS
    return pl.pallas_call(
        _regroup_kernel,
        grid=(depth, d // tc),
        in_specs=[pl.BlockSpec((None, n_in, tc), lambda l, i: (l, 0, i))],
        out_specs=pl.BlockSpec((None, tc, _INPROJ_WIDTH), lambda l, i: (l, i, 0)),
        out_shape=jax.ShapeDtypeStruct((depth, d, _INPROJ_WIDTH), BF16),
        compiler_params=pltpu.CompilerParams(
            dimension_semantics=("parallel", "parallel"), vmem_limit_bytes=VMEM_LIMIT_BYTES),
        name="regroup",
    )(w_t)


def kernel(x, norm_mix_g, w_in, conv_w, conv_b, b_igate, b_fgate, mlstm_norm_g, attn_sinks,
           rel_bias, w_branch_m, w_branch_a, w_out, norm_mlp_g, w_up, w_down, final_norm_g):
    B, S, D = x.shape
    T = B * S
    bucket = _bucket_table()
    w_all = _regroup_w_in(w_in)
    wm, wa, wo, wu, wd = (w.astype(BF16) for w in (w_branch_m, w_branch_a, w_out, w_up, w_down))
    x2 = x.reshape(T, D)
    for l in range(DEPTH):
        proj, g_rows = _inproj(
            l, x2, norm_mix_g[l].reshape(1, D), conv_w[l], conv_b[l].reshape(1, -1), w_all, S)
        gate_b = jnp.concatenate([b_igate[l], b_fgate[l]]).reshape(2 * M_HEADS, 1)
        hm, ha = _mixers(proj.reshape(B, S, -1), g_rows, gate_b, mlstm_norm_g[l].reshape(1, -1),
                         attn_sinks[l], rel_bias, bucket)
        x2 = _post(l, x2, hm.reshape(T, -1), ha.reshape(T, -1), proj, wm, wa, wo,
                   norm_mlp_g[l].reshape(1, D), wu, wd,
                   final_norm_g.reshape(1, D), final=(l == DEPTH - 1))
    return x2.reshape(B, S, D)
```

```python
import functools

import numpy as np
import jax
import jax.numpy as jnp
from jax import lax
from jax.experimental import pallas as pl
from jax.experimental.pallas import tpu as pltpu

D_MODEL = 1024
DEPTH = 2
M_HEADS = 4
M_QK_DIM = 128
M_V_DIM = 256
M_QK_WIDTH = M_HEADS * M_QK_DIM
M_V_WIDTH = M_HEADS * M_V_DIM
CONV_K = 4
A_Q_HEADS = 16
A_KV_HEADS = 4
A_HEAD_DIM = 64
A_GROUP = A_Q_HEADS // A_KV_HEADS
A_Q_WIDTH = A_Q_HEADS * A_HEAD_DIM
A_KV_WIDTH = A_KV_HEADS * A_HEAD_DIM
WINDOW = 128
N_BUCKETS = 32
MAX_DISTANCE = 128
D_FF = 4 * D_MODEL
EPS = 1e-6
IN_SIZES = (2 * M_QK_WIDTH, M_V_WIDTH, M_V_WIDTH, M_HEADS, M_HEADS,
            A_Q_WIDTH, A_KV_WIDTH, A_KV_WIDTH, D_MODEL, D_MODEL)

INPROJ_TM = 512
INPROJ_SUB = 512
POST_TM = 512
PROJ_SUB = 512
REGROUP_COLS = 256
REGROUP_PIECE = 512
MLSTM_CHUNK = 128
MLSTM_BLOCKS = 4
CONV_HALO = 8
SWA_BLOCKS = 4
SWA_UNITS_PER_GAP = (2, 2, 2, 2)
NEG_BIG = -1e30
LOG2E = 1.4426950408889634
LANES = 128
V7X_VMEM_BYTES = 64 * 1024 * 1024
VMEM_LIMIT_BYTES = V7X_VMEM_BYTES - 8 * 1024 * 1024

F32 = jnp.float32
BF16 = jnp.bfloat16


def _const_spec(shape):
    nd = len(shape)
    return pl.BlockSpec(shape, lambda *_: (0,) * nd, pipeline_mode=pl.Buffered(1))


def _layer_spec(layer, shape):
    nd = len(shape)
    return pl.BlockSpec((None,) + tuple(shape), lambda *_: (layer,) + (0,) * nd,
                        pipeline_mode=pl.Buffered(1))


def _rmsnorm(x, g):
    return x * lax.rsqrt(jnp.mean(x * x, axis=-1, keepdims=True) + EPS) * g


_INPROJ_GROUPS = (("qk", 2 * M_QK_WIDTH), ("v", M_V_WIDTH), ("o", M_V_WIDTH),
                  ("aq", A_Q_WIDTH), ("ak", A_KV_WIDTH), ("av", A_KV_WIDTH),
                  ("ga", D_MODEL), ("gb", D_MODEL), ("g", LANES))
_INPROJ_OFFS = dict(zip((n for n, _ in _INPROJ_GROUPS),
                        np.cumsum([0] + [w for _, w in _INPROJ_GROUPS])[:-1].tolist()))
_INPROJ_WIDTH = sum(w for _, w in _INPROJ_GROUPS)
_PROJ_OUT = tuple((n, 2 * w if n in ("ak", "av") else w) for n, w in _INPROJ_GROUPS[:-1])
_PROJ_OUT_OFFS = dict(zip((n for n, _ in _PROJ_OUT),
                          np.cumsum([0] + [w for _, w in _PROJ_OUT])[:-1].tolist()))
_PROJ_OUT_WIDTH = sum(w for _, w in _PROJ_OUT)


def _inproj_kernel(x_ref, g_ref, cw_ref, cb_ref, w_ref, out_ref, gates_ref, halo_ref, *, tiles_per_seq):
    @pl.when(pl.program_id(0) % tiles_per_seq == 0)
    def _():
        halo_ref[...] = jnp.zeros_like(halo_ref)

    sub = INPROJ_SUB
    for t in range(x_ref.shape[0] // sub):
        _inproj_rows(slice(t * sub, (t + 1) * sub), x_ref, g_ref, cw_ref, cb_ref, w_ref,
                     out_ref, gates_ref, halo_ref)


def _inproj_rows(rows, x_ref, g_ref, cw_ref, cb_ref, w_ref, out_ref, gates_ref, halo_ref):
    n_rows = rows.stop - rows.start
    h = _rmsnorm(x_ref[rows, :], g_ref[...]).astype(BF16)
    res = jnp.dot(h, w_ref[...], preferred_element_type=F32)
    group = lambda name, width: res[:, _INPROJ_OFFS[name]:_INPROJ_OFFS[name] + width]

    head_row = lax.broadcasted_iota(jnp.int32, (CONV_HALO, 1), 0)
    gw = 2 * M_QK_WIDTH // 4
    for c in range(4):
        cols = slice(c * gw, (c + 1) * gw)
        r = res[:, cols]
        prev = halo_ref[:, cols]
        halo_ref[:, cols] = r[n_rows - CONV_HALO:, :]
        y = cb_ref[:, cols] + r * cw_ref[CONV_K - 1:CONV_K, cols]
        for s in range(1, CONV_K):
            rolled = pltpu.roll(r, s, axis=0)
            head = jnp.where(head_row < s, pltpu.roll(prev, s, axis=0), rolled[:CONV_HALO])
            shifted = jnp.concatenate([head, rolled[CONV_HALO:]], axis=0)
            y = y + shifted * cw_ref[CONV_K - 1 - s:CONV_K - s, cols]
        y = y * jax.nn.sigmoid(y)
        if c < 2:
            y = y * (M_QK_DIM ** -0.5)
        out_ref[rows, cols] = y.astype(BF16)

    for name, width in _INPROJ_GROUPS[1:-1]:
        val = group(name, width)
        if name in ("ak", "av"):
            half = A_HEAD_DIM
            swapped = [pltpu.roll(val[:, c:c + 2 * half], half, axis=1) for c in range(0, width, 2 * half)]
            val = jnp.concatenate([val] + swapped, axis=1)
        off = _PROJ_OUT_OFFS[name]
        out_ref[rows, off:off + val.shape[1]] = val.astype(BF16)
    gates_ref[:, rows] = group("g", LANES).T[:2 * M_HEADS, :]


def _inproj(layer, x2, g, conv_w, conv_b, w_all, seq_len):
    T = x2.shape[0]
    tm = INPROJ_TM
    row = lambda n: pl.BlockSpec((tm, n), lambda i: (i, 0))
    return pl.pallas_call(
        functools.partial(_inproj_kernel, tiles_per_seq=seq_len // tm),
        grid=(T // tm,),
        in_specs=[row(D_MODEL), _const_spec((1, D_MODEL)), _const_spec((CONV_K, 2 * M_QK_WIDTH)),
                  _const_spec((1, 2 * M_QK_WIDTH)), _layer_spec(layer, (D_MODEL, _INPROJ_WIDTH))],
        out_specs=[row(_PROJ_OUT_WIDTH), pl.BlockSpec((2 * M_HEADS, tm), lambda i: (0, i))],
        out_shape=[jax.ShapeDtypeStruct((T, _PROJ_OUT_WIDTH), BF16),
                   jax.ShapeDtypeStruct((2 * M_HEADS, T), F32)],
        scratch_shapes=[pltpu.VMEM((CONV_HALO, 2 * M_QK_WIDTH), F32)],
        compiler_params=pltpu.CompilerParams(
            dimension_semantics=("arbitrary",), vmem_limit_bytes=VMEM_LIMIT_BYTES),
        name="inproj",
    )(x2, g, conv_w, conv_b, w_all)


def _mlstm_init(c_ref, m_ref):
    @pl.when(pl.program_id(0) == 0)
    def _():
        c_ref[...] = jnp.zeros_like(c_ref)
        m_ref[...] = jnp.zeros_like(m_ref)


def _mlstm_gates(blk, grow_ref, gb_ref, m_ref):
    L = MLSTM_CHUNK
    rows = slice(blk * L, (blk + 1) * L)
    row_i = lax.broadcasted_iota(jnp.int32, (L, L), 0)
    col_j = lax.broadcasted_iota(jnp.int32, (L, L), 1)
    triu_f = (row_i <= col_j).astype(F32)
    g = grow_ref[:, rows] + gb_ref[...]
    logf = pltpu.roll(jax.nn.log_sigmoid(g), M_HEADS, axis=0) * LOG2E
    b = jnp.dot(logf, triu_f, preferred_element_type=F32,
                precision=lax.Precision.HIGHEST)
    a = g * LOG2E - b
    m_prev = m_ref[...]
    m_last = jnp.maximum(m_prev, jnp.max(a, axis=-1, keepdims=True))
    wk = jnp.exp2(a - m_last)
    decay = jnp.exp2(m_prev - m_last)
    m_ref[...] = jnp.sum(logf, axis=-1, keepdims=True) + m_last
    return a, logf, m_prev, wk, decay


def _mlstm_chunk(blk, seqs, ng_ref, between):
    L = MLSTM_CHUNK
    rows = slice(blk * L, (blk + 1) * L)
    row_i = lax.broadcasted_iota(jnp.int32, (L, L), 0)
    col_j = lax.broadcasted_iota(jnp.int32, (L, L), 1)
    tril = col_j <= row_i
    ones_b = jnp.ones((L, LANES), BF16)
    nt = (((1,), (1,)), ((), ()))
    units = [(seq, h) for seq in seqs for h in range(M_HEADS)]
    n_units = range(len(units))

    qb, v_ext, s_raw, k_t, m_prev = [], [], [], [], []
    for u, ((gates, qk_ref, v_ref, _, _, _), h) in enumerate(units):
        qb.append(qk_ref[rows, h * M_QK_DIM:(h + 1) * M_QK_DIM])
        kb = qk_ref[rows, M_QK_WIDTH + h * M_QK_DIM:M_QK_WIDTH + (h + 1) * M_QK_DIM]
        v_ext.append(jnp.concatenate([v_ref[rows, h * M_V_DIM:(h + 1) * M_V_DIM], ones_b], axis=1))
        s_raw.append(lax.dot_general(qb[u], kb, nt, preferred_element_type=F32))
        k_t.append(kb.astype(F32).T)
        m_prev.append(gates[2][h:h + 1, 0:1])
    between[0]()

    both, floor = [], []
    zeros_b = jnp.zeros((M_QK_DIM, M_QK_DIM), BF16)
    for u, ((gates, _, _, _, _, c_ref), h) in enumerate(units):
        a, logf, _, wk, decay_rows = gates
        a_mat = jnp.where(tril, a[h:h + 1, :], -jnp.inf)
        m_row = jnp.maximum(m_prev[u], jnp.max(a_mat, axis=-1, keepdims=True))
        b_col = jnp.sum(jnp.where(tril, logf[h:h + 1, :], 0.0), axis=-1, keepdims=True)
        floor.append(jnp.exp2(-(b_col + m_row)))
        s = s_raw[u] * jnp.exp2(a_mat - m_row)
        q_inter = (qb[u].astype(F32) * jnp.exp2(m_prev[u] - m_row)).astype(BF16)
        kw_t = (k_t[u] * wk[h:h + 1, :]).astype(BF16)
        c_prev = c_ref[h]
        lhs = jnp.concatenate([jnp.concatenate([s.astype(BF16), q_inter], axis=1),
                               jnp.concatenate([kw_t, zeros_b], axis=1)], axis=0)
        res = jnp.dot(lhs, jnp.concatenate([v_ext[u], c_prev.astype(BF16)], axis=0),
                      preferred_element_type=F32)
        both.append(res[:L])
        c_ref[h] = decay_rows[h:h + 1, 0:1] * c_prev + res[L:]
    between[1]()

    hh = []
    for u in n_units:
        inv = 1.0 / jnp.maximum(jnp.abs(both[u][:, M_V_DIM:]), floor[u])
        hh.append(both[u][:, :M_V_DIM] * jnp.concatenate([inv, inv], axis=1))
    msq = [jnp.mean(x * x, axis=-1, keepdims=True) for x in hh]
    between[2]()

    for u, ((_, _, _, og_ref, out_ref, _), h) in enumerate(units):
        sl = slice(h * M_V_DIM, (h + 1) * M_V_DIM)
        hn = hh[u] * lax.rsqrt(msq[u] + EPS) * ng_ref[:, sl]
        out_ref[rows, sl] = (hn * jax.nn.sigmoid(og_ref[rows, sl].astype(F32))).astype(BF16)
    between[3]()


def _swa_init(sink_ref, rel_ref, bucket_ref, bias_ref):
    W = WINDOW

    @pl.when(pl.program_id(0) == 0)
    def _():
        bucket = bucket_ref[...]
        neg = jnp.full((W, 2 * W), NEG_BIG, F32)
        for hq in range(A_Q_HEADS):
            bias_ref[1, hq] = neg

        def fill(bb, carry):
            hit = bucket == bb
            for hq in range(A_Q_HEADS):
                bias_ref[1, hq] = jnp.where(hit, rel_ref[bb, hq] * LOG2E, bias_ref[1, hq])
            return carry
        lax.fori_loop(0, N_BUCKETS, fill, 0)
        col = lax.broadcasted_iota(jnp.int32, (W, 2 * W), 1)
        for hq in range(A_Q_HEADS):
            sink = sink_ref[hq] * LOG2E
            bias_ref[0, hq] = jnp.where(col == 0, sink, jnp.where(col < W, NEG_BIG, bias_ref[1, hq]))
            bias_ref[1, hq] = jnp.where(col == 0, sink, bias_ref[1, hq])


def _swa_kv_head(blk, kv, q_ref, kp_ref, kc_ref, vp_ref, vc_ref, out_ref, bias_ref):
    W = WINDOW
    lane_lo = lax.broadcasted_iota(jnp.int32, (2 * W, 2 * A_HEAD_DIM), 1) < A_HEAD_DIM
    not_sink = lax.broadcasted_iota(jnp.int32, (2 * W, 2 * A_HEAD_DIM), 0) > 0
    keep_lo = lane_lo & not_sink
    keep_hi = jnp.logical_not(lane_lo) & not_sink
    zero = jnp.zeros((), BF16)
    ones_lo = jnp.where(lane_lo, 1.0, 0.0).astype(BF16)
    ones_hi = jnp.where(lane_lo, 0.0, 1.0).astype(BF16)
    nt = (((1,), (1,)), ((), ()))
    rows = slice(blk * W, (blk + 1) * W)
    var = jnp.minimum(pl.program_id(0), 1) if blk == 0 else 1
    def tile(ref, rws, swapped):
        c0 = (A_KV_WIDTH if swapped else 0) + (kv // 2) * 2 * A_HEAD_DIM
        return ref[rws, c0:c0 + 2 * A_HEAD_DIM]

    def prev_cur(p_ref, c_ref, swapped):
        prev = (tile(p_ref, slice(None), swapped) if blk == 0
                else tile(c_ref, slice((blk - 1) * W, blk * W), swapped))
        return jnp.concatenate([prev, tile(c_ref, rows, swapped)], axis=0)

    in_low = kv % 2 == 1
    k_even = jnp.where(keep_lo, prev_cur(kp_ref, kc_ref, in_low), zero)
    k_odd = jnp.where(keep_hi, prev_cur(kp_ref, kc_ref, not in_low), zero)
    v_even = jnp.concatenate([jnp.where(keep_lo, prev_cur(vp_ref, vc_ref, in_low), zero), ones_lo], axis=1)
    v_odd = jnp.concatenate([jnp.where(keep_hi, prev_cur(vp_ref, vc_ref, not in_low), zero), ones_hi], axis=1)
    v_both = jnp.concatenate([v_even, v_odd], axis=0)
    pairs = [kv * (A_GROUP // 2) + p for p in range(A_GROUP // 2)]
    cols = [slice(pi * 2 * A_HEAD_DIM, (pi + 1) * 2 * A_HEAD_DIM) for pi in pairs]
    q_st = jnp.concatenate([q_ref[rows, c] for c in cols], axis=0)
    bias_e = jnp.concatenate([bias_ref[var, 2 * pi] for pi in pairs], axis=0)
    bias_o = jnp.concatenate([bias_ref[var, 2 * pi + 1] for pi in pairs], axis=0)
    s_e = lax.dot_general(q_st, k_even, nt, preferred_element_type=F32) + bias_e
    s_o = lax.dot_general(q_st, k_odd, nt, preferred_element_type=F32) + bias_o
    p_e = jnp.exp2(s_e - jnp.max(s_e, axis=-1, keepdims=True)).astype(BF16)
    p_o = jnp.exp2(s_o - jnp.max(s_o, axis=-1, keepdims=True)).astype(BF16)
    acc = jnp.dot(jnp.concatenate([p_e, p_o], axis=1), v_both,
                  preferred_element_type=F32)
    out = (acc[:, :2 * A_HEAD_DIM] / acc[:, 2 * A_HEAD_DIM:]).astype(out_ref.dtype)
    for p, c in enumerate(cols):
        out_ref[rows, c] = out[p * W:(p + 1) * W]


def _mixers_kernel(sink_ref, rel_ref, bucket_ref, qk_ref, v_ref, og_ref, gb_ref, ng_ref,
                   q_ref, kp_ref, kc_ref, vp_ref, vc_ref, *refs):
    n_seq = qk_ref.shape[0]
    grow_refs = refs[:n_seq]
    hm_ref, ha_ref, c_ref, m_ref, bias_ref = refs[n_seq:]
    _mlstm_init(c_ref, m_ref)
    _swa_init(sink_ref, rel_ref, bucket_ref, bias_ref)
    for blk in range(SWA_BLOCKS):
        seqs = [(_mlstm_gates(blk, grow_refs[b], gb_ref, m_ref.at[b]),
                 qk_ref.at[b], v_ref.at[b], og_ref.at[b], hm_ref.at[b], c_ref.at[b])
                for b in range(n_seq)]
        swa = [functools.partial(_swa_kv_head, blk, u, q_ref.at[b], kp_ref.at[b], kc_ref.at[b],
                                 vp_ref.at[b], vc_ref.at[b], ha_ref.at[b], bias_ref)
               for b in range(n_seq) for u in range(A_KV_HEADS)]
        cuts = np.cumsum((0,) + SWA_UNITS_PER_GAP)
        assert cuts[-1] == len(swa)
        _mlstm_chunk(blk, seqs, ng_ref,
                     between=[lambda g=g: [f() for f in swa[cuts[g]:cuts[g + 1]]] for g in range(4)])


def _proj_cols(name):
    width = dict(_PROJ_OUT)[name]
    off = _PROJ_OUT_OFFS[name]
    assert off % width == 0
    return width, off // width


def _mixers(proj, g_rows, gate_b, norm_g, sinks, rel_bias, bucket):
    B, S, _ = proj.shape
    W = WINDOW
    tq = SWA_BLOCKS * W
    assert tq == MLSTM_BLOCKS * MLSTM_CHUNK

    def cur(name):
        n, j = _proj_cols(name)
        return pl.BlockSpec((B, tq, n), lambda i: (0, i, j))

    def prev(name):
        n, j = _proj_cols(name)
        return pl.BlockSpec((B, W, n), lambda i: (0, jnp.maximum(i * SWA_BLOCKS - 1, 0), j))

    out = lambda n: pl.BlockSpec((B, tq, n), lambda i: (0, i, 0))
    gate_rows = [pl.BlockSpec((2 * M_HEADS, tq), lambda i, b=b: (0, b * (S // tq) + i)) for b in range(B)]
    smem = pl.BlockSpec(memory_space=pltpu.SMEM)
    return pl.pallas_call(
        _mixers_kernel,
        grid=(S // tq,),
        in_specs=[smem, smem, _const_spec((W, 2 * W)),
                  cur("qk"), cur("v"), cur("o"),
                  _const_spec((2 * M_HEADS, 1)), _const_spec((1, M_V_WIDTH)),
                  cur("aq"), prev("ak"), cur("ak"), prev("av"), cur("av")] + gate_rows,
        out_specs=[out(M_V_WIDTH), out(A_Q_WIDTH)],
        out_shape=[jax.ShapeDtypeStruct((B, S, M_V_WIDTH), BF16),
                   jax.ShapeDtypeStruct((B, S, A_Q_WIDTH), BF16)],
        scratch_shapes=[pltpu.VMEM((B, M_HEADS, M_QK_DIM, M_V_DIM + LANES), F32),
                        pltpu.VMEM((B, 2 * M_HEADS, MLSTM_CHUNK), F32),
                        pltpu.VMEM((2, A_Q_HEADS, W, 2 * W), F32)],
        compiler_params=pltpu.CompilerParams(
            dimension_semantics=("arbitrary",), vmem_limit_bytes=VMEM_LIMIT_BYTES),
        name="mixers",
    )(sinks, rel_bias, bucket, proj, proj, proj, gate_b, norm_g, proj, proj, proj, proj, proj,
      *([g_rows] * B))


def _post_kernel(x_ref, hm_ref, ha_ref, ga_ref, gb_ref, wm_ref, wa_ref, wo_ref,
                 g_ref, wu_ref, wd_ref, fg_ref, out_ref, *, final):
    sub = PROJ_SUB
    for t in range(x_ref.shape[0] // sub):
        rows = slice(t * sub, (t + 1) * sub)
        y = (jax.nn.sigmoid(ga_ref[rows, :].astype(F32))
             * jnp.dot(hm_ref[rows, :], wm_ref[...], preferred_element_type=F32)
             + jax.nn.sigmoid(gb_ref[rows, :].astype(F32))
             * jnp.dot(ha_ref[rows, :], wa_ref[...], preferred_element_type=F32))
        x = x_ref[rows, :] + jnp.dot(y.astype(BF16), wo_ref[...], preferred_element_type=F32)
        h = _rmsnorm(x, g_ref[...]).astype(BF16)
        u = jnp.maximum(jnp.dot(h, wu_ref[...], preferred_element_type=F32), 0.0)
        x = x + jnp.dot((u * u).astype(BF16), wd_ref[...], preferred_element_type=F32)
        if final:
            x = _rmsnorm(x, fg_ref[...])
        out_ref[rows, :] = x


def _post(layer, x2, hm, ha, proj, wm, wa, wo, g, wu, wd, fg, final):
    T = x2.shape[0]
    tm = POST_TM
    row = pl.BlockSpec((tm, D_MODEL), lambda i: (i, 0))
    gate = lambda name: pl.BlockSpec((tm, D_MODEL), lambda i: (i, _proj_cols(name)[1]))
    wspec = _layer_spec(layer, (D_MODEL, D_MODEL))
    vec = _const_spec((1, D_MODEL))
    return pl.pallas_call(
        functools.partial(_post_kernel, final=final),
        grid=(T // tm,),
        in_specs=[row, row, row, gate("ga"), gate("gb"), wspec, wspec, wspec,
                  vec, _layer_spec(layer, (D_MODEL, D_FF)), _layer_spec(layer, (D_FF, D_MODEL)), vec],
        out_specs=row,
        out_shape=jax.ShapeDtypeStruct((T, D_MODEL), F32),
        compiler_params=pltpu.CompilerParams(
            dimension_semantics=("parallel",), vmem_limit_bytes=VMEM_LIMIT_BYTES),
        name="post",
    )(x2, hm, ha, proj, proj, wm, wa, wo, g, wu, wd, fg)


def _t5_bucket(n):
    max_exact = N_BUCKETS // 2
    n = np.maximum(n, 0)
    large = max_exact + (np.log(np.maximum(n, 1) / max_exact)
                         / np.log(MAX_DISTANCE / max_exact)
                         * (N_BUCKETS - max_exact)).astype(np.int32)
    large = np.minimum(large, N_BUCKETS - 1)
    return np.where(n < max_exact, n, large).astype(np.int32)


def _bucket_table():
    W = WINDOW
    dist = np.arange(W)[:, None] + W - np.arange(2 * W)[None, :]
    valid = (dist >= 0) & (dist < W)
    return jnp.asarray(np.where(valid, _t5_bucket(dist), -1).astype(np.int32))


def _regroup_kernel(w_ref, out_ref):
    n_gate = 2 * M_HEADS
    gate0 = sum(IN_SIZES[:3])
    tail0 = gate0 + n_gate
    n_tail = sum(IN_SIZES[5:])
    step = REGROUP_PIECE
    for src, dst, n, scale in ((0, 0, gate0, 1.0),
                               (tail0, gate0, A_Q_WIDTH, A_HEAD_DIM ** -0.5 * LOG2E),
                               (tail0 + A_Q_WIDTH, gate0 + A_Q_WIDTH, n_tail - A_Q_WIDTH, 1.0)):
        for o in range(0, n, step):
            piece = w_ref[src + o:src + o + step, :]
            if scale != 1.0:
                piece = piece * scale
            out_ref[:, dst + o:dst + o + step] = piece.T.astype(BF16)
    gate_rows = jnp.concatenate(
        [w_ref[gate0:tail0, :], jnp.zeros((LANES - n_gate, w_ref.shape[1]), F32)], axis=0)
    out_ref[:, gate0 + n_tail:] = gate_rows.T.astype(BF16)


def _regroup_w_in(w_in):
    w_t = jnp.swapaxes(w_in, 1, 2)
    depth, n_in, d = w_t.shape
    tc = REGROUP_COLS
    return pl.pallas_call(
        _regroup_kernel,
        grid=(depth, d // tc),
        in_specs=[pl.BlockSpec((None, n_in, tc), lambda l, i: (l, 0, i))],
        out_specs=pl.BlockSpec((None, tc, _INPROJ_WIDTH), lambda l, i: (l, i, 0)),
        out_shape=jax.ShapeDtypeStruct((depth, d, _INPROJ_WIDTH), BF16),
        compiler_params=pltpu.CompilerParams(
            dimension_semantics=("parallel", "parallel"), vmem_limit_bytes=VMEM_LIMIT_BYTES),
        name="regroup",
    )(w_t)


def kernel(x, norm_mix_g, w_in, conv_w, conv_b, b_igate, b_fgate, mlstm_norm_g, attn_sinks,
           rel_bias, w_branch_m, w_branch_a, w_out, norm_mlp_g, w_up, w_down, final_norm_g):
    B, S, D = x.shape
    T = B * S
    bucket = _bucket_table()
    w_all = _regroup_w_in(w_in)
    wm, wa, wo, wu, wd = (w.astype(BF16) for w in (w_branch_m, w_branch_a, w_out, w_up, w_down))
    x2 = x.reshape(T, D)
    for l in range(DEPTH):
        proj, g_rows = _inproj(
            l, x2, norm_mix_g[l].reshape(1, D), conv_w[l], conv_b[l].reshape(1, -1), w_all, S)
        gate_b = jnp.concatenate([b_igate[l], b_fgate[l]]).reshape(2 * M_HEADS, 1)
        hm, ha = _mixers(proj.reshape(B, S, -1), g_rows, gate_b, mlstm_norm_g[l].reshape(1, -1),
                         attn_sinks[l], rel_bias, bucket)
        x2 = _post(l, x2, hm.reshape(T, -1), ha.reshape(T, -1), proj, wm, wa, wo,
                   norm_mlp_g[l].reshape(1, D), wu, wd,
                   final_norm_g.reshape(1, D), final=(l == DEPTH - 1))
    return x2.reshape(B, S, D)
```

```python
import functools

import numpy as np
import jax
import jax.numpy as jnp
from jax import lax
from jax.experimental import pallas as pl
from jax.experimental.pallas import tpu as pltpu

D_MODEL = 1024
DEPTH = 2
M_HEADS = 4
M_QK_DIM = 128
M_V_DIM = 256
M_QK_WIDTH = M_HEADS * M_QK_DIM
M_V_WIDTH = M_HEADS * M_V_DIM
CONV_K = 4
A_Q_HEADS = 16
A_KV_HEADS = 4
A_HEAD_DIM = 64
A_GROUP = A_Q_HEADS // A_KV_HEADS
A_Q_WIDTH = A_Q_HEADS * A_HEAD_DIM
A_KV_WIDTH = A_KV_HEADS * A_HEAD_DIM
WINDOW = 128
N_BUCKETS = 32
MAX_DISTANCE = 128
D_FF = 4 * D_MODEL
EPS = 1e-6
IN_SIZES = (2 * M_QK_WIDTH, M_V_WIDTH, M_V_WIDTH, M_HEADS, M_HEADS,
            A_Q_WIDTH, A_KV_WIDTH, A_KV_WIDTH, D_MODEL, D_MODEL)

INPROJ_TM = 512
INPROJ_SUB = 512
POST_TM = 512
PROJ_SUB = 512
REGROUP_COLS = 256
REGROUP_PIECE = 512
MLSTM_CHUNK = 128
MLSTM_BLOCKS = 4
CONV_HALO = 8
SWA_BLOCKS = 4
SWA_UNITS_PER_GAP = (2, 2, 2, 2)
NEG_BIG = -1e30
LOG2E = 1.4426950408889634
LANES = 128
V7X_VMEM_BYTES = 64 * 1024 * 1024
VMEM_LIMIT_BYTES = V7X_VMEM_BYTES - 8 * 1024 * 1024

F32 = jnp.float32
BF16 = jnp.bfloat16


def _const_spec(shape):
    nd = len(shape)
    return pl.BlockSpec(shape, lambda *_: (0,) * nd, pipeline_mode=pl.Buffered(1))


def _layer_spec(layer, shape):
    nd = len(shape)
    return pl.BlockSpec((None,) + tuple(shape), lambda *_: (layer,) + (0,) * nd,
                        pipeline_mode=pl.Buffered(1))


def _rmsnorm(x, g):
    return x * lax.rsqrt(jnp.mean(x * x, axis=-1, keepdims=True) + EPS) * g


_INPROJ_GROUPS = (("qk", 2 * M_QK_WIDTH), ("v", M_V_WIDTH), ("o", M_V_WIDTH),
                  ("aq", A_Q_WIDTH), ("ak", A_KV_WIDTH), ("av", A_KV_WIDTH),
                  ("ga", D_MODEL), ("gb", D_MODEL), ("g", LANES))
_INPROJ_OFFS = dict(zip((n for n, _ in _INPROJ_GROUPS),
                        np.cumsum([0] + [w for _, w in _INPROJ_GROUPS])[:-1].tolist()))
_INPROJ_WIDTH = sum(w for _, w in _INPROJ_GROUPS)
_PROJ_OUT = tuple((n, 2 * w if n in ("ak", "av") else w) for n, w in _INPROJ_GROUPS[:-1])
_PROJ_OUT_OFFS = dict(zip((n for n, _ in _PROJ_OUT),
                          np.cumsum([0] + [w for _, w in _PROJ_OUT])[:-1].tolist()))
_PROJ_OUT_WIDTH = sum(w for _, w in _PROJ_OUT)


def _inproj_kernel(x_ref, g_ref, cw_ref, cb_ref, w_ref, out_ref, gates_ref, halo_ref, *, tiles_per_seq):
    @pl.when(pl.program_id(0) % tiles_per_seq == 0)
    def _():
        halo_ref[...] = jnp.zeros_like(halo_ref)

    sub = INPROJ_SUB
    for t in range(x_ref.shape[0] // sub):
        _inproj_rows(slice(t * sub, (t + 1) * sub), x_ref, g_ref, cw_ref, cb_ref, w_ref,
                     out_ref, gates_ref, halo_ref)


def _inproj_rows(rows, x_ref, g_ref, cw_ref, cb_ref, w_ref, out_ref, gates_ref, halo_ref):
    n_rows = rows.stop - rows.start
    h = _rmsnorm(x_ref[rows, :], g_ref[...]).astype(BF16)
    res = jnp.dot(h, w_ref[...], preferred_element_type=F32)
    group = lambda name, width: res[:, _INPROJ_OFFS[name]:_INPROJ_OFFS[name] + width]

    head_row = lax.broadcasted_iota(jnp.int32, (CONV_HALO, 1), 0)
    gw = 2 * M_QK_WIDTH // 4
    for c in range(4):
        cols = slice(c * gw, (c + 1) * gw)
        r = res[:, cols]
        prev = halo_ref[:, cols]
        halo_ref[:, cols] = r[n_rows - CONV_HALO:, :]
        y = cb_ref[:, cols] + r * cw_ref[CONV_K - 1:CONV_K, cols]
        for s in range(1, CONV_K):
            rolled = pltpu.roll(r, s, axis=0)
            head = jnp.where(head_row < s, pltpu.roll(prev, s, axis=0), rolled[:CONV_HALO])
            shifted = jnp.concatenate([head, rolled[CONV_HALO:]], axis=0)
            y = y + shifted * cw_ref[CONV_K - 1 - s:CONV_K - s, cols]
        y = y * jax.nn.sigmoid(y)
        if c < 2:
            y = y * (M_QK_DIM ** -0.5)
        out_ref[rows, cols] = y.astype(BF16)

    for name, width in _INPROJ_GROUPS[1:-1]:
        val = group(name, width)
        if name == "o":
            val = jax.nn.sigmoid(val)
        if name in ("ak", "av"):
            half = A_HEAD_DIM
            swapped = [pltpu.roll(val[:, c:c + 2 * half], half, axis=1) for c in range(0, width, 2 * half)]
            val = jnp.concatenate([val] + swapped, axis=1)
        off = _PROJ_OUT_OFFS[name]
        out_ref[rows, off:off + val.shape[1]] = val.astype(BF16)
    gates_ref[:, rows] = group("g", LANES).T[:2 * M_HEADS, :]


def _inproj(layer, x2, g, conv_w, conv_b, w_all, seq_len):
    T = x2.shape[0]
    tm = INPROJ_TM
    row = lambda n: pl.BlockSpec((tm, n), lambda i: (i, 0))
    return pl.pallas_call(
        functools.partial(_inproj_kernel, tiles_per_seq=seq_len // tm),
        grid=(T // tm,),
        in_specs=[row(D_MODEL), _const_spec((1, D_MODEL)), _const_spec((CONV_K, 2 * M_QK_WIDTH)),
                  _const_spec((1, 2 * M_QK_WIDTH)), _layer_spec(layer, (D_MODEL, _INPROJ_WIDTH))],
        out_specs=[row(_PROJ_OUT_WIDTH), pl.BlockSpec((2 * M_HEADS, tm), lambda i: (0, i))],
        out_shape=[jax.ShapeDtypeStruct((T, _PROJ_OUT_WIDTH), BF16),
                   jax.ShapeDtypeStruct((2 * M_HEADS, T), F32)],
        scratch_shapes=[pltpu.VMEM((CONV_HALO, 2 * M_QK_WIDTH), F32)],
        compiler_params=pltpu.CompilerParams(
            dimension_semantics=("arbitrary",), vmem_limit_bytes=VMEM_LIMIT_BYTES),
        name="inproj",
    )(x2, g, conv_w, conv_b, w_all)


def _mlstm_init(c_ref, m_ref):
    @pl.when(pl.program_id(0) == 0)
    def _():
        c_ref[...] = jnp.zeros_like(c_ref)
        m_ref[...] = jnp.zeros_like(m_ref)


def _mlstm_gates(blk, grow_ref, gb_ref, m_ref):
    L = MLSTM_CHUNK
    rows = slice(blk * L, (blk + 1) * L)
    row_i = lax.broadcasted_iota(jnp.int32, (L, L), 0)
    col_j = lax.broadcasted_iota(jnp.int32, (L, L), 1)
    triu_f = (row_i <= col_j).astype(F32)
    g = grow_ref[:, rows] + gb_ref[...]
    logf = pltpu.roll(jax.nn.log_sigmoid(g), M_HEADS, axis=0) * LOG2E
    b = jnp.dot(logf, triu_f, preferred_element_type=F32,
                precision=lax.Precision.HIGHEST)
    a = g * LOG2E - b
    m_prev = m_ref[...]
    m_last = jnp.maximum(m_prev, jnp.max(a, axis=-1, keepdims=True))
    wk = jnp.exp2(a - m_last)
    decay = jnp.exp2(m_prev - m_last)
    m_ref[...] = jnp.sum(logf, axis=-1, keepdims=True) + m_last
    return a, logf, m_prev, wk, decay


def _mlstm_chunk(blk, seqs, ng_ref, between):
    L = MLSTM_CHUNK
    rows = slice(blk * L, (blk + 1) * L)
    row_i = lax.broadcasted_iota(jnp.int32, (L, L), 0)
    col_j = lax.broadcasted_iota(jnp.int32, (L, L), 1)
    tril = col_j <= row_i
    ones_b = jnp.ones((L, LANES), BF16)
    nt = (((1,), (1,)), ((), ()))
    units = [(seq, h) for seq in seqs for h in range(M_HEADS)]
    n_units = range(len(units))

    qb, v_ext, s_raw, k_t, m_prev = [], [], [], [], []
    for u, ((gates, qk_ref, v_ref, _, _, _), h) in enumerate(units):
        qb.append(qk_ref[rows, h * M_QK_DIM:(h + 1) * M_QK_DIM])
        kb = qk_ref[rows, M_QK_WIDTH + h * M_QK_DIM:M_QK_WIDTH + (h + 1) * M_QK_DIM]
        v_ext.append(jnp.concatenate([v_ref[rows, h * M_V_DIM:(h + 1) * M_V_DIM], ones_b], axis=1))
        s_raw.append(lax.dot_general(qb[u], kb, nt, preferred_element_type=F32))
        k_t.append(kb.astype(F32).T)
        m_prev.append(gates[2][h:h + 1, 0:1])
    between[0]()

    both, floor = [], []
    zeros_b = jnp.zeros((M_QK_DIM, M_QK_DIM), BF16)
    for u, ((gates, _, _, _, _, c_ref), h) in enumerate(units):
        a, logf, _, wk, decay_rows = gates
        a_mat = jnp.where(tril, a[h:h + 1, :], -jnp.inf)
        m_row = jnp.maximum(m_prev[u], jnp.max(a_mat, axis=-1, keepdims=True))
        b_col = jnp.sum(jnp.where(tril, logf[h:h + 1, :], 0.0), axis=-1, keepdims=True)
        floor.append(jnp.exp2(-(b_col + m_row)))
        s = s_raw[u] * jnp.exp2(a_mat - m_row)
        q_inter = (qb[u].astype(F32) * jnp.exp2(m_prev[u] - m_row)).astype(BF16)
        kw_t = (k_t[u] * wk[h:h + 1, :]).astype(BF16)
        c_prev = c_ref[h]
        lhs = jnp.concatenate([jnp.concatenate([s.astype(BF16), q_inter], axis=1),
                               jnp.concatenate([kw_t, zeros_b], axis=1)], axis=0)
        res = jnp.dot(lhs, jnp.concatenate([v_ext[u], c_prev.astype(BF16)], axis=0),
                      preferred_element_type=F32)
        both.append(res[:L])
        c_ref[h] = decay_rows[h:h + 1, 0:1] * c_prev + res[L:]
    between[1]()

    hh = []
    for u in n_units:
        inv = 1.0 / jnp.maximum(jnp.abs(both[u][:, M_V_DIM:]), floor[u])
        hh.append(both[u][:, :M_V_DIM] * jnp.concatenate([inv, inv], axis=1))
    msq = [jnp.mean(x * x, axis=-1, keepdims=True) for x in hh]
    between[2]()

    for u, ((_, _, _, og_ref, out_ref, _), h) in enumerate(units):
        sl = slice(h * M_V_DIM, (h + 1) * M_V_DIM)
        hn = hh[u] * lax.rsqrt(msq[u] + EPS) * ng_ref[:, sl]
        out_ref[rows, sl] = hn.astype(BF16) * og_ref[rows, sl]
    between[3]()


def _swa_init(sink_ref, rel_ref, bucket_ref, bias_ref):
    W = WINDOW

    @pl.when(pl.program_id(0) == 0)
    def _():
        bucket = bucket_ref[...]
        neg = jnp.full((W, 2 * W), NEG_BIG, F32)
        for hq in range(A_Q_HEADS):
            bias_ref[1, hq] = neg

        def fill(bb, carry):
            hit = bucket == bb
            for hq in range(A_Q_HEADS):
                bias_ref[1, hq] = jnp.where(hit, rel_ref[bb, hq] * LOG2E, bias_ref[1, hq])
            return carry
        lax.fori_loop(0, N_BUCKETS, fill, 0)
        col = lax.broadcasted_iota(jnp.int32, (W, 2 * W), 1)
        for hq in range(A_Q_HEADS):
            sink = sink_ref[hq] * LOG2E
            bias_ref[0, hq] = jnp.where(col == 0, sink, jnp.where(col < W, NEG_BIG, bias_ref[1, hq]))
            bias_ref[1, hq] = jnp.where(col == 0, sink, bias_ref[1, hq])


def _swa_kv_head(blk, kv, q_ref, kp_ref, kc_ref, vp_ref, vc_ref, out_ref, bias_ref):
    W = WINDOW
    lane_lo = lax.broadcasted_iota(jnp.int32, (2 * W, 2 * A_HEAD_DIM), 1) < A_HEAD_DIM
    not_sink = lax.broadcasted_iota(jnp.int32, (2 * W, 2 * A_HEAD_DIM), 0) > 0
    keep_lo = lane_lo & not_sink
    keep_hi = jnp.logical_not(lane_lo) & not_sink
    zero = jnp.zeros((), BF16)
    ones_lo = jnp.where(lane_lo, 1.0, 0.0).astype(BF16)
    ones_hi = jnp.where(lane_lo, 0.0, 1.0).astype(BF16)
    nt = (((1,), (1,)), ((), ()))
    rows = slice(blk * W, (blk + 1) * W)
    var = jnp.minimum(pl.program_id(0), 1) if blk == 0 else 1
    def tile(ref, rws, swapped):
        c0 = (A_KV_WIDTH if swapped else 0) + (kv // 2) * 2 * A_HEAD_DIM
        return ref[rws, c0:c0 + 2 * A_HEAD_DIM]

    def prev_cur(p_ref, c_ref, swapped):
        prev = (tile(p_ref, slice(None), swapped) if blk == 0
                else tile(c_ref, slice((blk - 1) * W, blk * W), swapped))
        return jnp.concatenate([prev, tile(c_ref, rows, swapped)], axis=0)

    in_low = kv % 2 == 1
    k_even = jnp.where(keep_lo, prev_cur(kp_ref, kc_ref, in_low), zero)
    k_odd = jnp.where(keep_hi, prev_cur(kp_ref, kc_ref, not in_low), zero)
    v_even = jnp.concatenate([jnp.where(keep_lo, prev_cur(vp_ref, vc_ref, in_low), zero), ones_lo], axis=1)
    v_odd = jnp.concatenate([jnp.where(keep_hi, prev_cur(vp_ref, vc_ref, not in_low), zero), ones_hi], axis=1)
    v_both = jnp.concatenate([v_even, v_odd], axis=0)
    pairs = [kv * (A_GROUP // 2) + p for p in range(A_GROUP // 2)]
    cols = [slice(pi * 2 * A_HEAD_DIM, (pi + 1) * 2 * A_HEAD_DIM) for pi in pairs]
    q_st = jnp.concatenate([q_ref[rows, c] for c in cols], axis=0)
    bias_e = jnp.concatenate([bias_ref[var, 2 * pi] for pi in pairs], axis=0)
    bias_o = jnp.concatenate([bias_ref[var, 2 * pi + 1] for pi in pairs], axis=0)
    s_e = lax.dot_general(q_st, k_even, nt, preferred_element_type=F32) + bias_e
    s_o = lax.dot_general(q_st, k_odd, nt, preferred_element_type=F32) + bias_o
    p_e = jnp.exp2(s_e - jnp.max(s_e, axis=-1, keepdims=True)).astype(BF16)
    p_o = jnp.exp2(s_o - jnp.max(s_o, axis=-1, keepdims=True)).astype(BF16)
    acc = jnp.dot(jnp.concatenate([p_e, p_o], axis=1), v_both,
                  preferred_element_type=F32)
    out = (acc[:, :2 * A_HEAD_DIM] / acc[:, 2 * A_HEAD_DIM:]).astype(out_ref.dtype)
    for p, c in enumerate(cols):
        out_ref[rows, c] = out[p * W:(p + 1) * W]


def _mixers_kernel(sink_ref, rel_ref, bucket_ref, qk_ref, v_ref, og_ref, gb_ref, ng_ref,
                   q_ref, kp_ref, kc_ref, vp_ref, vc_ref, *refs):
    n_seq = qk_ref.shape[0]
    grow_refs = refs[:n_seq]
    hm_ref, ha_ref, c_ref, m_ref, bias_ref = refs[n_seq:]
    _mlstm_init(c_ref, m_ref)
    _swa_init(sink_ref, rel_ref, bucket_ref, bias_ref)
    for blk in range(SWA_BLOCKS):
        seqs = [(_mlstm_gates(blk, grow_refs[b], gb_ref, m_ref.at[b]),
                 qk_ref.at[b], v_ref.at[b], og_ref.at[b], hm_ref.at[b], c_ref.at[b])
                for b in range(n_seq)]
        swa = [functools.partial(_swa_kv_head, blk, u, q_ref.at[b], kp_ref.at[b], kc_ref.at[b],
                                 vp_ref.at[b], vc_ref.at[b], ha_ref.at[b], bias_ref)
               for b in range(n_seq) for u in range(A_KV_HEADS)]
        cuts = np.cumsum((0,) + SWA_UNITS_PER_GAP)
        assert cuts[-1] == len(swa)
        _mlstm_chunk(blk, seqs, ng_ref,
                     between=[lambda g=g: [f() for f in swa[cuts[g]:cuts[g + 1]]] for g in range(4)])


def _proj_cols(name):
    width = dict(_PROJ_OUT)[name]
    off = _PROJ_OUT_OFFS[name]
    assert off % width == 0
    return width, off // width


def _mixers(proj, g_rows, gate_b, norm_g, sinks, rel_bias, bucket):
    B, S, _ = proj.shape
    W = WINDOW
    tq = SWA_BLOCKS * W
    assert tq == MLSTM_BLOCKS * MLSTM_CHUNK

    def cur(name):
        n, j = _proj_cols(name)
        return pl.BlockSpec((B, tq, n), lambda i: (0, i, j))

    def prev(name):
        n, j = _proj_cols(name)
        return pl.BlockSpec((B, W, n), lambda i: (0, jnp.maximum(i * SWA_BLOCKS - 1, 0), j))

    out = lambda n: pl.BlockSpec((B, tq, n), lambda i: (0, i, 0))
    gate_rows = [pl.BlockSpec((2 * M_HEADS, tq), lambda i, b=b: (0, b * (S // tq) + i)) for b in range(B)]
    smem = pl.BlockSpec(memory_space=pltpu.SMEM)
    return pl.pallas_call(
        _mixers_kernel,
        grid=(S // tq,),
        in_specs=[smem, smem, _const_spec((W, 2 * W)),
                  cur("qk"), cur("v"), cur("o"),
                  _const_spec((2 * M_HEADS, 1)), _const_spec((1, M_V_WIDTH)),
                  cur("aq"), prev("ak"), cur("ak"), prev("av"), cur("av")] + gate_rows,
        out_specs=[out(M_V_WIDTH), out(A_Q_WIDTH)],
        out_shape=[jax.ShapeDtypeStruct((B, S, M_V_WIDTH), BF16),
                   jax.ShapeDtypeStruct((B, S, A_Q_WIDTH), BF16)],
        scratch_shapes=[pltpu.VMEM((B, M_HEADS, M_QK_DIM, M_V_DIM + LANES), F32),
                        pltpu.VMEM((B, 2 * M_HEADS, MLSTM_CHUNK), F32),
                        pltpu.VMEM((2, A_Q_HEADS, W, 2 * W), F32)],
        compiler_params=pltpu.CompilerParams(
            dimension_semantics=("arbitrary",), vmem_limit_bytes=VMEM_LIMIT_BYTES),
        name="mixers",
    )(sinks, rel_bias, bucket, proj, proj, proj, gate_b, norm_g, proj, proj, proj, proj, proj,
      *([g_rows] * B))


def _post_kernel(x_ref, hm_ref, ha_ref, ga_ref, gb_ref, wm_ref, wa_ref, wo_ref,
                 g_ref, wu_ref, wd_ref, fg_ref, out_ref, *, final):
    sub = PROJ_SUB
    for t in range(x_ref.shape[0] // sub):
        rows = slice(t * sub, (t + 1) * sub)
        y = (jax.nn.sigmoid(ga_ref[rows, :].astype(F32))
             * jnp.dot(hm_ref[rows, :], wm_ref[...], preferred_element_type=F32)
             + jax.nn.sigmoid(gb_ref[rows, :].astype(F32))
             * jnp.dot(ha_ref[rows, :], wa_ref[...], preferred_element_type=F32))
        x = x_ref[rows, :] + jnp.dot(y.astype(BF16), wo_ref[...], preferred_element_type=F32)
        h = _rmsnorm(x, g_ref[...]).astype(BF16)
        u = jnp.maximum(jnp.dot(h, wu_ref[...], preferred_element_type=F32), 0.0)
        x = x + jnp.dot((u * u).astype(BF16), wd_ref[...], preferred_element_type=F32)
        if final:
            x = _rmsnorm(x, fg_ref[...])
        out_ref[rows, :] = x


def _post(layer, x2, hm, ha, proj, wm, wa, wo, g, wu, wd, fg, final):
    T = x2.shape[0]
    tm = POST_TM
    row = pl.BlockSpec((tm, D_MODEL), lambda i: (i, 0))
    gate = lambda name: pl.BlockSpec((tm, D_MODEL), lambda i: (i, _proj_cols(name)[1]))
    wspec = _layer_spec(layer, (D_MODEL, D_MODEL))
    vec = _const_spec((1, D_MODEL))
    return pl.pallas_call(
        functools.partial(_post_kernel, final=final),
        grid=(T // tm,),
        in_specs=[row, row, row, gate("ga"), gate("gb"), wspec, wspec, wspec,
                  vec, _layer_spec(layer, (D_MODEL, D_FF)), _layer_spec(layer, (D_FF, D_MODEL)), vec],
        out_specs=row,
        out_shape=jax.ShapeDtypeStruct((T, D_MODEL), F32),
        compiler_params=pltpu.CompilerParams(
            dimension_semantics=("parallel",), vmem_limit_bytes=VMEM_LIMIT_BYTES),
        name="post",
    )(x2, hm, ha, proj, proj, wm, wa, wo, g, wu, wd, fg)


def _t5_bucket(n):
    max_exact = N_BUCKETS // 2
    n = np.maximum(n, 0)
    large = max_exact + (np.log(np.maximum(n, 1) / max_exact)
                         / np.log(MAX_DISTANCE / max_exact)
                         * (N_BUCKETS - max_exact)).astype(np.int32)
    large = np.minimum(large, N_BUCKETS - 1)
    return np.where(n < max_exact, n, large).astype(np.int32)


def _bucket_table():
    W = WINDOW
    dist = np.arange(W)[:, None] + W - np.arange(2 * W)[None, :]
    valid = (dist >= 0) & (dist < W)
    return jnp.asarray(np.where(valid, _t5_bucket(dist), -1).astype(np.int32))


def _regroup_kernel(w_ref, out_ref):
    n_gate = 2 * M_HEADS
    gate0 = sum(IN_SIZES[:3])
    tail0 = gate0 + n_gate
    n_tail = sum(IN_SIZES[5:])
    step = REGROUP_PIECE
    for src, dst, n, scale in ((0, 0, gate0, 1.0),
                               (tail0, gate0, A_Q_WIDTH, A_HEAD_DIM ** -0.5 * LOG2E),
                               (tail0 + A_Q_WIDTH, gate0 + A_Q_WIDTH, n_tail - A_Q_WIDTH, 1.0)):
        for o in range(0, n, step):
            piece = w_ref[src + o:src + o + step, :]
            if scale != 1.0:
                piece = piece * scale
            out_ref[:, dst + o:dst + o + step] = piece.T.astype(BF16)
    gate_rows = jnp.concatenate(
        [w_ref[gate0:tail0, :], jnp.zeros((LANES - n_gate, w_ref.shape[1]), F32)], axis=0)
    out_ref[:, gate0 + n_tail:] = gate_rows.T.astype(BF16)


def _regroup_w_in(w_in):
    w_t = jnp.swapaxes(w_in, 1, 2)
    depth, n_in, d = w_t.shape
    tc = REGROUP_COLS
    return pl.pallas_call(
        _regroup_kernel,
        grid=(depth, d // tc),
        in_specs=[pl.BlockSpec((None, n_in, tc), lambda l, i: (l, 0, i))],
        out_specs=pl.BlockSpec((None, tc, _INPROJ_WIDTH), lambda l, i: (l, i, 0)),
        out_shape=jax.ShapeDtypeStruct((depth, d, _INPROJ_WIDTH), BF16),
        compiler_params=pltpu.CompilerParams(
            dimension_semantics=("parallel", "parallel"), vmem_limit_bytes=VMEM_LIMIT_BYTES),
        name="regroup",
    )(w_t)


def kernel(x, norm_mix_g, w_in, conv_w, conv_b, b_igate, b_fgate, mlstm_norm_g, attn_sinks,
           rel_bias, w_branch_m, w_branch_a, w_out, norm_mlp_g, w_up, w_down, final_norm_g):
    B, S, D = x.shape
    T = B * S
    bucket = _bucket_table()
    w_all = _regroup_w_in(w_in)
    wm, wa, wo, wu, wd = (w.astype(BF16) for w in (w_branch_m, w_branch_a, w_out, w_up, w_down))
    x2 = x.reshape(T, D)
    for l in range(DEPTH):
        proj, g_rows = _inproj(
            l, x2, norm_mix_g[l].reshape(1, D), conv_w[l], conv_b[l].reshape(1, -1), w_all, S)
        gate_b = jnp.concatenate([b_igate[l], b_fgate[l]]).reshape(2 * M_HEADS, 1)
        hm, ha = _mixers(proj.reshape(B, S, -1), g_rows, gate_b, mlstm_norm_g[l].reshape(1, -1),
                         attn_sinks[l], rel_bias, bucket)
        x2 = _post(l, x2, hm.reshape(T, -1), ha.reshape(T, -1), proj, wm, wa, wo,
                   norm_mlp_g[l].reshape(1, D), wu, wd,
                   final_norm_g.reshape(1, D), final=(l == DEPTH - 1))
    return x2.reshape(B, S, D)
```
